```python
import math
import jax, jax.numpy as jnp
from jax import lax
import numpy as np

D_MODEL = 1024
BATCH = 8
SEQ = 2048
DEPTH = 1
DEC_BATCH = 32
DEC_SEQ = 1
PAST_LEN = 8192
PAGE_SIZE = 128

N_META = 16
ATTN_WIDTH = D_MODEL // 2
CONV_DIM = D_MODEL - ATTN_WIDTH
N_DIFF_HEADS = 4
HEAD_DIM = ATTN_WIDTH // N_DIFF_HEADS // 2
V_HEAD_DIM = 2 * HEAD_DIM
CONV_WIDTH = 31
FFN_HIDDEN = -(-8 * D_MODEL // (3 * 256)) * 256
IN_COLS = 3 * ATTN_WIDTH + 2 * CONV_DIM
Q_BLOCK = 128
EPS = 1e-6
NEG_INF = -1e30

kernel_name = "hymba_diffattn_conformer_decoder_step"


def rms_norm(x, w):
    xf = x.astype(jnp.float32)
    y = xf * lax.rsqrt(jnp.mean(xf * xf, axis=-1, keepdims=True) + EPS)
    return (y * w.astype(jnp.float32)).astype(x.dtype)


def layer_norm(x, w, b):
    xf = x.astype(jnp.float32)
    mu = jnp.mean(xf, axis=-1, keepdims=True)
    xc = xf - mu
    y = xc * lax.rsqrt(jnp.mean(xc * xc, axis=-1, keepdims=True) + EPS)
    return (y * w.astype(jnp.float32) + b.astype(jnp.float32)).astype(x.dtype)


def alibi_slopes():
    h = jnp.arange(1, N_DIFF_HEADS + 1, dtype=jnp.float32)
    return jnp.exp2(-8.0 * h / N_DIFF_HEADS)


def lambda_init_of(li):
    return 0.8 - 0.6 * math.exp(-0.3 * li)


def diff_lambda(lw, li):
    f = jnp.float32
    return (jnp.exp(jnp.sum(lw['lq1'].astype(f) * lw['lk1'].astype(f)))
            - jnp.exp(jnp.sum(lw['lq2'].astype(f) * lw['lk2'].astype(f)))
            + lambda_init_of(li))


def split_proj(p):
    B, L, _ = p.shape
    q = p[..., :ATTN_WIDTH].reshape(B, L, N_DIFF_HEADS, 2, HEAD_DIM)
    k = p[..., ATTN_WIDTH:2 * ATTN_WIDTH].reshape(B, L, N_DIFF_HEADS, 2, HEAD_DIM)
    v = p[..., 2 * ATTN_WIDTH:3 * ATTN_WIDTH].reshape(B, L, N_DIFF_HEADS, V_HEAD_DIM)
    a = p[..., 3 * ATTN_WIDTH:3 * ATTN_WIDTH + CONV_DIM]
    g = p[..., 3 * ATTN_WIDTH + CONV_DIM:]
    u = a * jax.nn.sigmoid(g)
    return q, k, v, u


def diff_attend(q, k, v, q_pos, k_pos, lam, slopes):
    s = jnp.einsum('bqhjd,bkhjd->bhjqk', q, k,
                   preferred_element_type=jnp.float32) * (HEAD_DIM ** -0.5)
    dist = (q_pos[:, None] - k_pos[None, :]).astype(jnp.float32)
    s = s - slopes[None, :, None, None, None] * dist
    s = jnp.where((k_pos[None, :] <= q_pos[:, None])[None, None, None], s, NEG_INF)
    p = jax.nn.softmax(s, axis=-1)
    a = p[:, :, 0] - lam * p[:, :, 1]
    return jnp.einsum('bhqk,bkhe->bqhe', a.astype(v.dtype), v)


def diff_head_out(o, lw, li):
    o = rms_norm(o, lw['subln']) * (1.0 - lambda_init_of(li))
    return o.reshape(o.shape[:2] + (ATTN_WIDTH,))


def causal_depthwise(u_padded, w, b):
    y = lax.conv_general_dilated(u_padded, w[:, None, :].astype(u_padded.dtype),
                                 window_strides=(1,), padding='VALID',
                                 dimension_numbers=('NWC', 'WIO', 'NWC'),
                                 feature_group_count=CONV_DIM)
    return y + b


def finish_layer(x_res, attn_o, conv_y, lw, li):
    attn_out = diff_head_out(attn_o, lw, li)
    conv_out = jax.nn.silu(layer_norm(conv_y, lw['cln_w'], lw['cln_b']))
    mix = jnp.concatenate([attn_out, conv_out], axis=-1) @ lw['w_out']
    x = x_res + rms_norm(mix, lw['ln_mix_post'])
    h = rms_norm(x, lw['ln_ffn_pre'])
    f = (jax.nn.silu(h @ lw['w_gate']) * (h @ lw['w_up'])) @ lw['w_down']
    return x + rms_norm(f, lw['ln_ffn_post'])


def prompt_layer(x, lw, li, last):
    B, T, _ = x.shape
    n_real = T - N_META
    nb = n_real // Q_BLOCK
    proj = rms_norm(x, lw['ln_mix_pre']) @ lw['w_in']
    q, k, v, u = split_proj(proj)
    lam = diff_lambda(lw, li)
    slopes = alibi_slopes()
    k_pos = jnp.arange(T)
    q_blocks = q[:, N_META:].reshape(B, nb, Q_BLOCK, N_DIFF_HEADS, 2, HEAD_DIM).transpose(1, 0, 2, 3, 4, 5)
    qpos_blocks = (N_META + jnp.arange(n_real)).reshape(nb, Q_BLOCK)
    o_blocks = lax.map(lambda a: diff_attend(a[0], k, v, a[1], k_pos, lam, slopes),
                       (q_blocks, qpos_blocks))
    o = o_blocks.transpose(1, 0, 2, 3, 4).reshape(B, n_real, N_DIFF_HEADS, V_HEAD_DIM)
    r0 = N_META
    if not last:
        o_meta = diff_attend(q[:, :N_META], k[:, :N_META], v[:, :N_META],
                             k_pos[:N_META], k_pos[:N_META], lam, slopes)
        o = jnp.concatenate([o_meta, o], axis=1)
        r0 = 0
    u_pad = jnp.pad(u, ((0, 0), (CONV_WIDTH - 1, 0), (0, 0)))
    y = causal_depthwise(u_pad, lw['conv_w'], lw['conv_b'])
    x_out = finish_layer(x[:, r0:], o, y[:, r0:], lw, li)
    new_k = k.reshape(B, T, 2 * N_DIFF_HEADS, HEAD_DIM)
    new_conv = u[:, T - (CONV_WIDTH - 1):]
    return x_out, new_k, v, new_conv


def sample_layer(x, ck, cv, conv_state, page_table, lw, li):
    Bd, S, _ = x.shape
    past = page_table.shape[1] * PAGE_SIZE
    proj = rms_norm(x, lw['ln_mix_pre']) @ lw['w_in']
    q, k, v, u = split_proj(proj)
    lam = diff_lambda(lw, li)
    k_past = ck[page_table].reshape(Bd, past, N_DIFF_HEADS, 2, HEAD_DIM).astype(k.dtype)
    v_past = cv[page_table].reshape(Bd, past, N_DIFF_HEADS, V_HEAD_DIM).astype(v.dtype)
    k_all = jnp.concatenate([k_past, k], axis=1)
    v_all = jnp.concatenate([v_past, v], axis=1)
    k_pos = jnp.arange(past + S)
    q_pos = past + jnp.arange(S)
    o = diff_attend(q, k_all, v_all, q_pos, k_pos, lam, alibi_slopes())
    u_full = jnp.concatenate([conv_state.astype(u.dtype), u], axis=1)
    y = causal_depthwise(u_full, lw['conv_w'], lw['conv_b'])
    x_out = finish_layer(x, o, y, lw, li)
    new_k = k.reshape(Bd, S, 2 * N_DIFF_HEADS, HEAD_DIM)
    new_conv = u_full[:, S:]
    return x_out, new_k, v, new_conv


def setup_inputs(seed: int = 0) -> dict:
    key = jax.random.key(seed)
    ks = jax.random.split(key, 32)
    f = jnp.float32
    n_pages = PAST_LEN // PAGE_SIZE
    n_used = DEC_BATCH * n_pages
    n_pool = n_used + max(1, n_used // 4)
    page_table = jax.random.permutation(ks[0], n_pool)[:n_used].reshape(DEC_BATCH, n_pages).astype(jnp.int32)
    nrm = lambda i, shape, s: (jax.random.normal(ks[i], shape, f) * s).astype(f)
    gain = lambda i, shape: (1.0 + 0.05 * jax.random.normal(ks[i], shape, f)).astype(f)
    return {
        'x_prompt': nrm(1, (BATCH, SEQ, D_MODEL), 1.0),
        'x_sample': nrm(2, (DEC_BATCH, DEC_SEQ, D_MODEL), 1.0),
        'cache_k': nrm(3, (DEPTH, n_pool, PAGE_SIZE, 2 * N_DIFF_HEADS, HEAD_DIM), 1.0),
        'cache_v': nrm(4, (DEPTH, n_pool, PAGE_SIZE, N_DIFF_HEADS, V_HEAD_DIM), 1.0),
        'state_conv': nrm(5, (DEPTH, DEC_BATCH, CONV_WIDTH - 1, CONV_DIM), 0.5),
        'page_table': page_table,
        'meta_tokens': nrm(6, (N_META, D_MODEL), 1.0),
        'ln_mix_pre': gain(7, (DEPTH, D_MODEL)),
        'ln_mix_post': gain(8, (DEPTH, D_MODEL)),
        'w_in': nrm(9, (DEPTH, D_MODEL, IN_COLS), D_MODEL ** -0.5),
        'lambda_q1': nrm(10, (DEPTH, HEAD_DIM), 0.1),
        'lambda_k1': nrm(11, (DEPTH, HEAD_DIM), 0.1),
        'lambda_q2': nrm(12, (DEPTH, HEAD_DIM), 0.1),
        'lambda_k2': nrm(13, (DEPTH, HEAD_DIM), 0.1),
        'subln_w': gain(14, (DEPTH, V_HEAD_DIM)),
        'conv_w': nrm(15, (DEPTH, CONV_WIDTH, CONV_DIM), CONV_WIDTH ** -0.5),
        'conv_b': nrm(16, (DEPTH, CONV_DIM), 0.02),
        'conv_ln_w': gain(17, (DEPTH, CONV_DIM)),
        'conv_ln_b': nrm(18, (DEPTH, CONV_DIM), 0.02),
        'w_out': nrm(19, (DEPTH, D_MODEL, D_MODEL), D_MODEL ** -0.5),
        'ln_ffn_pre': gain(20, (DEPTH, D_MODEL)),
        'ln_ffn_post': gain(21, (DEPTH, D_MODEL)),
        'w_gate': nrm(22, (DEPTH, D_MODEL, FFN_HIDDEN), D_MODEL ** -0.5),
        'w_up': nrm(23, (DEPTH, D_MODEL, FFN_HIDDEN), D_MODEL ** -0.5),
        'w_down': nrm(24, (DEPTH, FFN_HIDDEN, D_MODEL), FFN_HIDDEN ** -0.5),
    }


def reference(x_prompt, x_sample, cache_k, cache_v, state_conv, page_table, meta_tokens,
              ln_mix_pre, ln_mix_post, w_in, lambda_q1, lambda_k1, lambda_q2, lambda_k2,
              subln_w, conv_w, conv_b, conv_ln_w, conv_ln_b, w_out, ln_ffn_pre, ln_ffn_post,
              w_gate, w_up, w_down):
    B = x_prompt.shape[0]
    meta = jnp.broadcast_to(meta_tokens[None].astype(x_prompt.dtype), (B, N_META, D_MODEL))
    xp = jnp.concatenate([meta, x_prompt], axis=1)
    xs = x_sample
    kp_l, vp_l, cp_l, ks_l, vs_l, cs_l = [], [], [], [], [], []
    for li in range(DEPTH):
        lw = dict(ln_mix_pre=ln_mix_pre[li], ln_mix_post=ln_mix_post[li], w_in=w_in[li],
                  lq1=lambda_q1[li], lk1=lambda_k1[li], lq2=lambda_q2[li], lk2=lambda_k2[li],
                  subln=subln_w[li], conv_w=conv_w[li], conv_b=conv_b[li],
                  cln_w=conv_ln_w[li], cln_b=conv_ln_b[li], w_out=w_out[li],
                  ln_ffn_pre=ln_ffn_pre[li], ln_ffn_post=ln_ffn_post[li],
                  w_gate=w_gate[li], w_up=w_up[li], w_down=w_down[li])
        last = li == DEPTH - 1
        xp, kp, vp, cp = prompt_layer(xp, lw, li, last)
        xs, ks_, vs_, cs_ = sample_layer(xs, cache_k[li], cache_v[li], state_conv[li], page_table, lw, li)
        kp_l.append(kp); vp_l.append(vp); cp_l.append(cp)
        ks_l.append(ks_); vs_l.append(vs_); cs_l.append(cs_)
    return (xp, xs, jnp.stack(kp_l), jnp.stack(vp_l), jnp.stack(cp_l),
            jnp.stack(ks_l), jnp.stack(vs_l), jnp.stack(cs_l))
```

```python
import functools
import math

import jax
import jax.numpy as jnp
from jax import lax
from jax.experimental import pallas as pl
from jax.experimental.pallas import tpu as pltpu

D_MODEL = 1024
N_META = 16
ATTN_WIDTH = 512
CONV_DIM = 512
N_HEADS = 4
HEAD_DIM = 64
V_HEAD_DIM = 128
N_MAPS = 2 * N_HEADS
CONV_WIDTH = 31
FFN_HIDDEN = 2816
PAGE_SIZE = 128
EPS = 1e-6
NEG_INF = -1e30
LAMBDA_INIT = 0.8 - 0.6 * math.exp(-0.3 * 0)
SCALE = HEAD_DIM ** -0.5
SLOPES = tuple(2.0 ** (-8.0 * (h + 1) / N_HEADS) for h in range(N_HEADS))

VMEM_LIMIT_BYTES = 56 * 1024 * 1024

F32 = jnp.float32
BF16 = jnp.bfloat16


def _rms(x, g):
    return x * lax.rsqrt(jnp.mean(x * x, axis=-1, keepdims=True) + EPS) * g


def _const_spec(shape):
    nd = len(shape)
    return pl.BlockSpec(shape, lambda *_: (0,) * nd, pipeline_mode=pl.Buffered(1))


def _project(x, g, w_ref):
    xn = _rms(x, g).astype(BF16)
    seg = lambda c: jnp.dot(xn, w_ref[:, c * 512:(c + 1) * 512], preferred_element_type=F32)
    q = seg(0) * SCALE
    k = seg(1)
    v = seg(2)
    u = seg(3) * jax.nn.sigmoid(seg(4))
    return q, k, v, u


def _proj_kernel(x_ref, g_ref, w_ref, q_ref, k_ref, v_ref, u_ref):
    q, k, v, u = _project(x_ref[...], g_ref[...], w_ref)
    q_ref[...] = q
    k_ref[...] = k
    v_ref[...] = v
    u_ref[...] = u


def _proj_call(x, g, w_bf):
    rows = x.shape[0]
    out = jax.ShapeDtypeStruct((rows, 512), F32)
    return pl.pallas_call(
        _proj_kernel,
        out_shape=(out, out, out, out),
        name="proj_small",
        compiler_params=pltpu.CompilerParams(vmem_limit_bytes=VMEM_LIMIT_BYTES),
    )(x, g, w_bf)


PROJ_TM = 512
U_PAD = 48
U_ROWS_OFF = U_PAD - (CONV_WIDTH - 1)


def _prompt_proj_kernel(x_ref, g_ref, w_ref, mk_ref, mv_ref, mu_ref,
                        q_ref, kb_ref, vb_ref, nk_ref, nv_ref, u_ref):
    i = pl.program_id(1)

    @pl.when(i == 0)
    def _():
        nk_ref[0, 0:N_META, :] = mk_ref[...]
        nv_ref[0, 0:N_META, :] = mv_ref[...]
        u_ref[0, 0:U_PAD - N_META, :] = jnp.zeros((U_PAD - N_META, CONV_DIM), F32)
        u_ref[0, U_PAD - N_META:U_PAD, :] = mu_ref[...]

    q, k, v, u = _project(x_ref[0], g_ref[...], w_ref)
    q_ref[0] = q.astype(BF16)
    kb_ref[0] = k.astype(BF16)
    vb_ref[0] = v.astype(BF16)
    r = i * PROJ_TM
    nk_ref[0, pl.ds(pl.multiple_of(r + N_META, 8), PROJ_TM), :] = k
    nv_ref[0, pl.ds(pl.multiple_of(r + N_META, 8), PROJ_TM), :] = v
    u_ref[0, pl.ds(pl.multiple_of(r + U_PAD, 8), PROJ_TM), :] = u


def _prompt_proj_call(x, g, w_bf, mk, mv, mu):
    B, S, _ = x.shape
    T = S + N_META
    nt = S // PROJ_TM
    tile = pl.BlockSpec((1, PROJ_TM, 512), lambda b, i: (b, i, 0))
    full = lambda rows: pl.BlockSpec((1, rows, 512), lambda b, i: (b, 0, 0))
    bf = jax.ShapeDtypeStruct((B, S, 512), BF16)
    return pl.pallas_call(
        _prompt_proj_kernel,
        grid=(B, nt),
        in_specs=[
            pl.BlockSpec((1, PROJ_TM, D_MODEL), lambda b, i: (b, i, 0)),
            _const_spec((1, D_MODEL)),
            _const_spec(w_bf.shape),
            _const_spec((N_META, 512)),
            _const_spec((N_META, 512)),
            _const_spec((N_META, 512)),
        ],
        out_specs=(tile, tile, tile, full(T), full(T), full(S + U_PAD)),
        out_shape=(bf, bf, bf,
                   jax.ShapeDtypeStruct((B, T, 512), F32),
                   jax.ShapeDtypeStruct((B, T, 512), F32),
                   jax.ShapeDtypeStruct((B, S + U_PAD, 512), F32)),
        name="prompt_proj",
        compiler_params=pltpu.CompilerParams(
            dimension_semantics=("arbitrary", "arbitrary"),
            vmem_limit_bytes=VMEM_LIMIT_BYTES),
    )(x, g, w_bf, mk, mv, mu)


CONV_TM = 512
CONV_SUB = 32


def _ln_swish(y, w, b):
    mu = jnp.mean(y, axis=-1, keepdims=True)
    yc = y - mu
    z = yc * lax.rsqrt(jnp.mean(yc * yc, axis=-1, keepdims=True) + EPS) * w + b
    return z * jax.nn.sigmoid(z)


def _conv_kernel(u_ref, cw_ref, cb_ref, lw_ref, lb_ref, o_ref):
    i = pl.program_id(1)
    lead = U_ROWS_OFF % 8
    base = i * CONV_TM + U_ROWS_OFF - lead

    def body(s, carry):
        r = s * CONV_SUB
        win = u_ref[0, pl.ds(pl.multiple_of(base + r, 8), CONV_SUB + 32), :]
        acc = jnp.broadcast_to(cb_ref[...], (CONV_SUB, CONV_DIM))
        for w in range(CONV_WIDTH):
            acc = acc + win[lead + w:lead + w + CONV_SUB, :] * cw_ref[w:w + 1, :]
        z = _ln_swish(acc, lw_ref[...], lb_ref[...])
        o_ref[0, pl.ds(pl.multiple_of(r, CONV_SUB), CONV_SUB), :] = z.astype(BF16)
        return carry

    lax.fori_loop(0, CONV_TM // CONV_SUB, body, 0)


def _conv_call(u_ext, cw, cb, lw, lb):
    B, R, _ = u_ext.shape
    S = R - U_PAD
    return pl.pallas_call(
        _conv_kernel,
        grid=(B, S // CONV_TM),
        in_specs=[
            pl.BlockSpec((1, R, CONV_DIM), lambda b, i: (b, 0, 0)),
            _const_spec((CONV_WIDTH, CONV_DIM)),
            _const_spec((1, CONV_DIM)),
            _const_spec((1, CONV_DIM)),
            _const_spec((1, CONV_DIM)),
        ],
        out_specs=pl.BlockSpec((1, CONV_TM, CONV_DIM), lambda b, i: (b, i, 0)),
        out_shape=jax.ShapeDtypeStruct((B, S, CONV_DIM), BF16),
        name="conv",
        compiler_params=pltpu.CompilerParams(
            dimension_semantics=("arbitrary", "arbitrary"),
            vmem_limit_bytes=VMEM_LIMIT_BYTES),
    )(u_ext, cw, cb, lw, lb)


def _sample_conv_kernel(st_ref, u_ref, cw_ref, cb_ref, lw_ref, lb_ref, o_ref, y_ref):
    past_w = cw_ref[0:CONV_WIDTH - 1, :]
    for b in range(st_ref.shape[0]):
        y_ref[b:b + 1, :] = jnp.sum(st_ref[b] * past_w, axis=0, keepdims=True)
    acc = y_ref[...] + cb_ref[...] + u_ref[...] * cw_ref[CONV_WIDTH - 1:CONV_WIDTH, :]
    o_ref[...] = _ln_swish(acc, lw_ref[...], lb_ref[...]).astype(BF16)


def _sample_conv_call(state, u, cw, cb, lw, lb):
    return pl.pallas_call(
        _sample_conv_kernel,
        out_shape=jax.ShapeDtypeStruct(u.shape, BF16),
        scratch_shapes=[pltpu.VMEM(u.shape, F32)],
        name="sample_conv",
    )(state, u, cw, cb, lw, lb)


def _lambda(lq1, lk1, lq2, lk2):
    s1 = jnp.sum(lq1 * lk1, axis=-1, keepdims=True)
    s2 = jnp.sum(lq2 * lk2, axis=-1, keepdims=True)
    return jnp.exp(s1) - jnp.exp(s2) + LAMBDA_INIT


def _head_out(o0, o1, lam, subln):
    o = o0 - lam * o1
    return _rms(o, subln) * (1.0 - LAMBDA_INIT)


ATT_TQ = 256
ATT_TK = 256
META_BLK = 128


def _augment(x, pos, is_query, slope):
    lane = lax.broadcasted_iota(jnp.int32, x.shape, 1)
    hi = (pos >> 6).astype(F32)
    lo = (pos & 63).astype(F32)
    out = []
    for j in range(2):
        a = lane - (64 if j == 0 else 0)
        if is_query:
            aug = jnp.where(a == 0, hi * (-64.0 * slope),
                  jnp.where(a == 1, lo * (-slope),
                  jnp.where(a == 2, 64.0 * slope,
                  jnp.where(a == 3, slope, 0.0))))
        else:
            aug = jnp.where(a == 0, 1.0,
                  jnp.where(a == 1, 1.0,
                  jnp.where(a == 2, hi,
                  jnp.where(a == 3, lo, 0.0))))
        own = (lane < 64) if j == 0 else (lane >= 64)
        out.append(jnp.where(own, x, aug).astype(BF16))
    return out


def _softmax_step(qe, kc, vc, m, l, acc, mask):
    s = lax.dot_general(qe, kc, (((1,), (1,)), ((), ())), preferred_element_type=F32)
    if mask is not None:
        s = jnp.where(mask, s, NEG_INF)
    m_new = jnp.maximum(m, jnp.max(s, axis=-1, keepdims=True))
    alpha = jnp.exp(m - m_new)
    p = jnp.exp(s - m_new)
    l = alpha * l + jnp.sum(p, axis=-1, keepdims=True)
    acc = alpha * acc + jnp.dot(p.astype(BF16), vc, preferred_element_type=F32)
    return m_new, l, acc


def _attn_kernel(q_ref, kb_ref, vb_ref, mkb_ref, mvb_ref,
                 lq1_ref, lk1_ref, lq2_ref, lk2_ref, sub_ref,
                 o_ref, kexp_ref, qexp_ref):
    qi = pl.program_id(1)
    S = kb_ref.shape[1]

    @pl.when(qi == 0)
    def _():
        def put(rows_off, kc, pos0):
            rows = kc.shape[0]
            pos = pos0 + lax.broadcasted_iota(jnp.int32, (rows, 128), 0)
            for h in range(N_HEADS):
                e0, e1 = _augment(kc[:, h * 128:(h + 1) * 128].astype(F32), pos, False, 0.0)
                kexp_ref[rows_off:rows_off + rows, (2 * h) * 128:(2 * h + 1) * 128] = e0
                kexp_ref[rows_off:rows_off + rows, (2 * h + 1) * 128:(2 * h + 2) * 128] = e1

        put(0, mkb_ref[...], 0)
        for c in range(S // ATT_TK):
            put(META_BLK + c * ATT_TK, kb_ref[0, c * ATT_TK:(c + 1) * ATT_TK, :],
                N_META + c * ATT_TK)

    qpos = N_META + qi * ATT_TQ + lax.broadcasted_iota(jnp.int32, (ATT_TQ, 128), 0)
    q = q_ref[0].astype(F32)
    for h in range(N_HEADS):
        e0, e1 = _augment(q[:, h * 128:(h + 1) * 128], qpos, True, SLOPES[h])
        qexp_ref[:, (2 * h) * 128:(2 * h + 1) * 128] = e0
        qexp_ref[:, (2 * h + 1) * 128:(2 * h + 2) * 128] = e1

    lam = _lambda(lq1_ref[...], lk1_ref[...], lq2_ref[...], lk2_ref[...])
    meta_valid = lax.broadcasted_iota(jnp.int32, (ATT_TQ, META_BLK), 1) < N_META
    row = lax.broadcasted_iota(jnp.int32, (ATT_TQ, ATT_TK), 0)
    col = lax.broadcasted_iota(jnp.int32, (ATT_TQ, ATT_TK), 1)
    causal = col <= row

    for h in range(N_HEADS):
        vcols = slice(h * 128, (h + 1) * 128)
        outs = []
        for j in range(2):
            mcols = slice((2 * h + j) * 128, (2 * h + j + 1) * 128)
            qe = qexp_ref[:, mcols]
            init = (jnp.full((ATT_TQ, 1), NEG_INF, F32), jnp.zeros((ATT_TQ, 1), F32),
                    jnp.zeros((ATT_TQ, V_HEAD_DIM), F32))
            m, l, acc = _softmax_step(qe, kexp_ref[0:META_BLK, mcols], mvb_ref[:, vcols],
                                      *init, meta_valid)

            def body(c, carry):
                r = pl.multiple_of(c * ATT_TK, ATT_TK)
                kc = kexp_ref[pl.ds(META_BLK + r, ATT_TK), mcols]
                vc = vb_ref[0, pl.ds(r, ATT_TK), vcols]
                return _softmax_step(qe, kc, vc, *carry, None)

            m, l, acc = lax.fori_loop(0, qi, body, (m, l, acc))
            r = pl.multiple_of(qi * ATT_TK, ATT_TK)
            m, l, acc = _softmax_step(qe, kexp_ref[pl.ds(META_BLK + r, ATT_TK), mcols],
                                      vb_ref[0, pl.ds(r, ATT_TK), vcols], m, l, acc, causal)
            outs.append(acc / l)
        o_ref[0, :, vcols] = _head_out(outs[0], outs[1], lam, sub_ref[...]).astype(BF16)


def _attn_call(q, kb, vb, mkb, mvb, lq1, lk1, lq2, lk2, subln):
    B, S, _ = q.shape
    per_b = pl.BlockSpec((1, S, 512), lambda b, i: (b, 0, 0))
    tile = pl.BlockSpec((1, ATT_TQ, 512), lambda b, i: (b, i, 0))
    small = _const_spec((1, HEAD_DIM))
    return pl.pallas_call(
        _attn_kernel,
        grid=(B, S // ATT_TQ),
        in_specs=[tile, per_b, per_b,
                  _const_spec((META_BLK, 512)), _const_spec((META_BLK, 512)),
                  small, small, small, small, _const_spec((1, V_HEAD_DIM))],
        out_specs=tile,
        out_shape=jax.ShapeDtypeStruct((B, S, 512), BF16),
        scratch_shapes=[pltpu.VMEM((META_BLK + S, N_MAPS * 128), BF16),
                        pltpu.VMEM((ATT_TQ, N_MAPS * 128), BF16)],
        name="attn",
        compiler_params=pltpu.CompilerParams(
            dimension_semantics=("arbitrary", "arbitrary"),
            vmem_limit_bytes=VMEM_LIMIT_BYTES),
    )(q, kb, vb, mkb, mvb, lq1, lk1, lq2, lk2, subln)


DEC_PAGES = 8


def _decode_kernel(pt_ref, q_ref, kn_ref, vn_ref,
                   lq1_ref, lk1_ref, lq2_ref, lk2_ref, sub_ref, *rest):
    k_refs = rest[:DEC_PAGES]
    v_refs = rest[DEC_PAGES:2 * DEC_PAGES]
    o_ref = rest[2 * DEC_PAGES]
    qbd_ref, m_ref, l_ref, acc_ref = rest[2 * DEC_PAGES + 1:]
    g = pl.program_id(1)
    ng = pl.num_programs(1)
    past = ng * DEC_PAGES * PAGE_SIZE

    @pl.when(g == 0)
    def _():
        sub = lax.broadcasted_iota(jnp.int32, (N_MAPS, ATTN_WIDTH), 0)
        lane = lax.broadcasted_iota(jnp.int32, (N_MAPS, ATTN_WIDTH), 1)
        qbd = jnp.where((lane >> 6) == sub, jnp.broadcast_to(q_ref[0], (N_MAPS, ATTN_WIDTH)), 0.0)
        qbd_ref[...] = qbd.astype(BF16)
        m_ref[...] = jnp.sum(qbd * kn_ref[0], axis=-1, keepdims=True)
        l_ref[...] = jnp.ones((N_MAPS, 1), F32)
        acc_ref[...] = jnp.broadcast_to(vn_ref[0], (N_MAPS, ATTN_WIDTH))

    sub = lax.broadcasted_iota(jnp.int32, (N_MAPS, PAGE_SIZE), 0)
    lane = lax.broadcasted_iota(jnp.int32, (N_MAPS, PAGE_SIZE), 1)
    head = sub >> 1
    slope = jnp.where(head == 0, SLOPES[0],
            jnp.where(head == 1, SLOPES[1],
            jnp.where(head == 2, SLOPES[2], SLOPES[3])))
    qbd = qbd_ref[...]
    m, l, acc = m_ref[...], l_ref[...], acc_ref[...]
    for p in range(DEC_PAGES):
        kp = k_refs[p][0].astype(BF16)
        vp = v_refs[p][0].astype(BF16)
        s = lax.dot_general(qbd, kp, (((1,), (1,)), ((), ())), preferred_element_type=F32)
        dist = past - ((g * DEC_PAGES + p) * PAGE_SIZE + lane)
        s = s - slope * dist.astype(F32)
        m_new = jnp.maximum(m, jnp.max(s, axis=-1, keepdims=True))
        alpha = jnp.exp(m - m_new)
        pr = jnp.exp(s - m_new)
        l = alpha * l + jnp.sum(pr, axis=-1, keepdims=True)
        acc = alpha * acc + jnp.dot(pr.astype(BF16), vp, preferred_element_type=F32)
        m = m_new
    m_ref[...], l_ref[...], acc_ref[...] = m, l, acc

    @pl.when(g == ng - 1)
    def _():
        o = acc / l
        lam = _lambda(lq1_ref[...], lk1_ref[...], lq2_ref[...], lk2_ref[...])
        for h in range(N_HEADS):
            cols = slice(h * 128, (h + 1) * 128)
            ho = _head_out(o[2 * h:2 * h + 1, cols], o[2 * h + 1:2 * h + 2, cols], lam, sub_ref[...])
            o_ref[0, :, cols] = ho.astype(BF16)


def _decode_call(page_table, q, kn, vn, lq1, lk1, lq2, lk2, subln, ck, cv):
    Bd, n_pages = page_table.shape
    ng = n_pages // DEC_PAGES
    row = pl.BlockSpec((1, 1, ATTN_WIDTH), lambda b, g, pt: (b, 0, 0))
    small = pl.BlockSpec((1, HEAD_DIM), lambda b, g, pt: (0, 0))

    def page_spec(p):
        return pl.BlockSpec((1, PAGE_SIZE, ATTN_WIDTH),
                            lambda b, g, pt: (pt[b, g * DEC_PAGES + p], 0, 0))

    grid_spec = pltpu.PrefetchScalarGridSpec(
        num_scalar_prefetch=1,
        grid=(Bd, ng),
        in_specs=[row, row, row, small, small, small, small,
                  pl.BlockSpec((1, V_HEAD_DIM), lambda b, g, pt: (0, 0))]
                 + [page_spec(p) for p in range(DEC_PAGES)] * 2,
        out_specs=row,
        scratch_shapes=[pltpu.VMEM((N_MAPS, ATTN_WIDTH), BF16),
                        pltpu.VMEM((N_MAPS, 1), F32),
                        pltpu.VMEM((N_MAPS, 1), F32),
                        pltpu.VMEM((N_MAPS, ATTN_WIDTH), F32)],
    )
    return pl.pallas_call(
        _decode_kernel,
        grid_spec=grid_spec,
        out_shape=jax.ShapeDtypeStruct((Bd, 1, ATTN_WIDTH), BF16),
        name="decode",
        compiler_params=pltpu.CompilerParams(
            dimension_semantics=("arbitrary", "arbitrary"),
            vmem_limit_bytes=VMEM_LIMIT_BYTES),
    )(page_table, q, kn, vn, lq1, lk1, lq2, lk2, subln,
      *([ck] * DEC_PAGES), *([cv] * DEC_PAGES))


FFN_CHUNK = 256


def _finish_kernel(a_ref, c_ref, x_ref, wo_ref, gpost_ref, gpre_ref, gfpost_ref,
                   wg_ref, wu_ref, wd_ref, o_ref):
    mix = (jnp.dot(a_ref[...], wo_ref[0:ATTN_WIDTH, :], preferred_element_type=F32)
           + jnp.dot(c_ref[...], wo_ref[ATTN_WIDTH:, :], preferred_element_type=F32))
    x = x_ref[...] + _rms(mix, gpost_ref[...])
    hb = _rms(x, gpre_ref[...]).astype(BF16)
    f = jnp.zeros(x.shape, F32)
    for c in range(FFN_HIDDEN // FFN_CHUNK):
        cols = slice(c * FFN_CHUNK, (c + 1) * FFN_CHUNK)
        gate = jnp.dot(hb, wg_ref[:, cols], preferred_element_type=F32)
        up = jnp.dot(hb, wu_ref[:, cols], preferred_element_type=F32)
        act = (gate * jax.nn.sigmoid(gate) * up).astype(BF16)
        f = f + jnp.dot(act, wd_ref[cols, :], preferred_element_type=F32)
    o_ref[...] = x + _rms(f, gfpost_ref[...])


def _finish_call(a, c, x, wo, gpost, gpre, gfpost, wg, wu, wd, tm):
    rows = x.shape[0]
    half = pl.BlockSpec((tm, 512), lambda i: (i, 0))
    full = pl.BlockSpec((tm, D_MODEL), lambda i: (i, 0))
    vec = _const_spec((1, D_MODEL))
    return pl.pallas_call(
        _finish_kernel,
        grid=(rows // tm,),
        in_specs=[half, half, full, _const_spec(wo.shape), vec, vec, vec,
                  _const_spec(wg.shape), _const_spec(wu.shape), _const_spec(wd.shape)],
        out_specs=full,
        out_shape=jax.ShapeDtypeStruct((rows, D_MODEL), F32),
        name="finish",
        compiler_params=pltpu.CompilerParams(
            dimension_semantics=("arbitrary",),
            vmem_limit_bytes=VMEM_LIMIT_BYTES),
    )(a, c, x, wo, gpost, gpre, gfpost, wg, wu, wd)


def kernel(x_prompt, x_sample, cache_k, cache_v, state_conv, page_table, meta_tokens,
           ln_mix_pre, ln_mix_post, w_in, lambda_q1, lambda_k1, lambda_q2, lambda_k2,
           subln_w, conv_w, conv_b, conv_ln_w, conv_ln_b, w_out, ln_ffn_pre, ln_ffn_post,
           w_gate, w_up, w_down):
    B, S, _ = x_prompt.shape
    Bd = x_sample.shape[0]
    T = S + N_META
    li = 0
    w_in_bf = w_in[li].astype(BF16)
    wo_bf = w_out[li].astype(BF16)
    wg_bf = w_gate[li].astype(BF16)
    wu_bf = w_up[li].astype(BF16)
    wd_bf = w_down[li].astype(BF16)
    g_pre = ln_mix_pre[li][None]
    lam_args = (lambda_q1[li][None], lambda_k1[li][None], lambda_q2[li][None],
                lambda_k2[li][None], subln_w[li][None])
    conv_args = (conv_w[li], conv_b[li][None], conv_ln_w[li][None], conv_ln_b[li][None])
    fin_args = (wo_bf, ln_mix_post[li][None], ln_ffn_pre[li][None], ln_ffn_post[li][None],
                wg_bf, wu_bf, wd_bf)

    _, mk, mv, mu = _proj_call(meta_tokens, g_pre, w_in_bf)
    q_bf, k_bf, v_bf, new_k, new_v, u_ext = _prompt_proj_call(x_prompt, g_pre, w_in_bf, mk, mv, mu)
    pad = ((0, META_BLK - N_META), (0, 0))
    attn_o = _attn_call(q_bf, k_bf, v_bf, jnp.pad(mk.astype(BF16), pad), jnp.pad(mv.astype(BF16), pad),
                        *lam_args)
    conv_o = _conv_call(u_ext, *conv_args)
    y_prompt = _finish_call(attn_o.reshape(B * S, 512), conv_o.reshape(B * S, 512),
                            x_prompt.reshape(B * S, D_MODEL), *fin_args, tm=512)

    n_pool = cache_k.shape[1]
    qs, ks, vs, us = _proj_call(x_sample.reshape(Bd, D_MODEL), g_pre, w_in_bf)
    attn_s = _decode_call(page_table, qs[:, None], ks[:, None], vs[:, None], *lam_args,
                          cache_k[li].reshape(n_pool, PAGE_SIZE, ATTN_WIDTH),
                          cache_v[li].reshape(n_pool, PAGE_SIZE, ATTN_WIDTH))
    conv_s = _sample_conv_call(state_conv[li], us, *conv_args)
    y_sample = _finish_call(attn_s.reshape(Bd, 512), conv_s, x_sample.reshape(Bd, D_MODEL),
                            *fin_args, tm=Bd)

    return (y_prompt.reshape(B, S, D_MODEL),
            y_sample.reshape(Bd, 1, D_MODEL),
            new_k.reshape(1, B, T, N_MAPS, HEAD_DIM),
            new_v.reshape(1, B, T, N_HEADS, V_HEAD_DIM),
            u_ext[:, S + U_PAD - (CONV_WIDTH - 1):][None],
            ks.reshape(1, Bd, 1, N_MAPS, HEAD_DIM),
            vs.reshape(1, Bd, 1, N_HEADS, V_HEAD_DIM),
            jnp.concatenate([state_conv[li][:, 1:], us[:, None]], axis=1)[None])
```

```python
import functools
import math

import jax
import jax.numpy as jnp
from jax import lax
from jax.experimental import pallas as pl
from jax.experimental.pallas import tpu as pltpu

D_MODEL = 1024
N_META = 16
ATTN_WIDTH = 512
CONV_DIM = 512
N_HEADS = 4
HEAD_DIM = 64
V_HEAD_DIM = 128
N_MAPS = 2 * N_HEADS
CONV_WIDTH = 31
FFN_HIDDEN = 2816
PAGE_SIZE = 128
EPS = 1e-6
NEG_INF = -1e30
LAMBDA_INIT = 0.8 - 0.6 * math.exp(-0.3 * 0)
SCALE = HEAD_DIM ** -0.5
SLOPES = tuple(2.0 ** (-8.0 * (h + 1) / N_HEADS) for h in range(N_HEADS))

VMEM_LIMIT_BYTES = 56 * 1024 * 1024

F32 = jnp.float32
BF16 = jnp.bfloat16


def _rms(x, g):
    return x * lax.rsqrt(jnp.mean(x * x, axis=-1, keepdims=True) + EPS) * g


def _const_spec(shape):
    nd = len(shape)
    return pl.BlockSpec(shape, lambda *_: (0,) * nd, pipeline_mode=pl.Buffered(1))


def _project(x, g, w_ref):
    xn = _rms(x, g).astype(BF16)
    seg = lambda c: jnp.dot(xn, w_ref[:, c * 512:(c + 1) * 512], preferred_element_type=F32)
    q = seg(0) * SCALE
    k = seg(1)
    v = seg(2)
    u = seg(3) * jax.nn.sigmoid(seg(4))
    return q, k, v, u


def _proj_kernel(x_ref, g_ref, w_ref, q_ref, k_ref, v_ref, u_ref):
    q, k, v, u = _project(x_ref[...], g_ref[...], w_ref)
    q_ref[...] = q
    k_ref[...] = k
    v_ref[...] = v
    u_ref[...] = u


def _proj_call(x, g, w_bf):
    rows = x.shape[0]
    out = jax.ShapeDtypeStruct((rows, 512), F32)
    return pl.pallas_call(
        _proj_kernel,
        out_shape=(out, out, out, out),
        name="proj_small",
        compiler_params=pltpu.CompilerParams(vmem_limit_bytes=VMEM_LIMIT_BYTES),
    )(x, g, w_bf)


PROJ_TM = 512
U_PAD = 48
U_ROWS_OFF = U_PAD - (CONV_WIDTH - 1)


def _prompt_proj_kernel(x_ref, g_ref, w_ref, mk_ref, mv_ref, mu_ref,
                        q_ref, kb_ref, vb_ref, nk_ref, nv_ref, u_ref):
    i = pl.program_id(1)

    @pl.when(i == 0)
    def _():
        nk_ref[0, 0:N_META, :] = mk_ref[...]
        nv_ref[0, 0:N_META, :] = mv_ref[...]
        u_ref[0, 0:U_PAD - N_META, :] = jnp.zeros((U_PAD - N_META, CONV_DIM), F32)
        u_ref[0, U_PAD - N_META:U_PAD, :] = mu_ref[...]

    q, k, v, u = _project(x_ref[0], g_ref[...], w_ref)
    q_ref[0] = q.astype(BF16)
    kb_ref[0] = k.astype(BF16)
    vb_ref[0] = v.astype(BF16)
    r = i * PROJ_TM
    nk_ref[0, pl.ds(pl.multiple_of(r + N_META, 8), PROJ_TM), :] = k
    nv_ref[0, pl.ds(pl.multiple_of(r + N_META, 8), PROJ_TM), :] = v
    u_ref[0, pl.ds(pl.multiple_of(r + U_PAD, 8), PROJ_TM), :] = u


def _prompt_proj_call(x, g, w_bf, mk, mv, mu):
    B, S, _ = x.shape
    T = S + N_META
    nt = S // PROJ_TM
    tile = pl.BlockSpec((1, PROJ_TM, 512), lambda b, i: (b, i, 0))
    full = lambda rows: pl.BlockSpec((1, rows, 512), lambda b, i: (b, 0, 0))
    bf = jax.ShapeDtypeStruct((B, S, 512), BF16)
    return pl.pallas_call(
        _prompt_proj_kernel,
        grid=(B, nt),
        in_specs=[
            pl.BlockSpec((1, PROJ_TM, D_MODEL), lambda b, i: (b, i, 0)),
            _const_spec((1, D_MODEL)),
            _const_spec(w_bf.shape),
            _const_spec((N_META, 512)),
            _const_spec((N_META, 512)),
            _const_spec((N_META, 512)),
        ],
        out_specs=(tile, tile, tile, full(T), full(T), full(S + U_PAD)),
        out_shape=(bf, bf, bf,
                   jax.ShapeDtypeStruct((B, T, 512), F32),
                   jax.ShapeDtypeStruct((B, T, 512), F32),
                   jax.ShapeDtypeStruct((B, S + U_PAD, 512), F32)),
        name="prompt_proj",
        compiler_params=pltpu.CompilerParams(
            dimension_semantics=("arbitrary", "arbitrary"),
            vmem_limit_bytes=VMEM_LIMIT_BYTES),
    )(x, g, w_bf, mk, mv, mu)


CONV_TM = 512
CONV_SUB = 32


def _ln_swish(y, w, b):
    mu = jnp.mean(y, axis=-1, keepdims=True)
    yc = y - mu
    z = yc * lax.rsqrt(jnp.mean(yc * yc, axis=-1, keepdims=True) + EPS) * w + b
    return z * jax.nn.sigmoid(z)


def _conv_kernel(u_ref, cw_ref, cb_ref, lw_ref, lb_ref, o_ref, y_ref):
    i = pl.program_id(1)
    lead = U_ROWS_OFF % 8
    base = i * CONV_TM + U_ROWS_OFF - lead

    def body(s, carry):
        r = s * CONV_SUB
        win = u_ref[0, pl.ds(pl.multiple_of(base + r, 8), CONV_SUB + 32), :]
        acc = jnp.broadcast_to(cb_ref[...], (CONV_SUB, CONV_DIM))
        for res in range(8):
            rows = CONV_SUB + (8 if res else 0)
            part = None
            for w in range(CONV_WIDTH):
                if (lead + w) % 8 == res:
                    off = lead + w - res
                    term = win[off:off + rows, :].reshape(rows // 8, 8, CONV_DIM) * cw_ref[w]
                    part = term if part is None else part + term
            acc = acc + part.reshape(rows, CONV_DIM)[res:res + CONV_SUB, :]
        y_ref[pl.ds(pl.multiple_of(r, CONV_SUB), CONV_SUB), :] = acc
        return carry

    lax.fori_loop(0, CONV_TM // CONV_SUB, body, 0)
    o_ref[0] = _ln_swish(y_ref[...], lw_ref[...], lb_ref[...]).astype(BF16)


def _conv_call(u_ext, cw, cb, lw, lb):
    B, R, _ = u_ext.shape
    S = R - U_PAD
    return pl.pallas_call(
        _conv_kernel,
        grid=(B, S // CONV_TM),
        in_specs=[
            pl.BlockSpec((1, R, CONV_DIM), lambda b, i: (b, 0, 0)),
            _const_spec((CONV_WIDTH, 8, CONV_DIM)),
            _const_spec((1, CONV_DIM)),
            _const_spec((1, CONV_DIM)),
            _const_spec((1, CONV_DIM)),
        ],
        out_specs=pl.BlockSpec((1, CONV_TM, CONV_DIM), lambda b, i: (b, i, 0)),
        out_shape=jax.ShapeDtypeStruct((B, S, CONV_DIM), BF16),
        scratch_shapes=[pltpu.VMEM((CONV_TM, CONV_DIM), F32)],
        name="conv",
        compiler_params=pltpu.CompilerParams(
            dimension_semantics=("arbitrary", "arbitrary"),
            vmem_limit_bytes=VMEM_LIMIT_BYTES),
    )(u_ext, cw, cb, lw, lb)


def _sample_conv_kernel(st_ref, u_ref, cw_ref, cb_ref, lw_ref, lb_ref, o_ref, y_ref):
    past_w = cw_ref[0:CONV_WIDTH - 1, :]
    for b in range(st_ref.shape[0]):
        y_ref[b:b + 1, :] = jnp.sum(st_ref[b] * past_w, axis=0, keepdims=True)
    acc = y_ref[...] + cb_ref[...] + u_ref[...] * cw_ref[CONV_WIDTH - 1:CONV_WIDTH, :]
    o_ref[...] = _ln_swish(acc, lw_ref[...], lb_ref[...]).astype(BF16)


def _sample_conv_call(state, u, cw, cb, lw, lb):
    return pl.pallas_call(
        _sample_conv_kernel,
        out_shape=jax.ShapeDtypeStruct(u.shape, BF16),
        scratch_shapes=[pltpu.VMEM(u.shape, F32)],
        name="sample_conv",
    )(state, u, cw, cb, lw, lb)


def _lambda(lq1, lk1, lq2, lk2):
    s1 = jnp.sum(lq1 * lk1, axis=-1, keepdims=True)
    s2 = jnp.sum(lq2 * lk2, axis=-1, keepdims=True)
    return jnp.exp(s1) - jnp.exp(s2) + LAMBDA_INIT


def _head_out(o0, o1, lam, subln):
    o = o0 - lam * o1
    return _rms(o, subln) * (1.0 - LAMBDA_INIT)


ATT_TQ = 256
ATT_TK = 256
META_BLK = 128
QK_AHEAD = 4


def _augment(x, pos, is_query, slope):
    lane = lax.broadcasted_iota(jnp.int32, x.shape, 1)
    hi = (pos >> 6).astype(F32)
    lo = (pos & 63).astype(F32)
    out = []
    for j in range(2):
        a = lane - (64 if j == 0 else 0)
        if is_query:
            aug = jnp.where(a == 0, hi * (-64.0 * slope),
                  jnp.where(a == 1, lo * (-slope),
                  jnp.where(a == 2, 64.0 * slope,
                  jnp.where(a == 3, slope, 0.0))))
        else:
            aug = jnp.where(a == 0, 1.0,
                  jnp.where(a == 1, 1.0,
                  jnp.where(a == 2, hi,
                  jnp.where(a == 3, lo, 0.0))))
        own = (lane < 64) if j == 0 else (lane >= 64)
        out.append(jnp.where(own, x, aug).astype(BF16))
    return out


def _attn_kernel(q_ref, kb_ref, vb_ref, mkb_ref, mvb_ref,
                 lq1_ref, lk1_ref, lq2_ref, lk2_ref, subcol_ref,
                 o_ref, kexp_ref, vt_ref, qexp_ref, m_ref, l_ref, acc_ref):
    qi = pl.program_id(1)
    S = kb_ref.shape[1]

    def chunk_step(k_rows, v_cols, mask, first):
        def scores(mm):
            mcols = slice(mm * 128, (mm + 1) * 128)
            return lax.dot_general(kexp_ref[k_rows, mcols], qexp_ref[:, mcols],
                                   (((1,), (1,)), ((), ())), preferred_element_type=F32)

        pending = [scores(mm) for mm in range(QK_AHEAD)]
        for mm in range(N_MAPS):
            h = mm // 2
            vt = vt_ref[h * 128:(h + 1) * 128, v_cols]
            if mm + QK_AHEAD < N_MAPS:
                pending.append(scores(mm + QK_AHEAD))
            s = pending[mm]
            if mask is not None:
                s = jnp.where(mask, s, NEG_INF)
            m_cur = jnp.max(s, axis=0, keepdims=True)
            m_new = m_cur if first else jnp.maximum(m_ref[mm], m_cur)
            p = jnp.exp(s - m_new)
            l_cur = jnp.sum(p, axis=0, keepdims=True)
            pb = p.astype(BF16)
            if pb.shape[0] < vt.shape[1]:
                pb = jnp.concatenate(
                    [pb, jnp.zeros((vt.shape[1] - pb.shape[0], pb.shape[1]), BF16)], axis=0)
            pv = jnp.dot(vt, pb, preferred_element_type=F32)
            if first:
                l_ref[mm] = l_cur
                acc_ref[mm] = pv
            else:
                alpha = jnp.exp(m_ref[mm] - m_new)
                l_ref[mm] = alpha * l_ref[mm] + l_cur
                acc_ref[mm] = alpha * acc_ref[mm] + pv
            m_ref[mm] = m_new

    @pl.when(qi == 0)
    def _():
        def put_vt(cols_off, vc):
            for h in range(N_HEADS):
                vt_ref[h * 128:(h + 1) * 128, cols_off:cols_off + vc.shape[0]] = (
                    vc[:, h * 128:(h + 1) * 128].astype(F32).T.astype(BF16))

        put_vt(0, mvb_ref[...])
        for c in range(S // ATT_TK):
            put_vt(META_BLK + c * ATT_TK, vb_ref[0, c * ATT_TK:(c + 1) * ATT_TK, :])

        def put(rows_off, kc, pos0):
            rows = kc.shape[0]
            pos = pos0 + lax.broadcasted_iota(jnp.int32, (rows, 128), 0)
            for h in range(N_HEADS):
                e0, e1 = _augment(kc[:, h * 128:(h + 1) * 128].astype(F32), pos, False, 0.0)
                kexp_ref[rows_off:rows_off + rows, (2 * h) * 128:(2 * h + 1) * 128] = e0
                kexp_ref[rows_off:rows_off + rows, (2 * h + 1) * 128:(2 * h + 2) * 128] = e1

        put(0, mkb_ref[...], 0)
        for c in range(S // ATT_TK):
            put(META_BLK + c * ATT_TK, kb_ref[0, c * ATT_TK:(c + 1) * ATT_TK, :],
                N_META + c * ATT_TK)

    qpos = N_META + qi * ATT_TQ + lax.broadcasted_iota(jnp.int32, (ATT_TQ, 128), 0)
    q = q_ref[0].astype(F32)
    for h in range(N_HEADS):
        e0, e1 = _augment(q[:, h * 128:(h + 1) * 128], qpos, True, SLOPES[h])
        qexp_ref[:, (2 * h) * 128:(2 * h + 1) * 128] = e0
        qexp_ref[:, (2 * h + 1) * 128:(2 * h + 2) * 128] = e1

    lam = _lambda(lq1_ref[...], lk1_ref[...], lq2_ref[...], lk2_ref[...])
    key = lax.broadcasted_iota(jnp.int32, (ATT_TK, ATT_TQ), 0)
    qry = lax.broadcasted_iota(jnp.int32, (ATT_TK, ATT_TQ), 1)
    causal = key <= qry

    chunk_step(slice(0, N_META), slice(0, META_BLK), None, True)

    def real_chunk(c, mask):
        r = pl.multiple_of(META_BLK + c * ATT_TK, 128)
        chunk_step(pl.ds(r, ATT_TK), pl.ds(r, ATT_TK), mask, False)

    def body(c, carry):
        real_chunk(c, None)
        return carry

    lax.fori_loop(0, qi, body, 0)
    real_chunk(qi, causal)

    for h in range(N_HEADS):
        o0 = acc_ref[2 * h] / l_ref[2 * h]
        o1 = acc_ref[2 * h + 1] / l_ref[2 * h + 1]
        o = o0 - lam * o1
        o = o * lax.rsqrt(jnp.mean(o * o, axis=0, keepdims=True) + EPS)
        o = o * subcol_ref[...] * (1.0 - LAMBDA_INIT)
        o_ref[0, :, h * 128:(h + 1) * 128] = o.T.astype(BF16)


def _attn_call(q, kb, vb, mkb, mvb, lq1, lk1, lq2, lk2, subln):
    B, S, _ = q.shape
    per_b = pl.BlockSpec((1, S, 512), lambda b, i: (b, 0, 0))
    tile = pl.BlockSpec((1, ATT_TQ, 512), lambda b, i: (b, i, 0))
    small = _const_spec((1, HEAD_DIM))
    return pl.pallas_call(
        _attn_kernel,
        grid=(B, S // ATT_TQ),
        in_specs=[tile, per_b, per_b,
                  _const_spec((META_BLK, 512)), _const_spec((META_BLK, 512)),
                  small, small, small, small, _const_spec((V_HEAD_DIM, 1))],
        out_specs=tile,
        out_shape=jax.ShapeDtypeStruct((B, S, 512), BF16),
        scratch_shapes=[pltpu.VMEM((META_BLK + S, N_MAPS * 128), BF16),
                        pltpu.VMEM((ATTN_WIDTH, META_BLK + S), BF16),
                        pltpu.VMEM((ATT_TQ, N_MAPS * 128), BF16),
                        pltpu.VMEM((N_MAPS, 1, ATT_TQ), F32),
                        pltpu.VMEM((N_MAPS, 1, ATT_TQ), F32),
                        pltpu.VMEM((N_MAPS, V_HEAD_DIM, ATT_TQ), F32)],
        name="attn",
        compiler_params=pltpu.CompilerParams(
            dimension_semantics=("arbitrary", "arbitrary"),
            vmem_limit_bytes=VMEM_LIMIT_BYTES),
    )(q, kb, vb, mkb, mvb, lq1, lk1, lq2, lk2, subln)


DEC_PAGES = 8
KEYS_PER_VBLK = 128 // N_HEADS


def _decode_kernel(pt_ref, q_ref, qcol_ref, kn_ref, vn_ref,
                   lq1_ref, lk1_ref, lq2_ref, lk2_ref, sub_ref, *rest):
    k_refs = rest[:DEC_PAGES]
    v_refs = rest[DEC_PAGES:2 * DEC_PAGES]
    o_ref = rest[2 * DEC_PAGES]
    m_ref, l_ref, acc_ref = rest[2 * DEC_PAGES + 1:]
    g = pl.program_id(1)
    ng = pl.num_programs(1)
    past = ng * DEC_PAGES * PAGE_SIZE

    sub = lax.broadcasted_iota(jnp.int32, (N_MAPS, PAGE_SIZE), 0)
    lane = lax.broadcasted_iota(jnp.int32, (N_MAPS, PAGE_SIZE), 1)
    head = sub >> 1

    @pl.when(g == 0)
    def _():
        sub_w = lax.broadcasted_iota(jnp.int32, (N_MAPS, ATTN_WIDTH), 0)
        lane_w = lax.broadcasted_iota(jnp.int32, (N_MAPS, ATTN_WIDTH), 1)
        qbd = jnp.where((lane_w >> 6) == sub_w, jnp.broadcast_to(q_ref[0], (N_MAPS, ATTN_WIDTH)), 0.0)
        m_ref[...] = jnp.sum(qbd * kn_ref[0], axis=-1, keepdims=True)
        l_ref[...] = jnp.ones((N_MAPS, 1), F32)
        acc = jnp.zeros((N_MAPS, V_HEAD_DIM), F32)
        for h in range(N_HEADS):
            acc = jnp.where(head == h, vn_ref[0, :, h * 128:(h + 1) * 128], acc)
        acc_ref[...] = acc

    slope = jnp.where(head == 0, SLOPES[0],
            jnp.where(head == 1, SLOPES[1],
            jnp.where(head == 2, SLOPES[2], SLOPES[3])))
    own_head = (lane & (N_HEADS - 1)) == head
    key_of_row = lane >> 2
    qcol = qcol_ref[0]
    m, l, acc = m_ref[...], l_ref[...], acc_ref[...]
    scores = []
    for p in range(DEC_PAGES):
        s = jnp.sum((k_refs[p][0] * qcol).reshape(N_MAPS, HEAD_DIM, PAGE_SIZE), axis=1)
        dist = past - ((g * DEC_PAGES + p) * PAGE_SIZE + lane)
        scores.append(s - slope * dist.astype(F32))
    s = jnp.concatenate(scores, axis=1)
    m_new = jnp.maximum(m, jnp.max(s, axis=-1, keepdims=True))
    alpha = jnp.exp(m - m_new)
    pr = jnp.exp(s - m_new)
    l = alpha * l + jnp.sum(pr, axis=-1, keepdims=True)
    prb = pr.astype(BF16).astype(F32)
    pv = jnp.zeros((N_MAPS, V_HEAD_DIM), F32)
    for p in range(DEC_PAGES):
        prp = prb[:, p * PAGE_SIZE:(p + 1) * PAGE_SIZE]
        spread = [jnp.where(own_head,
                            jnp.take_along_axis(prp, key_of_row + c * KEYS_PER_VBLK, axis=1), 0.0)
                  for c in range(N_HEADS)]
        pe = jnp.concatenate(spread, axis=1).astype(BF16)
        pv = pv + jnp.dot(pe, v_refs[p][0].astype(BF16), preferred_element_type=F32)
    m_ref[...], l_ref[...], acc_ref[...] = m_new, l, alpha * acc + pv

    @pl.when(g == ng - 1)
    def _():
        o = acc_ref[...] / l_ref[...]
        lam = _lambda(lq1_ref[...], lk1_ref[...], lq2_ref[...], lk2_ref[...])
        for h in range(N_HEADS):
            ho = _head_out(o[2 * h:2 * h + 1], o[2 * h + 1:2 * h + 2], lam, sub_ref[...])
            o_ref[0, :, h * 128:(h + 1) * 128] = ho.astype(BF16)


def _decode_call(page_table, q, qcol, kn, vn, lq1, lk1, lq2, lk2, subln, ck, cv):
    Bd, n_pages = page_table.shape
    ng = n_pages // DEC_PAGES
    row = pl.BlockSpec((1, 1, ATTN_WIDTH), lambda b, g, pt: (b, 0, 0))
    small = pl.BlockSpec((1, HEAD_DIM), lambda b, g, pt: (0, 0))

    def page_spec(p):
        return pl.BlockSpec((1, ATTN_WIDTH, PAGE_SIZE),
                            lambda b, g, pt: (pt[b, g * DEC_PAGES + p], 0, 0))

    grid_spec = pltpu.PrefetchScalarGridSpec(
        num_scalar_prefetch=1,
        grid=(Bd, ng),
        in_specs=[row, pl.BlockSpec((1, ATTN_WIDTH, PAGE_SIZE), lambda b, g, pt: (b, 0, 0)),
                  row, row, small, small, small, small,
                  pl.BlockSpec((1, V_HEAD_DIM), lambda b, g, pt: (0, 0))]
                 + [page_spec(p) for p in range(DEC_PAGES)] * 2,
        out_specs=row,
        scratch_shapes=[pltpu.VMEM((N_MAPS, 1), F32),
                        pltpu.VMEM((N_MAPS, 1), F32),
                        pltpu.VMEM((N_MAPS, V_HEAD_DIM), F32)],
    )
    return pl.pallas_call(
        _decode_kernel,
        grid_spec=grid_spec,
        out_shape=jax.ShapeDtypeStruct((Bd, 1, ATTN_WIDTH), BF16),
        name="decode",
        compiler_params=pltpu.CompilerParams(
            dimension_semantics=("arbitrary", "arbitrary"),
            vmem_limit_bytes=VMEM_LIMIT_BYTES),
    )(page_table, q, qcol, kn, vn, lq1, lk1, lq2, lk2, subln,
      *([ck] * DEC_PAGES), *([cv] * DEC_PAGES))


FFN_CHUNK = 256


def _finish_kernel(a_ref, c_ref, x_ref, wo_ref, gpost_ref, gpre_ref, gfpost_ref,
                   wg_ref, wu_ref, wd_ref, o_ref):
    mix = (jnp.dot(a_ref[...], wo_ref[0:ATTN_WIDTH, :], preferred_element_type=F32)
           + jnp.dot(c_ref[...], wo_ref[ATTN_WIDTH:, :], preferred_element_type=F32))
    x = x_ref[...] + _rms(mix, gpost_ref[...])
    hb = _rms(x, gpre_ref[...]).astype(BF16)
    f = jnp.zeros(x.shape, F32)
    for c in range(FFN_HIDDEN // FFN_CHUNK):
        cols = slice(c * FFN_CHUNK, (c + 1) * FFN_CHUNK)
        gate = jnp.dot(hb, wg_ref[:, cols], preferred_element_type=F32)
        up = jnp.dot(hb, wu_ref[:, cols], preferred_element_type=F32)
        act = (gate * jax.nn.sigmoid(gate) * up).astype(BF16)
        f = f + jnp.dot(act, wd_ref[cols, :], preferred_element_type=F32)
    o_ref[...] = x + _rms(f, gfpost_ref[...])


def _finish_call(a, c, x, wo, gpost, gpre, gfpost, wg, wu, wd, tm):
    rows = x.shape[0]
    half = pl.BlockSpec((tm, 512), lambda i: (i, 0))
    full = pl.BlockSpec((tm, D_MODEL), lambda i: (i, 0))
    vec = _const_spec((1, D_MODEL))
    return pl.pallas_call(
        _finish_kernel,
        grid=(rows // tm,),
        in_specs=[half, half, full, _const_spec(wo.shape), vec, vec, vec,
                  _const_spec(wg.shape), _const_spec(wu.shape), _const_spec(wd.shape)],
        out_specs=full,
        out_shape=jax.ShapeDtypeStruct((rows, D_MODEL), F32),
        name="finish",
        compiler_params=pltpu.CompilerParams(
            dimension_semantics=("arbitrary",),
            vmem_limit_bytes=VMEM_LIMIT_BYTES),
    )(a, c, x, wo, gpost, gpre, gfpost, wg, wu, wd)


def kernel(x_prompt, x_sample, cache_k, cache_v, state_conv, page_table, meta_tokens,
           ln_mix_pre, ln_mix_post, w_in, lambda_q1, lambda_k1, lambda_q2, lambda_k2,
           subln_w, conv_w, conv_b, conv_ln_w, conv_ln_b, w_out, ln_ffn_pre, ln_ffn_post,
           w_gate, w_up, w_down):
    B, S, _ = x_prompt.shape
    Bd = x_sample.shape[0]
    T = S + N_META
    li = 0
    w_in_bf = w_in[li].astype(BF16)
    wo_bf = w_out[li].astype(BF16)
    wg_bf = w_gate[li].astype(BF16)
    wu_bf = w_up[li].astype(BF16)
    wd_bf = w_down[li].astype(BF16)
    g_pre = ln_mix_pre[li][None]
    lam_args = (lambda_q1[li][None], lambda_k1[li][None], lambda_q2[li][None],
                lambda_k2[li][None], subln_w[li][None])
    conv_args = (conv_w[li], conv_b[li][None], conv_ln_w[li][None], conv_ln_b[li][None])
    fin_args = (wo_bf, ln_mix_post[li][None], ln_ffn_pre[li][None], ln_ffn_post[li][None],
                wg_bf, wu_bf, wd_bf)

    _, mk, mv, mu = _proj_call(meta_tokens, g_pre, w_in_bf)
    q_bf, k_bf, v_bf, new_k, new_v, u_ext = _prompt_proj_call(x_prompt, g_pre, w_in_bf, mk, mv, mu)
    pad = ((0, META_BLK - N_META), (0, 0))
    attn_o = _attn_call(q_bf, k_bf, v_bf, jnp.pad(mk.astype(BF16), pad), jnp.pad(mv.astype(BF16), pad),
                        *lam_args[:4], subln_w[li][:, None])
    cw_tiles = jnp.broadcast_to(conv_w[li][:, None, :], (CONV_WIDTH, 8, CONV_DIM))
    conv_o = _conv_call(u_ext, cw_tiles, *conv_args[1:])
    y_prompt = _finish_call(attn_o.reshape(B * S, 512), conv_o.reshape(B * S, 512),
                            x_prompt.reshape(B * S, D_MODEL), *fin_args, tm=512)

    n_pool = cache_k.shape[1]
    qs, ks, vs, us = _proj_call(x_sample.reshape(Bd, D_MODEL), g_pre, w_in_bf)
    ck = jnp.transpose(cache_k[li], (0, 2, 3, 1)).reshape(n_pool, ATTN_WIDTH, PAGE_SIZE)
    cv = cache_v[li].reshape(n_pool, PAGE_SIZE * N_HEADS, V_HEAD_DIM)
    qcol = jnp.broadcast_to(qs[:, :, None], (Bd, ATTN_WIDTH, PAGE_SIZE))
    attn_s = _decode_call(page_table, qs[:, None], qcol, ks[:, None], vs[:, None], *lam_args, ck, cv)
    conv_s = _sample_conv_call(state_conv[li], us, *conv_args)
    y_sample = _finish_call(attn_s.reshape(Bd, 512), conv_s, x_sample.reshape(Bd, D_MODEL),
                            *fin_args, tm=Bd)

    return (y_prompt.reshape(B, S, D_MODEL),
            y_sample.reshape(Bd, 1, D_MODEL),
            new_k.reshape(1, B, T, N_MAPS, HEAD_DIM),
            new_v.reshape(1, B, T, N_HEADS, V_HEAD_DIM),
            u_ext[:, S + U_PAD - (CONV_WIDTH - 1):][None],
            ks.reshape(1, Bd, 1, N_MAPS, HEAD_DIM),
            vs.reshape(1, Bd, 1, N_HEADS, V_HEAD_DIM),
            jnp.concatenate([state_conv[li][:, 1:], us[:, None]], axis=1)[None])
```

```python
import functools
import math

import jax
import jax.numpy as jnp
from jax import lax
from jax.experimental import pallas as pl
from jax.experimental.pallas import tpu as pltpu

D_MODEL = 1024
N_META = 16
ATTN_WIDTH = 512
CONV_DIM = 512
N_HEADS = 4
HEAD_DIM = 64
V_HEAD_DIM = 128
N_MAPS = 2 * N_HEADS
CONV_WIDTH = 31
FFN_HIDDEN = 2816
PAGE_SIZE = 128
EPS = 1e-6
NEG_INF = -1e30
LAMBDA_INIT = 0.8 - 0.6 * math.exp(-0.3 * 0)
SCALE = HEAD_DIM ** -0.5
SLOPES = tuple(2.0 ** (-8.0 * (h + 1) / N_HEADS) for h in range(N_HEADS))

VMEM_LIMIT_BYTES = 56 * 1024 * 1024

F32 = jnp.float32
BF16 = jnp.bfloat16


def _rms(x, g):
    return x * lax.rsqrt(jnp.mean(x * x, axis=-1, keepdims=True) + EPS) * g


def _const_spec(shape):
    nd = len(shape)
    return pl.BlockSpec(shape, lambda *_: (0,) * nd, pipeline_mode=pl.Buffered(1))


def _project(x, g, w_ref):
    xn = _rms(x, g).astype(BF16)
    seg = lambda c: jnp.dot(xn, w_ref[:, c * 512:(c + 1) * 512], preferred_element_type=F32)
    q = seg(0) * SCALE
    k = seg(1)
    v = seg(2)
    u = seg(3) * jax.nn.sigmoid(seg(4))
    return q, k, v, u


def _proj_kernel(x_ref, g_ref, w_ref, q_ref, k_ref, v_ref, u_ref):
    q, k, v, u = _project(x_ref[...], g_ref[...], w_ref)
    q_ref[...] = q
    k_ref[...] = k
    v_ref[...] = v
    u_ref[...] = u


def _proj_call(x, g, w_bf):
    rows = x.shape[0]
    out = jax.ShapeDtypeStruct((rows, 512), F32)
    return pl.pallas_call(
        _proj_kernel,
        out_shape=(out, out, out, out),
        name="proj_small",
        compiler_params=pltpu.CompilerParams(vmem_limit_bytes=VMEM_LIMIT_BYTES),
    )(x, g, w_bf)


PROJ_TM = 512
U_PAD = 48
U_ROWS_OFF = U_PAD - (CONV_WIDTH - 1)


def _prompt_proj_kernel(x_ref, g_ref, w_ref, mk_ref, mv_ref, mu_ref,
                        q_ref, kb_ref, vb_ref, nk_ref, nv_ref, u_ref):
    i = pl.program_id(1)

    @pl.when(i == 0)
    def _():
        nk_ref[0, 0:N_META, :] = mk_ref[...]
        nv_ref[0, 0:N_META, :] = mv_ref[...]
        u_ref[0, 0:U_PAD - N_META, :] = jnp.zeros((U_PAD - N_META, CONV_DIM), F32)
        u_ref[0, U_PAD - N_META:U_PAD, :] = mu_ref[...]

    q, k, v, u = _project(x_ref[0], g_ref[...], w_ref)
    q_ref[0] = q.astype(BF16)
    kb_ref[0] = k.astype(BF16)
    vb_ref[0] = v.astype(BF16)
    r = i * PROJ_TM
    nk_ref[0, pl.ds(pl.multiple_of(r + N_META, 8), PROJ_TM), :] = k
    nv_ref[0, pl.ds(pl.multiple_of(r + N_META, 8), PROJ_TM), :] = v
    u_ref[0, pl.ds(pl.multiple_of(r + U_PAD, 8), PROJ_TM), :] = u


def _prompt_proj_call(x, g, w_bf, mk, mv, mu):
    B, S, _ = x.shape
    T = S + N_META
    nt = S // PROJ_TM
    tile = pl.BlockSpec((1, PROJ_TM, 512), lambda b, i: (b, i, 0))
    full = lambda rows: pl.BlockSpec((1, rows, 512), lambda b, i: (b, 0, 0))
    bf = jax.ShapeDtypeStruct((B, S, 512), BF16)
    return pl.pallas_call(
        _prompt_proj_kernel,
        grid=(B, nt),
        in_specs=[
            pl.BlockSpec((1, PROJ_TM, D_MODEL), lambda b, i: (b, i, 0)),
            _const_spec((1, D_MODEL)),
            _const_spec(w_bf.shape),
            _const_spec((N_META, 512)),
            _const_spec((N_META, 512)),
            _const_spec((N_META, 512)),
        ],
        out_specs=(tile, tile, tile, full(T), full(T), full(S + U_PAD)),
        out_shape=(bf, bf, bf,
                   jax.ShapeDtypeStruct((B, T, 512), F32),
                   jax.ShapeDtypeStruct((B, T, 512), F32),
                   jax.ShapeDtypeStruct((B, S + U_PAD, 512), F32)),
        name="prompt_proj",
        compiler_params=pltpu.CompilerParams(
            dimension_semantics=("arbitrary", "arbitrary"),
            vmem_limit_bytes=VMEM_LIMIT_BYTES),
    )(x, g, w_bf, mk, mv, mu)


CONV_TM = 512
CONV_SUB = 32


def _ln_swish(y, w, b):
    mu = jnp.mean(y, axis=-1, keepdims=True)
    yc = y - mu
    z = yc * lax.rsqrt(jnp.mean(yc * yc, axis=-1, keepdims=True) + EPS) * w + b
    return z * jax.nn.sigmoid(z)


def _conv_kernel(u_ref, cw_ref, cb_ref, lw_ref, lb_ref, o_ref, y_ref):
    i = pl.program_id(1)
    lead = U_ROWS_OFF % 8
    base = i * CONV_TM + U_ROWS_OFF - lead

    def body(s, carry):
        r = s * CONV_SUB
        win = u_ref[0, pl.ds(pl.multiple_of(base + r, 8), CONV_SUB + 32), :]
        acc = jnp.broadcast_to(cb_ref[...], (CONV_SUB, CONV_DIM))
        for res in range(8):
            rows = CONV_SUB + (8 if res else 0)
            part = None
            for w in range(CONV_WIDTH):
                if (lead + w) % 8 == res:
                    off = lead + w - res
                    term = win[off:off + rows, :].reshape(rows // 8, 8, CONV_DIM) * cw_ref[w]
                    part = term if part is None else part + term
            acc = acc + part.reshape(rows, CONV_DIM)[res:res + CONV_SUB, :]
        y_ref[pl.ds(pl.multiple_of(r, CONV_SUB), CONV_SUB), :] = acc
        return carry

    lax.fori_loop(0, CONV_TM // CONV_SUB, body, 0)
    o_ref[0] = _ln_swish(y_ref[...], lw_ref[...], lb_ref[...]).astype(BF16)


def _conv_call(u_ext, cw, cb, lw, lb):
    B, R, _ = u_ext.shape
    S = R - U_PAD
    return pl.pallas_call(
        _conv_kernel,
        grid=(B, S // CONV_TM),
        in_specs=[
            pl.BlockSpec((1, R, CONV_DIM), lambda b, i: (b, 0, 0)),
            _const_spec((CONV_WIDTH, 8, CONV_DIM)),
            _const_spec((1, CONV_DIM)),
            _const_spec((1, CONV_DIM)),
            _const_spec((1, CONV_DIM)),
        ],
        out_specs=pl.BlockSpec((1, CONV_TM, CONV_DIM), lambda b, i: (b, i, 0)),
        out_shape=jax.ShapeDtypeStruct((B, S, CONV_DIM), BF16),
        scratch_shapes=[pltpu.VMEM((CONV_TM, CONV_DIM), F32)],
        name="conv",
        compiler_params=pltpu.CompilerParams(
            dimension_semantics=("arbitrary", "arbitrary"),
            vmem_limit_bytes=VMEM_LIMIT_BYTES),
    )(u_ext, cw, cb, lw, lb)


def _sample_conv_kernel(st_ref, u_ref, cw_ref, cb_ref, lw_ref, lb_ref, o_ref, y_ref):
    past_w = cw_ref[0:CONV_WIDTH - 1, :]
    for b in range(st_ref.shape[0]):
        y_ref[b:b + 1, :] = jnp.sum(st_ref[b] * past_w, axis=0, keepdims=True)
    acc = y_ref[...] + cb_ref[...] + u_ref[...] * cw_ref[CONV_WIDTH - 1:CONV_WIDTH, :]
    o_ref[...] = _ln_swish(acc, lw_ref[...], lb_ref[...]).astype(BF16)


def _sample_conv_call(state, u, cw, cb, lw, lb):
    return pl.pallas_call(
        _sample_conv_kernel,
        out_shape=jax.ShapeDtypeStruct(u.shape, BF16),
        scratch_shapes=[pltpu.VMEM(u.shape, F32)],
        name="sample_conv",
    )(state, u, cw, cb, lw, lb)


def _lambda(lq1, lk1, lq2, lk2):
    s1 = jnp.sum(lq1 * lk1, axis=-1, keepdims=True)
    s2 = jnp.sum(lq2 * lk2, axis=-1, keepdims=True)
    return jnp.exp(s1) - jnp.exp(s2) + LAMBDA_INIT


def _head_out(o0, o1, lam, subln):
    o = o0 - lam * o1
    return _rms(o, subln) * (1.0 - LAMBDA_INIT)


ATT_TQ = 256
ATT_TK = 256
META_BLK = 128
QK_AHEAD = 4


def _augment(x, pos, is_query, slope):
    lane = lax.broadcasted_iota(jnp.int32, x.shape, 1)
    hi = (pos >> 6).astype(F32)
    lo = (pos & 63).astype(F32)
    out = []
    for j in range(2):
        a = lane - (64 if j == 0 else 0)
        if is_query:
            aug = jnp.where(a == 0, hi * (-64.0 * slope),
                  jnp.where(a == 1, lo * (-slope),
                  jnp.where(a == 2, 64.0 * slope,
                  jnp.where(a == 3, slope, 0.0))))
        else:
            aug = jnp.where(a == 0, 1.0,
                  jnp.where(a == 1, 1.0,
                  jnp.where(a == 2, hi,
                  jnp.where(a == 3, lo, 0.0))))
        own = (lane < 64) if j == 0 else (lane >= 64)
        out.append(jnp.where(own, x, aug).astype(BF16))
    return out


def _attn_kernel(q_ref, kb_ref, vb_ref, mkb_ref, mvb_ref,
                 lq1_ref, lk1_ref, lq2_ref, lk2_ref, subcol_ref,
                 o_ref, kexp_ref, vt_ref, qexp_ref, m_ref, l_ref, acc_ref):
    qi = pl.program_id(1)
    S = kb_ref.shape[1]

    def chunk_step(k_rows, v_cols, mask, first):
        def scores(mm):
            mcols = slice(mm * 128, (mm + 1) * 128)
            return lax.dot_general(kexp_ref[k_rows, mcols], qexp_ref[:, mcols],
                                   (((1,), (1,)), ((), ())), preferred_element_type=F32)

        pending = [scores(mm) for mm in range(QK_AHEAD)]
        for mm in range(N_MAPS):
            h = mm // 2
            vt = vt_ref[h * 128:(h + 1) * 128, v_cols]
            if mm + QK_AHEAD < N_MAPS:
                pending.append(scores(mm + QK_AHEAD))
            s = pending[mm]
            if mask is not None:
                s = jnp.where(mask, s, NEG_INF)
            m_cur = jnp.max(s, axis=0, keepdims=True)
            m_new = m_cur if first else jnp.maximum(m_ref[mm], m_cur)
            p = jnp.exp(s - m_new)
            l_cur = jnp.sum(p, axis=0, keepdims=True)
            pb = p.astype(BF16)
            if pb.shape[0] < vt.shape[1]:
                pb = jnp.concatenate(
                    [pb, jnp.zeros((vt.shape[1] - pb.shape[0], pb.shape[1]), BF16)], axis=0)
            pv = jnp.dot(vt, pb, preferred_element_type=F32)
            if first:
                l_ref[mm] = l_cur
                acc_ref[mm] = pv
            else:
                alpha = jnp.exp(m_ref[mm] - m_new)
                l_ref[mm] = alpha * l_ref[mm] + l_cur
                acc_ref[mm] = alpha * acc_ref[mm] + pv
            m_ref[mm] = m_new

    @pl.when(qi == 0)
    def _():
        def put_vt(cols_off, vc):
            for h in range(N_HEADS):
                vt_ref[h * 128:(h + 1) * 128, cols_off:cols_off + vc.shape[0]] = (
                    vc[:, h * 128:(h + 1) * 128].astype(F32).T.astype(BF16))

        put_vt(0, mvb_ref[...])
        for c in range(S // ATT_TK):
            put_vt(META_BLK + c * ATT_TK, vb_ref[0, c * ATT_TK:(c + 1) * ATT_TK, :])

        def put(rows_off, kc, pos0):
            rows = kc.shape[0]
            pos = pos0 + lax.broadcasted_iota(jnp.int32, (rows, 128), 0)
            for h in range(N_HEADS):
                e0, e1 = _augment(kc[:, h * 128:(h + 1) * 128].astype(F32), pos, False, 0.0)
                kexp_ref[rows_off:rows_off + rows, (2 * h) * 128:(2 * h + 1) * 128] = e0
                kexp_ref[rows_off:rows_off + rows, (2 * h + 1) * 128:(2 * h + 2) * 128] = e1

        put(0, mkb_ref[...], 0)
        for c in range(S // ATT_TK):
            put(META_BLK + c * ATT_TK, kb_ref[0, c * ATT_TK:(c + 1) * ATT_TK, :],
                N_META + c * ATT_TK)

    qpos = N_META + qi * ATT_TQ + lax.broadcasted_iota(jnp.int32, (ATT_TQ, 128), 0)
    q = q_ref[0].astype(F32)
    for h in range(N_HEADS):
        e0, e1 = _augment(q[:, h * 128:(h + 1) * 128], qpos, True, SLOPES[h])
        qexp_ref[:, (2 * h) * 128:(2 * h + 1) * 128] = e0
        qexp_ref[:, (2 * h + 1) * 128:(2 * h + 2) * 128] = e1

    lam = _lambda(lq1_ref[...], lk1_ref[...], lq2_ref[...], lk2_ref[...])
    key = lax.broadcasted_iota(jnp.int32, (ATT_TK, ATT_TQ), 0)
    qry = lax.broadcasted_iota(jnp.int32, (ATT_TK, ATT_TQ), 1)
    causal = key <= qry

    chunk_step(slice(0, N_META), slice(0, META_BLK), None, True)

    def real_chunk(c, mask):
        r = pl.multiple_of(META_BLK + c * ATT_TK, 128)
        chunk_step(pl.ds(r, ATT_TK), pl.ds(r, ATT_TK), mask, False)

    def body(c, carry):
        real_chunk(c, None)
        return carry

    lax.fori_loop(0, qi, body, 0)
    real_chunk(qi, causal)

    for h in range(N_HEADS):
        o0 = acc_ref[2 * h] / l_ref[2 * h]
        o1 = acc_ref[2 * h + 1] / l_ref[2 * h + 1]
        o = o0 - lam * o1
        o = o * lax.rsqrt(jnp.mean(o * o, axis=0, keepdims=True) + EPS)
        o = o * subcol_ref[...] * (1.0 - LAMBDA_INIT)
        o_ref[0, :, h * 128:(h + 1) * 128] = o.T.astype(BF16)


def _attn_call(q, kb, vb, mkb, mvb, lq1, lk1, lq2, lk2, subln):
    B, S, _ = q.shape
    per_b = pl.BlockSpec((1, S, 512), lambda b, i: (b, 0, 0))
    tile = pl.BlockSpec((1, ATT_TQ, 512), lambda b, i: (b, i, 0))
    small = _const_spec((1, HEAD_DIM))
    return pl.pallas_call(
        _attn_kernel,
        grid=(B, S // ATT_TQ),
        in_specs=[tile, per_b, per_b,
                  _const_spec((META_BLK, 512)), _const_spec((META_BLK, 512)),
                  small, small, small, small, _const_spec((V_HEAD_DIM, 1))],
        out_specs=tile,
        out_shape=jax.ShapeDtypeStruct((B, S, 512), BF16),
        scratch_shapes=[pltpu.VMEM((META_BLK + S, N_MAPS * 128), BF16),
                        pltpu.VMEM((ATTN_WIDTH, META_BLK + S), BF16),
                        pltpu.VMEM((ATT_TQ, N_MAPS * 128), BF16),
                        pltpu.VMEM((N_MAPS, 1, ATT_TQ), F32),
                        pltpu.VMEM((N_MAPS, 1, ATT_TQ), F32),
                        pltpu.VMEM((N_MAPS, V_HEAD_DIM, ATT_TQ), F32)],
        name="attn",
        compiler_params=pltpu.CompilerParams(
            dimension_semantics=("arbitrary", "arbitrary"),
            vmem_limit_bytes=VMEM_LIMIT_BYTES),
    )(q, kb, vb, mkb, mvb, lq1, lk1, lq2, lk2, subln)


KEYS_PER_VBLK = 128 // N_HEADS


def _decode_consts():
    sub = lax.broadcasted_iota(jnp.int32, (N_MAPS, PAGE_SIZE), 0)
    lane = lax.broadcasted_iota(jnp.int32, (N_MAPS, PAGE_SIZE), 1)
    head = sub >> 1
    slope = jnp.where(head == 0, SLOPES[0],
            jnp.where(head == 1, SLOPES[1],
            jnp.where(head == 2, SLOPES[2], SLOPES[3])))
    return dict(lane=lane, head=head, slope=slope,
                own_head=(lane & (N_HEADS - 1)) == head, key_of_row=lane >> 2)


def _decode_init(q_row, k_new, v_new, head):
    sub_w = lax.broadcasted_iota(jnp.int32, (N_MAPS, ATTN_WIDTH), 0)
    lane_w = lax.broadcasted_iota(jnp.int32, (N_MAPS, ATTN_WIDTH), 1)
    qbd = jnp.where((lane_w >> 6) == sub_w, jnp.broadcast_to(q_row, (N_MAPS, ATTN_WIDTH)), 0.0)
    m = jnp.sum(qbd * k_new, axis=-1, keepdims=True)
    acc = jnp.zeros((N_MAPS, V_HEAD_DIM), F32)
    for h in range(N_HEADS):
        acc = jnp.where(head == h, v_new[:, h * 128:(h + 1) * 128], acc)
    return m, jnp.ones((N_MAPS, 1), F32), acc


def _decode_probs(m, l, qcol, k_pages, first_page, past, cst):
    scores = []
    for p, kp in enumerate(k_pages):
        s = jnp.sum((kp * qcol).reshape(N_MAPS, HEAD_DIM, PAGE_SIZE), axis=1)
        dist = past - ((first_page + p) * PAGE_SIZE + cst["lane"])
        scores.append(s - cst["slope"] * dist.astype(F32))
    s = jnp.concatenate(scores, axis=1)
    m_new = jnp.maximum(m, jnp.max(s, axis=-1, keepdims=True))
    alpha = jnp.exp(m - m_new)
    pr = jnp.exp(s - m_new)
    l = alpha * l + jnp.sum(pr, axis=-1, keepdims=True)
    prb = pr.astype(BF16).astype(F32)
    spread_probs = []
    for p in range(len(k_pages)):
        prp = prb[:, p * PAGE_SIZE:(p + 1) * PAGE_SIZE]
        spread = [jnp.where(cst["own_head"],
                            jnp.take_along_axis(prp, cst["key_of_row"] + c * KEYS_PER_VBLK, axis=1),
                            0.0)
                  for c in range(N_HEADS)]
        spread_probs.append(jnp.concatenate(spread, axis=1).astype(BF16))
    return m_new, l, alpha, spread_probs


def _decode_values(acc, alpha, spread_probs, v_pages):
    pv = jnp.zeros((N_MAPS, V_HEAD_DIM), F32)
    for pe, vp in zip(spread_probs, v_pages):
        pv = pv + jnp.dot(pe, vp.astype(BF16), preferred_element_type=F32)
    return alpha * acc + pv


def _decode_out(l, acc, lam, subln, o_ref):
    o = acc / l
    for h in range(N_HEADS):
        ho = _head_out(o[2 * h:2 * h + 1], o[2 * h + 1:2 * h + 2], lam, subln)
        o_ref[0, :, h * 128:(h + 1) * 128] = ho.astype(BF16)


FFN_CHUNK = 256


N_FFN_CHUNKS = FFN_HIDDEN // FFN_CHUNK


def _finish_rows(a_ref, c_ref, x_ref, wo_ref, gpost_ref, gpre_ref, gfpost_ref,
                 wg_ref, wu_ref, wd_ref, o_ref, before_chunk=None, after_chunk=None):
    mix = (jnp.dot(a_ref[...], wo_ref[0:ATTN_WIDTH, :], preferred_element_type=F32)
           + jnp.dot(c_ref[...], wo_ref[ATTN_WIDTH:, :], preferred_element_type=F32))
    x = x_ref[...] + _rms(mix, gpost_ref[...])
    hb = _rms(x, gpre_ref[...]).astype(BF16)
    f = jnp.zeros(x.shape, F32)
    for c in range(N_FFN_CHUNKS):
        cols = slice(c * FFN_CHUNK, (c + 1) * FFN_CHUNK)
        if before_chunk is not None:
            before_chunk(c)
        gate = jnp.dot(hb, wg_ref[:, cols], preferred_element_type=F32)
        up = jnp.dot(hb, wu_ref[:, cols], preferred_element_type=F32)
        act = (gate * jax.nn.sigmoid(gate) * up).astype(BF16)
        f = f + jnp.dot(act, wd_ref[cols, :], preferred_element_type=F32)
        if after_chunk is not None:
            after_chunk(c)
    o_ref[...] = x + _rms(f, gfpost_ref[...])


def _finish_kernel(*refs):
    _finish_rows(*refs)


DEC_GROUP = 8


def _finish_decode_kernel(pt_ref, a_ref, c_ref, x_ref, wo_ref, gpost_ref, gpre_ref, gfpost_ref,
                          wg_ref, wu_ref, wd_ref,
                          q_ref, qcol_ref, kn_ref, vn_ref,
                          lq1_ref, lk1_ref, lq2_ref, lk2_ref, sub_ref, ck_hbm, cv_hbm,
                          o_ref, os_ref, kbuf, vbuf, sem):
    i = pl.program_id(0)
    n_steps = pl.num_programs(0)
    n_pages = pt_ref.shape[1]
    n_groups = n_pages // DEC_GROUP
    past = n_pages * PAGE_SIZE

    def group_copies(b, g, slot):
        copies = []
        for p in range(DEC_GROUP):
            page = pt_ref[b, g * DEC_GROUP + p]
            copies.append(pltpu.make_async_copy(ck_hbm.at[page], kbuf.at[slot, p], sem.at[0, slot]))
            copies.append(pltpu.make_async_copy(cv_hbm.at[page], vbuf.at[slot, p], sem.at[1, slot]))
        return copies

    @pl.when(i == 0)
    def _():
        for cp in group_copies(0, 0, 0):
            cp.start()

    cst = _decode_consts()
    qcol = qcol_ref[0]
    m, l, acc = _decode_init(q_ref[0], kn_ref[0], vn_ref[0], cst["head"])
    st = dict(m=m, l=l, acc=acc)

    def before_chunk(g):
        if g >= n_groups:
            return
        slot = g % 2
        if g + 1 < n_groups:
            for cp in group_copies(i, g + 1, 1 - slot):
                cp.start()
        else:
            @pl.when(i + 1 < n_steps)
            def _():
                for cp in group_copies(i + 1, 0, 1 - slot):
                    cp.start()
        for cp in group_copies(i, g, slot):
            cp.wait()
        st["m"], st["l"], st["alpha"], st["probs"] = _decode_probs(
            st["m"], st["l"], qcol, [kbuf[slot, p] for p in range(DEC_GROUP)],
            g * DEC_GROUP, past, cst)

    def after_chunk(g):
        if g >= n_groups:
            return
        st["acc"] = _decode_values(st["acc"], st["alpha"], st["probs"],
                                   [vbuf[g % 2, p] for p in range(DEC_GROUP)])

    _finish_rows(a_ref, c_ref, x_ref, wo_ref, gpost_ref, gpre_ref, gfpost_ref,
                 wg_ref, wu_ref, wd_ref, o_ref, before_chunk, after_chunk)
    lam = _lambda(lq1_ref[...], lk1_ref[...], lq2_ref[...], lk2_ref[...])
    _decode_out(st["l"], st["acc"], lam, sub_ref[...], os_ref)


def _finish_decode_call(page_table, a, c, x, wo, gpost, gpre, gfpost, wg, wu, wd,
                        q, qcol, kn, vn, lq1, lk1, lq2, lk2, subln, ck, cv, tm):
    rows = x.shape[0]
    Bd, n_pages = page_table.shape
    assert rows // tm == Bd and n_pages % (2 * DEC_GROUP) == 0
    assert n_pages // DEC_GROUP <= N_FFN_CHUNKS
    half = pl.BlockSpec((tm, 512), lambda i, pt: (i, 0))
    full = pl.BlockSpec((tm, D_MODEL), lambda i, pt: (i, 0))
    vec = _const_spec((1, D_MODEL))
    row = pl.BlockSpec((1, 1, ATTN_WIDTH), lambda i, pt: (i, 0, 0))
    small = _const_spec((1, HEAD_DIM))
    hbm = pl.BlockSpec(memory_space=pl.ANY)
    grid_spec = pltpu.PrefetchScalarGridSpec(
        num_scalar_prefetch=1,
        grid=(Bd,),
        in_specs=[half, half, full, _const_spec(wo.shape), vec, vec, vec,
                  _const_spec(wg.shape), _const_spec(wu.shape), _const_spec(wd.shape),
                  row, pl.BlockSpec((1, ATTN_WIDTH, PAGE_SIZE), lambda i, pt: (i, 0, 0)),
                  row, row, small, small, small, small, _const_spec((1, V_HEAD_DIM)),
                  hbm, hbm],
        out_specs=(full, row),
        scratch_shapes=[pltpu.VMEM((2, DEC_GROUP, ATTN_WIDTH, PAGE_SIZE), F32),
                        pltpu.VMEM((2, DEC_GROUP, ATTN_WIDTH, PAGE_SIZE), F32),
                        pltpu.SemaphoreType.DMA((2, 2))],
    )
    return pl.pallas_call(
        _finish_decode_kernel,
        grid_spec=grid_spec,
        out_shape=(jax.ShapeDtypeStruct((rows, D_MODEL), F32),
                   jax.ShapeDtypeStruct((Bd, 1, ATTN_WIDTH), BF16)),
        name="finish_decode",
        compiler_params=pltpu.CompilerParams(
            dimension_semantics=("arbitrary",),
            vmem_limit_bytes=VMEM_LIMIT_BYTES),
    )(page_table, a, c, x, wo, gpost, gpre, gfpost, wg, wu, wd,
      q, qcol, kn, vn, lq1, lk1, lq2, lk2, subln, ck, cv)


def _finish_call(a, c, x, wo, gpost, gpre, gfpost, wg, wu, wd, tm):
    rows = x.shape[0]
    half = pl.BlockSpec((tm, 512), lambda i: (i, 0))
    full = pl.BlockSpec((tm, D_MODEL), lambda i: (i, 0))
    vec = _const_spec((1, D_MODEL))
    return pl.pallas_call(
        _finish_kernel,
        grid=(rows // tm,),
        in_specs=[half, half, full, _const_spec(wo.shape), vec, vec, vec,
                  _const_spec(wg.shape), _const_spec(wu.shape), _const_spec(wd.shape)],
        out_specs=full,
        out_shape=jax.ShapeDtypeStruct((rows, D_MODEL), F32),
        name="finish",
        compiler_params=pltpu.CompilerParams(
            dimension_semantics=("arbitrary",),
            vmem_limit_bytes=VMEM_LIMIT_BYTES),
    )(a, c, x, wo, gpost, gpre, gfpost, wg, wu, wd)


def kernel(x_prompt, x_sample, cache_k, cache_v, state_conv, page_table, meta_tokens,
           ln_mix_pre, ln_mix_post, w_in, lambda_q1, lambda_k1, lambda_q2, lambda_k2,
           subln_w, conv_w, conv_b, conv_ln_w, conv_ln_b, w_out, ln_ffn_pre, ln_ffn_post,
           w_gate, w_up, w_down):
    B, S, _ = x_prompt.shape
    Bd = x_sample.shape[0]
    T = S + N_META
    li = 0
    w_in_bf = w_in[li].astype(BF16)
    wo_bf = w_out[li].astype(BF16)
    wg_bf = w_gate[li].astype(BF16)
    wu_bf = w_up[li].astype(BF16)
    wd_bf = w_down[li].astype(BF16)
    g_pre = ln_mix_pre[li][None]
    lam_args = (lambda_q1[li][None], lambda_k1[li][None], lambda_q2[li][None],
                lambda_k2[li][None], subln_w[li][None])
    conv_args = (conv_w[li], conv_b[li][None], conv_ln_w[li][None], conv_ln_b[li][None])
    fin_args = (wo_bf, ln_mix_post[li][None], ln_ffn_pre[li][None], ln_ffn_post[li][None],
                wg_bf, wu_bf, wd_bf)

    _, mk, mv, mu = _proj_call(meta_tokens, g_pre, w_in_bf)
    q_bf, k_bf, v_bf, new_k, new_v, u_ext = _prompt_proj_call(x_prompt, g_pre, w_in_bf, mk, mv, mu)
    pad = ((0, META_BLK - N_META), (0, 0))
    attn_o = _attn_call(q_bf, k_bf, v_bf, jnp.pad(mk.astype(BF16), pad), jnp.pad(mv.astype(BF16), pad),
                        *lam_args[:4], subln_w[li][:, None])
    cw_tiles = jnp.broadcast_to(conv_w[li][:, None, :], (CONV_WIDTH, 8, CONV_DIM))
    conv_o = _conv_call(u_ext, cw_tiles, *conv_args[1:])

    n_pool = cache_k.shape[1]
    qs, ks, vs, us = _proj_call(x_sample.reshape(Bd, D_MODEL), g_pre, w_in_bf)
    ck = jnp.transpose(cache_k[li], (0, 2, 3, 1)).reshape(n_pool, ATTN_WIDTH, PAGE_SIZE)
    cv = cache_v[li].reshape(n_pool, PAGE_SIZE * N_HEADS, V_HEAD_DIM)
    qcol = jnp.broadcast_to(qs[:, :, None], (Bd, ATTN_WIDTH, PAGE_SIZE))
    y_prompt, attn_s = _finish_decode_call(
        page_table, attn_o.reshape(B * S, 512), conv_o.reshape(B * S, 512),
        x_prompt.reshape(B * S, D_MODEL), *fin_args,
        qs[:, None], qcol, ks[:, None], vs[:, None], *lam_args, ck, cv, tm=B * S // Bd)
    conv_s = _sample_conv_call(state_conv[li], us, *conv_args)
    y_sample = _finish_call(attn_s.reshape(Bd, 512), conv_s, x_sample.reshape(Bd, D_MODEL),
                            *fin_args, tm=Bd)

    return (y_prompt.reshape(B, S, D_MODEL),
            y_sample.reshape(Bd, 1, D_MODEL),
            new_k.reshape(1, B, T, N_MAPS, HEAD_DIM),
            new_v.reshape(1, B, T, N_HEADS, V_HEAD_DIM),
            u_ext[:, S + U_PAD - (CONV_WIDTH - 1):][None],
            ks.reshape(1, Bd, 1, N_MAPS, HEAD_DIM),
            vs.reshape(1, Bd, 1, N_HEADS, V_HEAD_DIM),
            jnp.concatenate([state_conv[li][:, 1:], us[:, None]], axis=1)[None])
```

```python
import functools
import math

import jax
import jax.numpy as jnp
from jax import lax
from jax.experimental import pallas as pl
from jax.experimental.pallas import tpu as pltpu

D_MODEL = 1024
N_META = 16
ATTN_WIDTH = 512
CONV_DIM = 512
N_HEADS = 4
HEAD_DIM = 64
V_HEAD_DIM = 128
N_MAPS = 2 * N_HEADS
CONV_WIDTH = 31
FFN_HIDDEN = 2816
PAGE_SIZE = 128
EPS = 1e-6
NEG_INF = -1e30
LAMBDA_INIT = 0.8 - 0.6 * math.exp(-0.3 * 0)
SCALE = HEAD_DIM ** -0.5
SLOPES = tuple(2.0 ** (-8.0 * (h + 1) / N_HEADS) for h in range(N_HEADS))

VMEM_LIMIT_BYTES = 56 * 1024 * 1024

F32 = jnp.float32
BF16 = jnp.bfloat16


def _rms(x, g):
    return x * lax.rsqrt(jnp.mean(x * x, axis=-1, keepdims=True) + EPS) * g


def _const_spec(shape):
    nd = len(shape)
    return pl.BlockSpec(shape, lambda *_: (0,) * nd, pipeline_mode=pl.Buffered(1))


def _project(x, g, w_ref):
    xn = _rms(x, g).astype(BF16)
    seg = lambda c: jnp.dot(xn, w_ref[:, c * 512:(c + 1) * 512], preferred_element_type=F32)
    q = seg(0) * SCALE
    k = seg(1)
    v = seg(2)
    u = seg(3) * jax.nn.sigmoid(seg(4))
    return q, k, v, u


def _proj_kernel(x_ref, g_ref, w_ref, q_ref, k_ref, v_ref, u_ref):
    q, k, v, u = _project(x_ref[...], g_ref[...], w_ref)
    q_ref[...] = q
    k_ref[...] = k
    v_ref[...] = v
    u_ref[...] = u


def _proj_call(x, g, w_bf):
    rows = x.shape[0]
    out = jax.ShapeDtypeStruct((rows, 512), F32)
    return pl.pallas_call(
        _proj_kernel,
        out_shape=(out, out, out, out),
        name="proj_small",
        compiler_params=pltpu.CompilerParams(vmem_limit_bytes=VMEM_LIMIT_BYTES),
    )(x, g, w_bf)


PROJ_TM = 512
U_PAD = 48
U_ROWS_OFF = U_PAD - (CONV_WIDTH - 1)


CONV_SUB = 32
PROJ_RB = 512


def _ln_swish(y, w, b):
    mu = jnp.mean(y, axis=-1, keepdims=True)
    yc = y - mu
    z = yc * lax.rsqrt(jnp.mean(yc * yc, axis=-1, keepdims=True) + EPS) * w + b
    return z * jax.nn.sigmoid(z)


def _conv_rows(u_ref, win_start, cw_ref, cb_ref):
    lead = U_ROWS_OFF % 8
    win = u_ref[pl.ds(win_start, CONV_SUB + 32), :]
    acc = jnp.broadcast_to(cb_ref[...], (CONV_SUB, CONV_DIM))
    for res in range(8):
        rows = CONV_SUB + (8 if res else 0)
        part = None
        for w in range(CONV_WIDTH):
            if (lead + w) % 8 == res:
                off = lead + w - res
                term = win[off:off + rows, :].reshape(rows // 8, 8, CONV_DIM) * cw_ref[w]
                part = term if part is None else part + term
        acc = acc + part.reshape(rows, CONV_DIM)[res:res + CONV_SUB, :]
    return acc


def _prompt_proj_kernel(x_ref, g_ref, w_ref, mk_ref, mv_ref, mu_ref,
                        cw_ref, cb_ref, lw_ref, lb_ref,
                        q_ref, kb_ref, vb_ref, nk_ref, nv_ref, co_ref, nc_ref,
                        u_ref, y_ref):
    i = pl.program_id(1)
    nt = pl.num_programs(1)

    @pl.when(i == 0)
    def _():
        nk_ref[0, 0:N_META, :] = mk_ref[...]
        nv_ref[0, 0:N_META, :] = mv_ref[...]
        u_ref[0:U_PAD - N_META, :] = jnp.zeros((U_PAD - N_META, CONV_DIM), F32)
        u_ref[U_PAD - N_META:U_PAD, :] = mu_ref[...]

    xn = _rms(x_ref[0], g_ref[...]).astype(BF16)
    seg = lambda c: jnp.dot(xn, w_ref[:, c * 512:(c + 1) * 512], preferred_element_type=F32)
    r = i * PROJ_TM
    u_ref[pl.ds(pl.multiple_of(r + U_PAD, 8), PROJ_TM), :] = seg(3) * jax.nn.sigmoid(seg(4))

    win0 = r + U_ROWS_OFF - U_ROWS_OFF % 8
    n_sub = PROJ_TM // CONV_SUB

    def conv_passes(lo, hi):
        for s in range(lo, hi):
            y_ref[s * CONV_SUB:(s + 1) * CONV_SUB, :] = _conv_rows(
                u_ref, pl.multiple_of(win0 + s * CONV_SUB, 8), cw_ref, cb_ref)

    n_blk = PROJ_TM // PROJ_RB
    for rb in range(n_blk):
        rows = slice(rb * PROJ_RB, (rb + 1) * PROJ_RB)
        out_rows = pl.ds(pl.multiple_of(r + N_META + rb * PROJ_RB, 8), PROJ_RB)
        blk = lambda c: jnp.dot(xn[rows], w_ref[:, c * 512:(c + 1) * 512],
                                preferred_element_type=F32)
        q_ref[0, rows, :] = (blk(0) * SCALE).astype(BF16)
        k = blk(1)
        kb_ref[0, rows, :] = k.astype(BF16)
        nk_ref[0, out_rows, :] = k
        v = blk(2)
        vb_ref[0, rows, :] = v.astype(BF16)
        nv_ref[0, out_rows, :] = v
        conv_passes(rb * n_sub // n_blk, (rb + 1) * n_sub // n_blk)
    co_ref[0] = _ln_swish(y_ref[...], lw_ref[...], lb_ref[...]).astype(BF16)

    @pl.when(i == nt - 1)
    def _():
        last = nt * PROJ_TM + U_PAD
        nc_ref[0] = u_ref[last - (CONV_WIDTH - 1):last, :]


def _prompt_proj_call(x, g, w_bf, mk, mv, mu, cw_tiles, cb, lw, lb):
    B, S, _ = x.shape
    T = S + N_META
    nt = S // PROJ_TM
    tile = pl.BlockSpec((1, PROJ_TM, 512), lambda b, i: (b, i, 0))
    full = lambda rows: pl.BlockSpec((1, rows, 512), lambda b, i: (b, 0, 0))
    bf = jax.ShapeDtypeStruct((B, S, 512), BF16)
    vec = _const_spec((1, CONV_DIM))
    return pl.pallas_call(
        _prompt_proj_kernel,
        grid=(B, nt),
        in_specs=[
            pl.BlockSpec((1, PROJ_TM, D_MODEL), lambda b, i: (b, i, 0)),
            _const_spec((1, D_MODEL)),
            _const_spec(w_bf.shape),
            _const_spec((N_META, 512)),
            _const_spec((N_META, 512)),
            _const_spec((N_META, 512)),
            _const_spec((CONV_WIDTH, 8, CONV_DIM)), vec, vec, vec,
        ],
        out_specs=(tile, tile, tile, full(T), full(T), tile, full(CONV_WIDTH - 1)),
        out_shape=(bf, bf, bf,
                   jax.ShapeDtypeStruct((B, T, 512), F32),
                   jax.ShapeDtypeStruct((B, T, 512), F32),
                   bf,
                   jax.ShapeDtypeStruct((B, CONV_WIDTH - 1, CONV_DIM), F32)),
        scratch_shapes=[pltpu.VMEM((S + U_PAD, CONV_DIM), F32),
                        pltpu.VMEM((PROJ_TM, CONV_DIM), F32)],
        name="prompt_proj",
        compiler_params=pltpu.CompilerParams(
            dimension_semantics=("arbitrary", "arbitrary"),
            vmem_limit_bytes=VMEM_LIMIT_BYTES),
    )(x, g, w_bf, mk, mv, mu, cw_tiles, cb, lw, lb)


def _sample_conv_kernel(st_ref, u_ref, cw_ref, cb_ref, lw_ref, lb_ref, o_ref, y_ref):
    past_w = cw_ref[0:CONV_WIDTH - 1, :]
    for b in range(st_ref.shape[0]):
        y_ref[b:b + 1, :] = jnp.sum(st_ref[b] * past_w, axis=0, keepdims=True)
    acc = y_ref[...] + cb_ref[...] + u_ref[...] * cw_ref[CONV_WIDTH - 1:CONV_WIDTH, :]
    o_ref[...] = _ln_swish(acc, lw_ref[...], lb_ref[...]).astype(BF16)


def _sample_conv_call(state, u, cw, cb, lw, lb):
    return pl.pallas_call(
        _sample_conv_kernel,
        out_shape=jax.ShapeDtypeStruct(u.shape, BF16),
        scratch_shapes=[pltpu.VMEM(u.shape, F32)],
        name="sample_conv",
    )(state, u, cw, cb, lw, lb)


def _lambda(lq1, lk1, lq2, lk2):
    s1 = jnp.sum(lq1 * lk1, axis=-1, keepdims=True)
    s2 = jnp.sum(lq2 * lk2, axis=-1, keepdims=True)
    return jnp.exp(s1) - jnp.exp(s2) + LAMBDA_INIT


def _head_out(o0, o1, lam, subln):
    o = o0 - lam * o1
    return _rms(o, subln) * (1.0 - LAMBDA_INIT)


ATT_TQ = 256
ATT_TK = 256
META_BLK = 128
QK_AHEAD = 4


def _augment(x, pos, is_query, slope):
    lane = lax.broadcasted_iota(jnp.int32, x.shape, 1)
    hi = (pos >> 6).astype(F32)
    lo = (pos & 63).astype(F32)
    out = []
    for j in range(2):
        a = lane - (64 if j == 0 else 0)
        if is_query:
            aug = jnp.where(a == 0, hi * (-64.0 * slope),
                  jnp.where(a == 1, lo * (-slope),
                  jnp.where(a == 2, 64.0 * slope,
                  jnp.where(a == 3, slope, 0.0))))
        else:
            aug = jnp.where(a == 0, 1.0,
                  jnp.where(a == 1, 1.0,
                  jnp.where(a == 2, hi,
                  jnp.where(a == 3, lo, 0.0))))
        own = (lane < 64) if j == 0 else (lane >= 64)
        out.append(jnp.where(own, x, aug).astype(BF16))
    return out


def _attn_kernel(q_ref, kb_ref, vb_ref, mkb_ref, mvb_ref,
                 lq1_ref, lk1_ref, lq2_ref, lk2_ref, subcol_ref,
                 o_ref, kexp_ref, vt_ref, qexp_ref, m_ref, l_ref, acc_ref):
    qi = pl.program_id(1)
    S = kb_ref.shape[1]

    def chunk_step(k_rows, v_cols, mask, first):
        def scores(mm):
            mcols = slice(mm * 128, (mm + 1) * 128)
            return lax.dot_general(kexp_ref[k_rows, mcols], qexp_ref[:, mcols],
                                   (((1,), (1,)), ((), ())), preferred_element_type=F32)

        pending = [scores(mm) for mm in range(QK_AHEAD)]
        for mm in range(N_MAPS):
            h = mm // 2
            vt = vt_ref[h * 128:(h + 1) * 128, v_cols]
            if mm + QK_AHEAD < N_MAPS:
                pending.append(scores(mm + QK_AHEAD))
            s = pending[mm]
            if mask is not None:
                s = jnp.where(mask, s, NEG_INF)
            m_cur = jnp.max(s, axis=0, keepdims=True)
            m_new = m_cur if first else jnp.maximum(m_ref[mm], m_cur)
            p = jnp.exp(s - m_new)
            l_cur = jnp.sum(p, axis=0, keepdims=True)
            pb = p.astype(BF16)
            if pb.shape[0] < vt.shape[1]:
                pb = jnp.concatenate(
                    [pb, jnp.zeros((vt.shape[1] - pb.shape[0], pb.shape[1]), BF16)], axis=0)
            pv = jnp.dot(vt, pb, preferred_element_type=F32)
            if first:
                l_ref[mm] = l_cur
                acc_ref[mm] = pv
            else:
                alpha = jnp.exp(m_ref[mm] - m_new)
                l_ref[mm] = alpha * l_ref[mm] + l_cur
                acc_ref[mm] = alpha * acc_ref[mm] + pv
            m_ref[mm] = m_new

    @pl.when(qi == 0)
    def _():
        def put_vt(cols_off, vc):
            for h in range(N_HEADS):
                vt_ref[h * 128:(h + 1) * 128, cols_off:cols_off + vc.shape[0]] = (
                    vc[:, h * 128:(h + 1) * 128].astype(F32).T.astype(BF16))

        put_vt(0, mvb_ref[...])
        for c in range(S // ATT_TK):
            put_vt(META_BLK + c * ATT_TK, vb_ref[0, c * ATT_TK:(c + 1) * ATT_TK, :])

        def put(rows_off, kc, pos0):
            rows = kc.shape[0]
            pos = pos0 + lax.broadcasted_iota(jnp.int32, (rows, 128), 0)
            for h in range(N_HEADS):
                e0, e1 = _augment(kc[:, h * 128:(h + 1) * 128].astype(F32), pos, False, 0.0)
                kexp_ref[rows_off:rows_off + rows, (2 * h) * 128:(2 * h + 1) * 128] = e0
                kexp_ref[rows_off:rows_off + rows, (2 * h + 1) * 128:(2 * h + 2) * 128] = e1

        put(0, mkb_ref[...], 0)
        for c in range(S // ATT_TK):
            put(META_BLK + c * ATT_TK, kb_ref[0, c * ATT_TK:(c + 1) * ATT_TK, :],
                N_META + c * ATT_TK)

    qpos = N_META + qi * ATT_TQ + lax.broadcasted_iota(jnp.int32, (ATT_TQ, 128), 0)
    q = q_ref[0].astype(F32)
    for h in range(N_HEADS):
        e0, e1 = _augment(q[:, h * 128:(h + 1) * 128], qpos, True, SLOPES[h])
        qexp_ref[:, (2 * h) * 128:(2 * h + 1) * 128] = e0
        qexp_ref[:, (2 * h + 1) * 128:(2 * h + 2) * 128] = e1

    lam = _lambda(lq1_ref[...], lk1_ref[...], lq2_ref[...], lk2_ref[...])
    key = lax.broadcasted_iota(jnp.int32, (ATT_TK, ATT_TQ), 0)
    qry = lax.broadcasted_iota(jnp.int32, (ATT_TK, ATT_TQ), 1)
    causal = key <= qry

    chunk_step(slice(0, N_META), slice(0, META_BLK), None, True)

    def real_chunk(c, mask):
        r = pl.multiple_of(META_BLK + c * ATT_TK, 128)
        chunk_step(pl.ds(r, ATT_TK), pl.ds(r, ATT_TK), mask, False)

    def body(c, carry):
        real_chunk(c, None)
        return carry

    lax.fori_loop(0, qi, body, 0)
    real_chunk(qi, causal)

    for h in range(N_HEADS):
        o0 = acc_ref[2 * h] / l_ref[2 * h]
        o1 = acc_ref[2 * h + 1] / l_ref[2 * h + 1]
        o = o0 - lam * o1
        o = o * lax.rsqrt(jnp.mean(o * o, axis=0, keepdims=True) + EPS)
        o = o * subcol_ref[...] * (1.0 - LAMBDA_INIT)
        o_ref[0, :, h * 128:(h + 1) * 128] = o.T.astype(BF16)


def _attn_call(q, kb, vb, mkb, mvb, lq1, lk1, lq2, lk2, subln):
    B, S, _ = q.shape
    per_b = pl.BlockSpec((1, S, 512), lambda b, i: (b, 0, 0))
    tile = pl.BlockSpec((1, ATT_TQ, 512), lambda b, i: (b, i, 0))
    small = _const_spec((1, HEAD_DIM))
    return pl.pallas_call(
        _attn_kernel,
        grid=(B, S // ATT_TQ),
        in_specs=[tile, per_b, per_b,
                  _const_spec((META_BLK, 512)), _const_spec((META_BLK, 512)),
                  small, small, small, small, _const_spec((V_HEAD_DIM, 1))],
        out_specs=tile,
        out_shape=jax.ShapeDtypeStruct((B, S, 512), BF16),
        scratch_shapes=[pltpu.VMEM((META_BLK + S, N_MAPS * 128), BF16),
                        pltpu.VMEM((ATTN_WIDTH, META_BLK + S), BF16),
                        pltpu.VMEM((ATT_TQ, N_MAPS * 128), BF16),
                        pltpu.VMEM((N_MAPS, 1, ATT_TQ), F32),
                        pltpu.VMEM((N_MAPS, 1, ATT_TQ), F32),
                        pltpu.VMEM((N_MAPS, V_HEAD_DIM, ATT_TQ), F32)],
        name="attn",
        compiler_params=pltpu.CompilerParams(
            dimension_semantics=("arbitrary", "arbitrary"),
            vmem_limit_bytes=VMEM_LIMIT_BYTES),
    )(q, kb, vb, mkb, mvb, lq1, lk1, lq2, lk2, subln)


KEYS_PER_VBLK = 128 // N_HEADS


def _decode_consts():
    sub = lax.broadcasted_iota(jnp.int32, (N_MAPS, PAGE_SIZE), 0)
    lane = lax.broadcasted_iota(jnp.int32, (N_MAPS, PAGE_SIZE), 1)
    head = sub >> 1
    slope = jnp.where(head == 0, SLOPES[0],
            jnp.where(head == 1, SLOPES[1],
            jnp.where(head == 2, SLOPES[2], SLOPES[3])))
    return dict(lane=lane, head=head, slope=slope,
                own_head=(lane & (N_HEADS - 1)) == head, key_of_row=lane >> 2)


def _decode_init(q_row, k_new, v_new, head):
    sub_w = lax.broadcasted_iota(jnp.int32, (N_MAPS, ATTN_WIDTH), 0)
    lane_w = lax.broadcasted_iota(jnp.int32, (N_MAPS, ATTN_WIDTH), 1)
    qbd = jnp.where((lane_w >> 6) == sub_w, jnp.broadcast_to(q_row, (N_MAPS, ATTN_WIDTH)), 0.0)
    m = jnp.sum(qbd * k_new, axis=-1, keepdims=True)
    acc = jnp.zeros((N_MAPS, V_HEAD_DIM), F32)
    for h in range(N_HEADS):
        acc = jnp.where(head == h, v_new[:, h * 128:(h + 1) * 128], acc)
    return m, jnp.ones((N_MAPS, 1), F32), acc


def _decode_probs(m, l, qcol, k_pages, first_page, past, cst):
    scores = []
    for p, kp in enumerate(k_pages):
        s = jnp.sum((kp * qcol).reshape(N_MAPS, HEAD_DIM, PAGE_SIZE), axis=1)
        dist = past - ((first_page + p) * PAGE_SIZE + cst["lane"])
        scores.append(s - cst["slope"] * dist.astype(F32))
    s = jnp.concatenate(scores, axis=1)
    m_new = jnp.maximum(m, jnp.max(s, axis=-1, keepdims=True))
    alpha = jnp.exp(m - m_new)
    pr = jnp.exp(s - m_new)
    l = alpha * l + jnp.sum(pr, axis=-1, keepdims=True)
    prb = pr.astype(BF16).astype(F32)
    spread_probs = []
    for p in range(len(k_pages)):
        prp = prb[:, p * PAGE_SIZE:(p + 1) * PAGE_SIZE]
        spread = [jnp.where(cst["own_head"],
                            jnp.take_along_axis(prp, cst["key_of_row"] + c * KEYS_PER_VBLK, axis=1),
                            0.0)
                  for c in range(N_HEADS)]
        spread_probs.append(jnp.concatenate(spread, axis=1).astype(BF16))
    return m_new, l, alpha, spread_probs


def _decode_values(acc, alpha, spread_probs, v_pages):
    pv = jnp.zeros((N_MAPS, V_HEAD_DIM), F32)
    for pe, vp in zip(spread_probs, v_pages):
        pv = pv + jnp.dot(pe, vp.astype(BF16), preferred_element_type=F32)
    return alpha * acc + pv


def _decode_out(l, acc, lam, subln, o_ref):
    o = acc / l
    for h in range(N_HEADS):
        ho = _head_out(o[2 * h:2 * h + 1], o[2 * h + 1:2 * h + 2], lam, subln)
        o_ref[0, :, h * 128:(h + 1) * 128] = ho.astype(BF16)


FFN_CHUNK = 256


N_FFN_CHUNKS = FFN_HIDDEN // FFN_CHUNK


def _finish_rows(a_ref, c_ref, x_ref, wo_ref, gpost_ref, gpre_ref, gfpost_ref,
                 wg_ref, wu_ref, wd_ref, o_ref, before_chunk=None, after_chunk=None):
    mix = (jnp.dot(a_ref[...], wo_ref[0:ATTN_WIDTH, :], preferred_element_type=F32)
           + jnp.dot(c_ref[...], wo_ref[ATTN_WIDTH:, :], preferred_element_type=F32))
    x = x_ref[...] + _rms(mix, gpost_ref[...])
    hb = _rms(x, gpre_ref[...]).astype(BF16)
    f = jnp.zeros(x.shape, F32)
    for c in range(N_FFN_CHUNKS):
        cols = slice(c * FFN_CHUNK, (c + 1) * FFN_CHUNK)
        if before_chunk is not None:
            before_chunk(c)
        gate = jnp.dot(hb, wg_ref[:, cols], preferred_element_type=F32)
        up = jnp.dot(hb, wu_ref[:, cols], preferred_element_type=F32)
        act = (gate * jax.nn.sigmoid(gate) * up).astype(BF16)
        f = f + jnp.dot(act, wd_ref[cols, :], preferred_element_type=F32)
        if after_chunk is not None:
            after_chunk(c)
    o_ref[...] = x + _rms(f, gfpost_ref[...])


def _finish_kernel(*refs):
    _finish_rows(*refs)


DEC_GROUP = 8
DEC_AHEAD = 2
DEC_SLOTS = DEC_AHEAD + 1


def _finish_decode_kernel(pt_ref, a_ref, c_ref, x_ref, wo_ref, gpost_ref, gpre_ref, gfpost_ref,
                          wg_ref, wu_ref, wd_ref,
                          q_ref, qcol_ref, kn_ref, vn_ref,
                          lq1_ref, lk1_ref, lq2_ref, lk2_ref, sub_ref, ck_hbm, cv_hbm,
                          o_ref, os_ref, kbuf, vbuf, sem):
    i = pl.program_id(0)
    n_steps = pl.num_programs(0)
    n_pages = pt_ref.shape[1]
    n_groups = n_pages // DEC_GROUP
    past = n_pages * PAGE_SIZE

    def group_copies(b, g, slot):
        copies = []
        for p in range(DEC_GROUP):
            page = pt_ref[b, g * DEC_GROUP + p]
            copies.append(pltpu.make_async_copy(ck_hbm.at[page], kbuf.at[slot, p], sem.at[0, slot]))
            copies.append(pltpu.make_async_copy(cv_hbm.at[page], vbuf.at[slot, p], sem.at[1, slot]))
        return copies

    def slot_of(g):
        return lax.rem(i * n_groups + g, DEC_SLOTS)

    def start_group(g):
        if g < n_groups:
            for cp in group_copies(i, g, slot_of(g)):
                cp.start()
        else:
            @pl.when(i + 1 < n_steps)
            def _():
                for cp in group_copies(i + 1, g - n_groups, slot_of(g)):
                    cp.start()

    @pl.when(i == 0)
    def _():
        for g in range(DEC_AHEAD):
            start_group(g)

    cst = _decode_consts()
    qcol = qcol_ref[0]
    m, l, acc = _decode_init(q_ref[0], kn_ref[0], vn_ref[0], cst["head"])
    st = dict(m=m, l=l, acc=acc)

    def before_chunk(g):
        if g >= n_groups:
            return
        start_group(g + DEC_AHEAD)
        slot = slot_of(g)
        for cp in group_copies(i, g, slot):
            cp.wait()
        st["m"], st["l"], st["alpha"], st["probs"] = _decode_probs(
            st["m"], st["l"], qcol, [kbuf[slot, p] for p in range(DEC_GROUP)],
            g * DEC_GROUP, past, cst)

    def after_chunk(g):
        if g >= n_groups:
            return
        slot = slot_of(g)
        st["acc"] = _decode_values(st["acc"], st["alpha"], st["probs"],
                                   [vbuf[slot, p] for p in range(DEC_GROUP)])

    _finish_rows(a_ref, c_ref, x_ref, wo_ref, gpost_ref, gpre_ref, gfpost_ref,
                 wg_ref, wu_ref, wd_ref, o_ref, before_chunk, after_chunk)
    lam = _lambda(lq1_ref[...], lk1_ref[...], lq2_ref[...], lk2_ref[...])
    _decode_out(st["l"], st["acc"], lam, sub_ref[...], os_ref)


def _finish_decode_call(page_table, a, c, x, wo, gpost, gpre, gfpost, wg, wu, wd,
                        q, qcol, kn, vn, lq1, lk1, lq2, lk2, subln, ck, cv, tm):
    rows = x.shape[0]
    Bd, n_pages = page_table.shape
    assert rows // tm == Bd and n_pages % DEC_GROUP == 0
    assert DEC_AHEAD <= n_pages // DEC_GROUP <= N_FFN_CHUNKS
    half = pl.BlockSpec((tm, 512), lambda i, pt: (i, 0))
    full = pl.BlockSpec((tm, D_MODEL), lambda i, pt: (i, 0))
    vec = _const_spec((1, D_MODEL))
    row = pl.BlockSpec((1, 1, ATTN_WIDTH), lambda i, pt: (i, 0, 0))
    small = _const_spec((1, HEAD_DIM))
    hbm = pl.BlockSpec(memory_space=pl.ANY)
    grid_spec = pltpu.PrefetchScalarGridSpec(
        num_scalar_prefetch=1,
        grid=(Bd,),
        in_specs=[half, half, full, _const_spec(wo.shape), vec, vec, vec,
                  _const_spec(wg.shape), _const_spec(wu.shape), _const_spec(wd.shape),
                  row, pl.BlockSpec((1, ATTN_WIDTH, PAGE_SIZE), lambda i, pt: (i, 0, 0)),
                  row, row, small, small, small, small, _const_spec((1, V_HEAD_DIM)),
                  hbm, hbm],
        out_specs=(full, row),
        scratch_shapes=[pltpu.VMEM((DEC_SLOTS, DEC_GROUP, ATTN_WIDTH, PAGE_SIZE), F32),
                        pltpu.VMEM((DEC_SLOTS, DEC_GROUP, ATTN_WIDTH, PAGE_SIZE), F32),
                        pltpu.SemaphoreType.DMA((2, DEC_SLOTS))],
    )
    return pl.pallas_call(
        _finish_decode_kernel,
        grid_spec=grid_spec,
        out_shape=(jax.ShapeDtypeStruct((rows, D_MODEL), F32),
                   jax.ShapeDtypeStruct((Bd, 1, ATTN_WIDTH), BF16)),
        name="finish_decode",
        compiler_params=pltpu.CompilerParams(
            dimension_semantics=("arbitrary",),
            vmem_limit_bytes=VMEM_LIMIT_BYTES),
    )(page_table, a, c, x, wo, gpost, gpre, gfpost, wg, wu, wd,
      q, qcol, kn, vn, lq1, lk1, lq2, lk2, subln, ck, cv)


def _finish_call(a, c, x, wo, gpost, gpre, gfpost, wg, wu, wd, tm):
    rows = x.shape[0]
    half = pl.BlockSpec((tm, 512), lambda i: (i, 0))
    full = pl.BlockSpec((tm, D_MODEL), lambda i: (i, 0))
    vec = _const_spec((1, D_MODEL))
    return pl.pallas_call(
        _finish_kernel,
        grid=(rows // tm,),
        in_specs=[half, half, full, _const_spec(wo.shape), vec, vec, vec,
                  _const_spec(wg.shape), _const_spec(wu.shape), _const_spec(wd.shape)],
        out_specs=full,
        out_shape=jax.ShapeDtypeStruct((rows, D_MODEL), F32),
        name="finish",
        compiler_params=pltpu.CompilerParams(
            dimension_semantics=("arbitrary",),
            vmem_limit_bytes=VMEM_LIMIT_BYTES),
    )(a, c, x, wo, gpost, gpre, gfpost, wg, wu, wd)


def kernel(x_prompt, x_sample, cache_k, cache_v, state_conv, page_table, meta_tokens,
           ln_mix_pre, ln_mix_post, w_in, lambda_q1, lambda_k1, lambda_q2, lambda_k2,
           subln_w, conv_w, conv_b, conv_ln_w, conv_ln_b, w_out, ln_ffn_pre, ln_ffn_post,
           w_gate, w_up, w_down):
    B, S, _ = x_prompt.shape
    Bd = x_sample.shape[0]
    T = S + N_META
    li = 0
    w_in_bf = w_in[li].astype(BF16)
    wo_bf = w_out[li].astype(BF16)
    wg_bf = w_gate[li].astype(BF16)
    wu_bf = w_up[li].astype(BF16)
    wd_bf = w_down[li].astype(BF16)
    g_pre = ln_mix_pre[li][None]
    lam_args = (lambda_q1[li][None], lambda_k1[li][None], lambda_q2[li][None],
                lambda_k2[li][None], subln_w[li][None])
    conv_args = (conv_w[li], conv_b[li][None], conv_ln_w[li][None], conv_ln_b[li][None])
    fin_args = (wo_bf, ln_mix_post[li][None], ln_ffn_pre[li][None], ln_ffn_post[li][None],
                wg_bf, wu_bf, wd_bf)

    _, mk, mv, mu = _proj_call(meta_tokens, g_pre, w_in_bf)
    cw_tiles = jnp.broadcast_to(conv_w[li][:, None, :], (CONV_WIDTH, 8, CONV_DIM))
    q_bf, k_bf, v_bf, new_k, new_v, conv_o, new_conv = _prompt_proj_call(
        x_prompt, g_pre, w_in_bf, mk, mv, mu, cw_tiles, *conv_args[1:])
    pad = ((0, META_BLK - N_META), (0, 0))
    attn_o = _attn_call(q_bf, k_bf, v_bf, jnp.pad(mk.astype(BF16), pad), jnp.pad(mv.astype(BF16), pad),
                        *lam_args[:4], subln_w[li][:, None])

    n_pool = cache_k.shape[1]
    qs, ks, vs, us = _proj_call(x_sample.reshape(Bd, D_MODEL), g_pre, w_in_bf)
    ck = jnp.transpose(cache_k[li], (0, 2, 3, 1)).reshape(n_pool, ATTN_WIDTH, PAGE_SIZE)
    cv = cache_v[li].reshape(n_pool, PAGE_SIZE * N_HEADS, V_HEAD_DIM)
    qcol = jnp.broadcast_to(qs[:, :, None], (Bd, ATTN_WIDTH, PAGE_SIZE))
    y_prompt, attn_s = _finish_decode_call(
        page_table, attn_o.reshape(B * S, 512), conv_o.reshape(B * S, 512),
        x_prompt.reshape(B * S, D_MODEL), *fin_args,
        qs[:, None], qcol, ks[:, None], vs[:, None], *lam_args, ck, cv, tm=B * S // Bd)
    conv_s = _sample_conv_call(state_conv[li], us, *conv_args)
    y_sample = _finish_call(attn_s.reshape(Bd, 512), conv_s, x_sample.reshape(Bd, D_MODEL),
                            *fin_args, tm=Bd)

    return (y_prompt.reshape(B, S, D_MODEL),
            y_sample.reshape(Bd, 1, D_MODEL),
            new_k.reshape(1, B, T, N_MAPS, HEAD_DIM),
            new_v.reshape(1, B, T, N_HEADS, V_HEAD_DIM),
            new_conv[None],
            ks.reshape(1, Bd, 1, N_MAPS, HEAD_DIM),
            vs.reshape(1, Bd, 1, N_HEADS, V_HEAD_DIM),
            jnp.concatenate([state_conv[li][:, 1:], us[:, None]], axis=1)[None])
```

```python
import functools
import math

import jax
import jax.numpy as jnp
from jax import lax
from jax.experimental import pallas as pl
from jax.experimental.pallas import tpu as pltpu

D_MODEL = 1024
N_META = 16
ATTN_WIDTH = 512
CONV_DIM = 512
N_HEADS = 4
HEAD_DIM = 64
V_HEAD_DIM = 128
N_MAPS = 2 * N_HEADS
CONV_WIDTH = 31
FFN_HIDDEN = 2816
PAGE_SIZE = 128
EPS = 1e-6
NEG_INF = -1e30
LAMBDA_INIT = 0.8 - 0.6 * math.exp(-0.3 * 0)
SCALE = HEAD_DIM ** -0.5
SLOPES = tuple(2.0 ** (-8.0 * (h + 1) / N_HEADS) for h in range(N_HEADS))

VMEM_LIMIT_BYTES = 56 * 1024 * 1024

F32 = jnp.float32
BF16 = jnp.bfloat16


def _rms(x, g):
    return x * lax.rsqrt(jnp.mean(x * x, axis=-1, keepdims=True) + EPS) * g


def _const_spec(shape):
    nd = len(shape)
    return pl.BlockSpec(shape, lambda *_: (0,) * nd, pipeline_mode=pl.Buffered(1))


def _project(x, g, w_ref):
    xn = _rms(x, g).astype(BF16)
    seg = lambda c: jnp.dot(xn, w_ref[:, c * 512:(c + 1) * 512], preferred_element_type=F32)
    q = seg(0) * SCALE
    k = seg(1)
    v = seg(2)
    u = seg(3) * jax.nn.sigmoid(seg(4))
    return q, k, v, u


def _proj_kernel(x_ref, g_ref, w_ref, q_ref, k_ref, v_ref, u_ref):
    q, k, v, u = _project(x_ref[...], g_ref[...], w_ref)
    q_ref[...] = q
    k_ref[...] = k
    v_ref[...] = v
    u_ref[...] = u


def _proj_call(x, g, w_bf):
    rows = x.shape[0]
    out = jax.ShapeDtypeStruct((rows, 512), F32)
    return pl.pallas_call(
        _proj_kernel,
        out_shape=(out, out, out, out),
        name="proj_small",
        compiler_params=pltpu.CompilerParams(vmem_limit_bytes=VMEM_LIMIT_BYTES),
    )(x, g, w_bf)


PROJ_TM = 512
U_PAD = 48
U_ROWS_OFF = U_PAD - (CONV_WIDTH - 1)


CONV_SUB = 32
PROJ_RB = 256


def _ln_swish(y, w, b):
    mu = jnp.mean(y, axis=-1, keepdims=True)
    yc = y - mu
    z = yc * lax.rsqrt(jnp.mean(yc * yc, axis=-1, keepdims=True) + EPS) * w + b
    return z * jax.nn.sigmoid(z)


def _zero_after(x):
    bits = pltpu.bitcast(x[0:8, 0:128], jnp.uint32)
    z = ((bits >> 16) >> 16).astype(F32)[0:1, :]
    return jnp.concatenate([z] * (CONV_DIM // 128), axis=1)


def _conv_rows(u_ref, win_start, cw_ref, bias):
    lead = U_ROWS_OFF % 8
    win = u_ref[pl.ds(win_start, CONV_SUB + 32), :]
    acc = jnp.broadcast_to(bias, (CONV_SUB, CONV_DIM))
    for res in range(8):
        rows = CONV_SUB + (8 if res else 0)
        part = None
        for w in range(CONV_WIDTH):
            if (lead + w) % 8 == res:
                off = lead + w - res
                term = win[off:off + rows, :].reshape(rows // 8, 8, CONV_DIM) * cw_ref[w]
                part = term if part is None else part + term
        acc = acc + part.reshape(rows, CONV_DIM)[res:res + CONV_SUB, :]
    return acc


def _prompt_proj_kernel(x_ref, g_ref, w_ref, mk_ref, mv_ref, mu_ref,
                        cw_ref, cb_ref, lw_ref, lb_ref,
                        q_ref, kb_ref, vb_ref, nk_ref, nv_ref, co_ref, nc_ref,
                        u_ref, y_ref):
    i = pl.program_id(1)
    nt = pl.num_programs(1)

    @pl.when(i == 0)
    def _():
        nk_ref[0, 0:N_META, :] = mk_ref[...]
        nv_ref[0, 0:N_META, :] = mv_ref[...]
        u_ref[0:U_PAD - N_META, :] = jnp.zeros((U_PAD - N_META, CONV_DIM), F32)
        u_ref[U_PAD - N_META:U_PAD, :] = mu_ref[...]

    xn = _rms(x_ref[0], g_ref[...]).astype(BF16)
    seg = lambda c: jnp.dot(xn, w_ref[:, c * 512:(c + 1) * 512], preferred_element_type=F32)
    r = i * PROJ_TM
    u_ref[pl.ds(pl.multiple_of(r + U_PAD, 8), PROJ_TM), :] = seg(3) * jax.nn.sigmoid(seg(4))

    win0 = r + U_ROWS_OFF - U_ROWS_OFF % 8
    n_sub = PROJ_TM // CONV_SUB

    def conv_passes(lo, hi, bias):
        for s in range(lo, hi):
            y_ref[s * CONV_SUB:(s + 1) * CONV_SUB, :] = _conv_rows(
                u_ref, pl.multiple_of(win0 + s * CONV_SUB, 8), cw_ref, bias)

    n_blk = PROJ_TM // PROJ_RB
    for rb in range(n_blk):
        rows = slice(rb * PROJ_RB, (rb + 1) * PROJ_RB)
        out_rows = pl.ds(pl.multiple_of(r + N_META + rb * PROJ_RB, 8), PROJ_RB)
        blk = lambda c: jnp.dot(xn[rows], w_ref[:, c * 512:(c + 1) * 512],
                                preferred_element_type=F32)
        q_ref[0, rows, :] = (blk(0) * SCALE).astype(BF16)
        k = blk(1)
        kb_ref[0, rows, :] = k.astype(BF16)
        nk_ref[0, out_rows, :] = k
        v = blk(2)
        vb_ref[0, rows, :] = v.astype(BF16)
        nv_ref[0, out_rows, :] = v
        conv_passes(rb * n_sub // n_blk, (rb + 1) * n_sub // n_blk, cb_ref[...] + _zero_after(v))
    co_ref[0] = _ln_swish(y_ref[...], lw_ref[...], lb_ref[...]).astype(BF16)

    @pl.when(i == nt - 1)
    def _():
        last = nt * PROJ_TM + U_PAD
        nc_ref[0] = u_ref[last - (CONV_WIDTH - 1):last, :]


def _prompt_proj_call(x, g, w_bf, mk, mv, mu, cw_tiles, cb, lw, lb):
    B, S, _ = x.shape
    T = S + N_META
    nt = S // PROJ_TM
    tile = pl.BlockSpec((1, PROJ_TM, 512), lambda b, i: (b, i, 0))
    full = lambda rows: pl.BlockSpec((1, rows, 512), lambda b, i: (b, 0, 0))
    bf = jax.ShapeDtypeStruct((B, S, 512), BF16)
    vec = _const_spec((1, CONV_DIM))
    return pl.pallas_call(
        _prompt_proj_kernel,
        grid=(B, nt),
        in_specs=[
            pl.BlockSpec((1, PROJ_TM, D_MODEL), lambda b, i: (b, i, 0)),
            _const_spec((1, D_MODEL)),
            _const_spec(w_bf.shape),
            _const_spec((N_META, 512)),
            _const_spec((N_META, 512)),
            _const_spec((N_META, 512)),
            _const_spec((CONV_WIDTH, 8, CONV_DIM)), vec, vec, vec,
        ],
        out_specs=(tile, tile, tile, full(T), full(T), tile, full(CONV_WIDTH - 1)),
        out_shape=(bf, bf, bf,
                   jax.ShapeDtypeStruct((B, T, 512), F32),
                   jax.ShapeDtypeStruct((B, T, 512), F32),
                   bf,
                   jax.ShapeDtypeStruct((B, CONV_WIDTH - 1, CONV_DIM), F32)),
        scratch_shapes=[pltpu.VMEM((S + U_PAD, CONV_DIM), F32),
                        pltpu.VMEM((PROJ_TM, CONV_DIM), F32)],
        name="prompt_proj",
        compiler_params=pltpu.CompilerParams(
            dimension_semantics=("arbitrary", "arbitrary"),
            vmem_limit_bytes=VMEM_LIMIT_BYTES),
    )(x, g, w_bf, mk, mv, mu, cw_tiles, cb, lw, lb)


def _sample_conv_kernel(st_ref, u_ref, cw_ref, cb_ref, lw_ref, lb_ref, o_ref, y_ref):
    past_w = cw_ref[0:CONV_WIDTH - 1, :]
    for b in range(st_ref.shape[0]):
        y_ref[b:b + 1, :] = jnp.sum(st_ref[b] * past_w, axis=0, keepdims=True)
    acc = y_ref[...] + cb_ref[...] + u_ref[...] * cw_ref[CONV_WIDTH - 1:CONV_WIDTH, :]
    o_ref[...] = _ln_swish(acc, lw_ref[...], lb_ref[...]).astype(BF16)


def _sample_conv_call(state, u, cw, cb, lw, lb):
    return pl.pallas_call(
        _sample_conv_kernel,
        out_shape=jax.ShapeDtypeStruct(u.shape, BF16),
        scratch_shapes=[pltpu.VMEM(u.shape, F32)],
        name="sample_conv",
    )(state, u, cw, cb, lw, lb)


def _lambda(lq1, lk1, lq2, lk2):
    s1 = jnp.sum(lq1 * lk1, axis=-1, keepdims=True)
    s2 = jnp.sum(lq2 * lk2, axis=-1, keepdims=True)
    return jnp.exp(s1) - jnp.exp(s2) + LAMBDA_INIT


def _head_out(o0, o1, lam, subln):
    o = o0 - lam * o1
    return _rms(o, subln) * (1.0 - LAMBDA_INIT)


ATT_TQ = 256
ATT_TK = 256
ATT_TK_WIDE = 512
META_BLK = 128
QK_AHEAD = 4
VT_ROWS = V_HEAD_DIM + 16


def _augment(x, pos, is_query, slope):
    lane = lax.broadcasted_iota(jnp.int32, x.shape, 1)
    hi = (pos >> 6).astype(F32)
    lo = (pos & 63).astype(F32)
    out = []
    for j in range(2):
        a = lane - (64 if j == 0 else 0)
        if is_query:
            aug = jnp.where(a == 0, hi * (-64.0 * slope),
                  jnp.where(a == 1, lo * (-slope),
                  jnp.where(a == 2, 64.0 * slope,
                  jnp.where(a == 3, slope, 0.0))))
        else:
            aug = jnp.where(a == 0, 1.0,
                  jnp.where(a == 1, 1.0,
                  jnp.where(a == 2, hi,
                  jnp.where(a == 3, lo, 0.0))))
        own = (lane < 64) if j == 0 else (lane >= 64)
        out.append(jnp.where(own, x, aug).astype(BF16))
    return out


def _attn_kernel(q_ref, kb_ref, vb_ref, mkb_ref, mvb_ref,
                 lq1_ref, lk1_ref, lq2_ref, lk2_ref, subcol_ref,
                 o_ref, kexp_ref, vt_ref, qexp_ref, m_ref, acc_ref):
    qi = pl.program_id(1)
    S = kb_ref.shape[1]

    def chunk_step(k_rows, v_cols, mask, first):
        def scores(mm):
            mcols = slice(mm * 128, (mm + 1) * 128)
            return lax.dot_general(kexp_ref[k_rows, mcols], qexp_ref[:, mcols],
                                   (((1,), (1,)), ((), ())), preferred_element_type=F32)

        pending = [scores(mm) for mm in range(QK_AHEAD)]
        for mm in range(N_MAPS):
            h = mm // 2
            vt = vt_ref[h * VT_ROWS:(h + 1) * VT_ROWS, v_cols]
            if mm + QK_AHEAD < N_MAPS:
                pending.append(scores(mm + QK_AHEAD))
            s = pending[mm]
            if mask is not None:
                s = jnp.where(mask, s, NEG_INF)
            m_cur = jnp.max(s, axis=0, keepdims=True)
            m_new = m_cur if first else jnp.maximum(m_ref[mm], m_cur)
            pb = jnp.exp(s - m_new).astype(BF16)
            if pb.shape[0] < vt.shape[1]:
                pb = jnp.concatenate(
                    [pb, jnp.zeros((vt.shape[1] - pb.shape[0], pb.shape[1]), BF16)], axis=0)
            pv = jnp.dot(vt, pb, preferred_element_type=F32)
            if first:
                acc_ref[mm] = pv
            else:
                acc_ref[mm] = jnp.exp(m_ref[mm] - m_new) * acc_ref[mm] + pv
            m_ref[mm] = m_new

    @pl.when(qi == 0)
    def _():
        ones_rows = (lax.broadcasted_iota(jnp.int32, (VT_ROWS - V_HEAD_DIM, ATT_TK), 0) == 0)

        def put_vt(cols_off, vc):
            n = vc.shape[0]
            for h in range(N_HEADS):
                vt_ref[h * VT_ROWS:h * VT_ROWS + V_HEAD_DIM, cols_off:cols_off + n] = (
                    vc[:, h * 128:(h + 1) * 128].astype(F32).T.astype(BF16))
                vt_ref[h * VT_ROWS + V_HEAD_DIM:(h + 1) * VT_ROWS, cols_off:cols_off + n] = (
                    ones_rows[:, :n].astype(F32).astype(BF16))

        put_vt(0, mvb_ref[...])
        for c in range(S // ATT_TK):
            put_vt(META_BLK + c * ATT_TK, vb_ref[0, c * ATT_TK:(c + 1) * ATT_TK, :])

        def put(rows_off, kc, pos0):
            rows = kc.shape[0]
            pos = pos0 + lax.broadcasted_iota(jnp.int32, (rows, 128), 0)
            for h in range(N_HEADS):
                e0, e1 = _augment(kc[:, h * 128:(h + 1) * 128].astype(F32), pos, False, 0.0)
                kexp_ref[rows_off:rows_off + rows, (2 * h) * 128:(2 * h + 1) * 128] = e0
                kexp_ref[rows_off:rows_off + rows, (2 * h + 1) * 128:(2 * h + 2) * 128] = e1

        put(0, mkb_ref[...], 0)
        for c in range(S // ATT_TK):
            put(META_BLK + c * ATT_TK, kb_ref[0, c * ATT_TK:(c + 1) * ATT_TK, :],
                N_META + c * ATT_TK)

    qpos = N_META + qi * ATT_TQ + lax.broadcasted_iota(jnp.int32, (ATT_TQ, 128), 0)
    q = q_ref[0].astype(F32)
    for h in range(N_HEADS):
        e0, e1 = _augment(q[:, h * 128:(h + 1) * 128], qpos, True, SLOPES[h])
        qexp_ref[:, (2 * h) * 128:(2 * h + 1) * 128] = e0
        qexp_ref[:, (2 * h + 1) * 128:(2 * h + 2) * 128] = e1

    lam = _lambda(lq1_ref[...], lk1_ref[...], lq2_ref[...], lk2_ref[...])
    key = lax.broadcasted_iota(jnp.int32, (ATT_TK, ATT_TQ), 0)
    qry = lax.broadcasted_iota(jnp.int32, (ATT_TK, ATT_TQ), 1)
    causal = key <= qry

    chunk_step(slice(0, N_META), slice(0, META_BLK), None, True)

    def real_chunk(first_key, width, mask):
        r = pl.multiple_of(META_BLK + first_key, 128)
        chunk_step(pl.ds(r, width), pl.ds(r, width), mask, False)

    def body(c, carry):
        real_chunk(c * ATT_TK_WIDE, ATT_TK_WIDE, None)
        return carry

    n_wide = lax.div(qi * ATT_TK, ATT_TK_WIDE)
    lax.fori_loop(0, n_wide, body, 0)

    @pl.when(n_wide * ATT_TK_WIDE < qi * ATT_TK)
    def _():
        real_chunk(n_wide * ATT_TK_WIDE, ATT_TK, None)

    real_chunk(qi * ATT_TK, ATT_TK, causal)

    for h in range(N_HEADS):
        a0, a1 = acc_ref[2 * h], acc_ref[2 * h + 1]
        o0 = a0[:V_HEAD_DIM] / a0[V_HEAD_DIM:V_HEAD_DIM + 1]
        o1 = a1[:V_HEAD_DIM] / a1[V_HEAD_DIM:V_HEAD_DIM + 1]
        o = o0 - lam * o1
        o = o * lax.rsqrt(jnp.mean(o * o, axis=0, keepdims=True) + EPS)
        o = o * subcol_ref[...] * (1.0 - LAMBDA_INIT)
        o_ref[0, :, h * 128:(h + 1) * 128] = o.T.astype(BF16)


def _attn_call(q, kb, vb, mkb, mvb, lq1, lk1, lq2, lk2, subln):
    B, S, _ = q.shape
    per_b = pl.BlockSpec((1, S, 512), lambda b, i: (b, 0, 0))
    tile = pl.BlockSpec((1, ATT_TQ, 512), lambda b, i: (b, i, 0))
    small = _const_spec((1, HEAD_DIM))
    return pl.pallas_call(
        _attn_kernel,
        grid=(B, S // ATT_TQ),
        in_specs=[tile, per_b, per_b,
                  _const_spec((META_BLK, 512)), _const_spec((META_BLK, 512)),
                  small, small, small, small, _const_spec((V_HEAD_DIM, 1))],
        out_specs=tile,
        out_shape=jax.ShapeDtypeStruct((B, S, 512), BF16),
        scratch_shapes=[pltpu.VMEM((META_BLK + S, N_MAPS * 128), BF16),
                        pltpu.VMEM((N_HEADS * VT_ROWS, META_BLK + S), BF16),
                        pltpu.VMEM((ATT_TQ, N_MAPS * 128), BF16),
                        pltpu.VMEM((N_MAPS, 1, ATT_TQ), F32),
                        pltpu.VMEM((N_MAPS, VT_ROWS, ATT_TQ), F32)],
        name="attn",
        compiler_params=pltpu.CompilerParams(
            dimension_semantics=("arbitrary", "arbitrary"),
            vmem_limit_bytes=VMEM_LIMIT_BYTES),
    )(q, kb, vb, mkb, mvb, lq1, lk1, lq2, lk2, subln)


KEYS_PER_VBLK = 128 // N_HEADS


def _decode_consts():
    sub = lax.broadcasted_iota(jnp.int32, (N_MAPS, PAGE_SIZE), 0)
    lane = lax.broadcasted_iota(jnp.int32, (N_MAPS, PAGE_SIZE), 1)
    head = sub >> 1
    slope = jnp.where(head == 0, SLOPES[0],
            jnp.where(head == 1, SLOPES[1],
            jnp.where(head == 2, SLOPES[2], SLOPES[3])))
    return dict(lane=lane, head=head, slope=slope,
                own_head=(lane & (N_HEADS - 1)) == head, key_of_row=lane >> 2)


def _decode_init(q_row, k_new, v_new, head):
    sub_w = lax.broadcasted_iota(jnp.int32, (N_MAPS, ATTN_WIDTH), 0)
    lane_w = lax.broadcasted_iota(jnp.int32, (N_MAPS, ATTN_WIDTH), 1)
    qbd = jnp.where((lane_w >> 6) == sub_w, jnp.broadcast_to(q_row, (N_MAPS, ATTN_WIDTH)), 0.0)
    m = jnp.sum(qbd * k_new, axis=-1, keepdims=True)
    acc = jnp.zeros((N_MAPS, V_HEAD_DIM), F32)
    for h in range(N_HEADS):
        acc = jnp.where(head == h, v_new[:, h * 128:(h + 1) * 128], acc)
    return m, jnp.ones((N_MAPS, 1), F32), acc


def _decode_probs(m, l, qcol, k_pages, first_page, past, cst):
    scores = []
    for p, kp in enumerate(k_pages):
        s = jnp.sum((kp * qcol).reshape(N_MAPS, HEAD_DIM, PAGE_SIZE), axis=1)
        dist = past - ((first_page + p) * PAGE_SIZE + cst["lane"])
        scores.append(s - cst["slope"] * dist.astype(F32))
    s = jnp.concatenate(scores, axis=1)
    m_new = jnp.maximum(m, jnp.max(s, axis=-1, keepdims=True))
    alpha = jnp.exp(m - m_new)
    pr = jnp.exp(s - m_new)
    l = alpha * l + jnp.sum(pr, axis=-1, keepdims=True)
    prb = pr.astype(BF16).astype(F32)
    spread_probs = []
    for p in range(len(k_pages)):
        prp = prb[:, p * PAGE_SIZE:(p + 1) * PAGE_SIZE]
        spread = [jnp.where(cst["own_head"],
                            jnp.take_along_axis(prp, cst["key_of_row"] + c * KEYS_PER_VBLK, axis=1),
                            0.0)
                  for c in range(N_HEADS)]
        spread_probs.append(jnp.concatenate(spread, axis=1).astype(BF16))
    return m_new, l, alpha, spread_probs


def _decode_values(acc, alpha, spread_probs, v_pages):
    pv = jnp.zeros((N_MAPS, V_HEAD_DIM), F32)
    for pe, vp in zip(spread_probs, v_pages):
        pv = pv + jnp.dot(pe, vp.astype(BF16), preferred_element_type=F32)
    return alpha * acc + pv


def _decode_out(l, acc, lam, subln, o_ref):
    o = acc / l
    for h in range(N_HEADS):
        ho = _head_out(o[2 * h:2 * h + 1], o[2 * h + 1:2 * h + 2], lam, subln)
        o_ref[0, :, h * 128:(h + 1) * 128] = ho.astype(BF16)


FFN_CHUNK = 256


N_FFN_CHUNKS = FFN_HIDDEN // FFN_CHUNK


def _finish_rows(a_ref, c_ref, x_ref, wo_ref, gpost_ref, gpre_ref, gfpost_ref,
                 wg_ref, wu_ref, wd_ref, o_ref, before_chunk=None, after_chunk=None):
    mix = (jnp.dot(a_ref[...], wo_ref[0:ATTN_WIDTH, :], preferred_element_type=F32)
           + jnp.dot(c_ref[...], wo_ref[ATTN_WIDTH:, :], preferred_element_type=F32))
    x = x_ref[...] + _rms(mix, gpost_ref[...])
    hb = _rms(x, gpre_ref[...]).astype(BF16)
    f = jnp.zeros(x.shape, F32)
    for c in range(N_FFN_CHUNKS):
        cols = slice(c * FFN_CHUNK, (c + 1) * FFN_CHUNK)
        if before_chunk is not None:
            before_chunk(c)
        gate = jnp.dot(hb, wg_ref[:, cols], preferred_element_type=F32)
        up = jnp.dot(hb, wu_ref[:, cols], preferred_element_type=F32)
        act = (gate * jax.nn.sigmoid(gate) * up).astype(BF16)
        f = f + jnp.dot(act, wd_ref[cols, :], preferred_element_type=F32)
        if after_chunk is not None:
            after_chunk(c)
    o_ref[...] = x + _rms(f, gfpost_ref[...])


def _finish_kernel(*refs):
    _finish_rows(*refs)


DEC_GROUP = 8
DEC_AHEAD = 2
DEC_SLOTS = DEC_AHEAD + 1


def _finish_decode_kernel(pt_ref, a_ref, c_ref, x_ref, wo_ref, gpost_ref, gpre_ref, gfpost_ref,
                          wg_ref, wu_ref, wd_ref,
                          q_ref, qcol_ref, kn_ref, vn_ref,
                          lq1_ref, lk1_ref, lq2_ref, lk2_ref, sub_ref, ck_hbm, cv_hbm,
                          o_ref, os_ref, kbuf, vbuf, sem):
    i = pl.program_id(0)
    n_steps = pl.num_programs(0)
    n_pages = pt_ref.shape[1]
    n_groups = n_pages // DEC_GROUP
    past = n_pages * PAGE_SIZE

    def group_copies(b, g, slot):
        copies = []
        for p in range(DEC_GROUP):
            page = pt_ref[b, g * DEC_GROUP + p]
            copies.append(pltpu.make_async_copy(ck_hbm.at[page], kbuf.at[slot, p], sem.at[0, slot]))
            copies.append(pltpu.make_async_copy(cv_hbm.at[page], vbuf.at[slot, p], sem.at[1, slot]))
        return copies

    def slot_of(g):
        return lax.rem(i * n_groups + g, DEC_SLOTS)

    def start_group(g):
        if g < n_groups:
            for cp in group_copies(i, g, slot_of(g)):
                cp.start()
        else:
            @pl.when(i + 1 < n_steps)
            def _():
                for cp in group_copies(i + 1, g - n_groups, slot_of(g)):
                    cp.start()

    @pl.when(i == 0)
    def _():
        for g in range(DEC_AHEAD):
            start_group(g)

    cst = _decode_consts()
    qcol = qcol_ref[0]
    m, l, acc = _decode_init(q_ref[0], kn_ref[0], vn_ref[0], cst["head"])
    st = dict(m=m, l=l, acc=acc)

    def before_chunk(g):
        if g >= n_groups:
            return
        start_group(g + DEC_AHEAD)
        slot = slot_of(g)
        for cp in group_copies(i, g, slot):
            cp.wait()
        st["m"], st["l"], st["alpha"], st["probs"] = _decode_probs(
            st["m"], st["l"], qcol, [kbuf[slot, p] for p in range(DEC_GROUP)],
            g * DEC_GROUP, past, cst)

    def after_chunk(g):
        if g >= n_groups:
            return
        slot = slot_of(g)
        st["acc"] = _decode_values(st["acc"], st["alpha"], st["probs"],
                                   [vbuf[slot, p] for p in range(DEC_GROUP)])

    _finish_rows(a_ref, c_ref, x_ref, wo_ref, gpost_ref, gpre_ref, gfpost_ref,
                 wg_ref, wu_ref, wd_ref, o_ref, before_chunk, after_chunk)
    lam = _lambda(lq1_ref[...], lk1_ref[...], lq2_ref[...], lk2_ref[...])
    _decode_out(st["l"], st["acc"], lam, sub_ref[...], os_ref)


def _finish_decode_call(page_table, a, c, x, wo, gpost, gpre, gfpost, wg, wu, wd,
                        q, qcol, kn, vn, lq1, lk1, lq2, lk2, subln, ck, cv, tm):
    rows = x.shape[0]
    Bd, n_pages = page_table.shape
    assert rows // tm == Bd and n_pages % DEC_GROUP == 0
    assert DEC_AHEAD <= n_pages // DEC_GROUP <= N_FFN_CHUNKS
    half = pl.BlockSpec((tm, 512), lambda i, pt: (i, 0))
    full = pl.BlockSpec((tm, D_MODEL), lambda i, pt: (i, 0))
    vec = _const_spec((1, D_MODEL))
    row = pl.BlockSpec((1, 1, ATTN_WIDTH), lambda i, pt: (i, 0, 0))
    small = _const_spec((1, HEAD_DIM))
    hbm = pl.BlockSpec(memory_space=pl.ANY)
    grid_spec = pltpu.PrefetchScalarGridSpec(
        num_scalar_prefetch=1,
        grid=(Bd,),
        in_specs=[half, half, full, _const_spec(wo.shape), vec, vec, vec,
                  _const_spec(wg.shape), _const_spec(wu.shape), _const_spec(wd.shape),
                  row, pl.BlockSpec((1, ATTN_WIDTH, PAGE_SIZE), lambda i, pt: (i, 0, 0)),
                  row, row, small, small, small, small, _const_spec((1, V_HEAD_DIM)),
                  hbm, hbm],
        out_specs=(full, row),
        scratch_shapes=[pltpu.VMEM((DEC_SLOTS, DEC_GROUP, ATTN_WIDTH, PAGE_SIZE), F32),
                        pltpu.VMEM((DEC_SLOTS, DEC_GROUP, ATTN_WIDTH, PAGE_SIZE), F32),
                        pltpu.SemaphoreType.DMA((2, DEC_SLOTS))],
    )
    return pl.pallas_call(
        _finish_decode_kernel,
        grid_spec=grid_spec,
        out_shape=(jax.ShapeDtypeStruct((rows, D_MODEL), F32),
                   jax.ShapeDtypeStruct((Bd, 1, ATTN_WIDTH), BF16)),
        name="finish_decode",
        compiler_params=pltpu.CompilerParams(
            dimension_semantics=("arbitrary",),
            vmem_limit_bytes=VMEM_LIMIT_BYTES),
    )(page_table, a, c, x, wo, gpost, gpre, gfpost, wg, wu, wd,
      q, qcol, kn, vn, lq1, lk1, lq2, lk2, subln, ck, cv)


def _finish_call(a, c, x, wo, gpost, gpre, gfpost, wg, wu, wd, tm):
    rows = x.shape[0]
    half = pl.BlockSpec((tm, 512), lambda i: (i, 0))
    full = pl.BlockSpec((tm, D_MODEL), lambda i: (i, 0))
    vec = _const_spec((1, D_MODEL))
    return pl.pallas_call(
        _finish_kernel,
        grid=(rows // tm,),
        in_specs=[half, half, full, _const_spec(wo.shape), vec, vec, vec,
                  _const_spec(wg.shape), _const_spec(wu.shape), _const_spec(wd.shape)],
        out_specs=full,
        out_shape=jax.ShapeDtypeStruct((rows, D_MODEL), F32),
        name="finish",
        compiler_params=pltpu.CompilerParams(
            dimension_semantics=("arbitrary",),
            vmem_limit_bytes=VMEM_LIMIT_BYTES),
    )(a, c, x, wo, gpost, gpre, gfpost, wg, wu, wd)


def kernel(x_prompt, x_sample, cache_k, cache_v, state_conv, page_table, meta_tokens,
           ln_mix_pre, ln_mix_post, w_in, lambda_q1, lambda_k1, lambda_q2, lambda_k2,
           subln_w, conv_w, conv_b, conv_ln_w, conv_ln_b, w_out, ln_ffn_pre, ln_ffn_post,
           w_gate, w_up, w_down):
    B, S, _ = x_prompt.shape
    Bd = x_sample.shape[0]
    T = S + N_META
    li = 0
    w_in_bf = w_in[li].astype(BF16)
    wo_bf = w_out[li].astype(BF16)
    wg_bf = w_gate[li].astype(BF16)
    wu_bf = w_up[li].astype(BF16)
    wd_bf = w_down[li].astype(BF16)
    g_pre = ln_mix_pre[li][None]
    lam_args = (lambda_q1[li][None], lambda_k1[li][None], lambda_q2[li][None],
                lambda_k2[li][None], subln_w[li][None])
    conv_args = (conv_w[li], conv_b[li][None], conv_ln_w[li][None], conv_ln_b[li][None])
    fin_args = (wo_bf, ln_mix_post[li][None], ln_ffn_pre[li][None], ln_ffn_post[li][None],
                wg_bf, wu_bf, wd_bf)

    _, mk, mv, mu = _proj_call(meta_tokens, g_pre, w_in_bf)
    cw_tiles = jnp.broadcast_to(conv_w[li][:, None, :], (CONV_WIDTH, 8, CONV_DIM))
    q_bf, k_bf, v_bf, new_k, new_v, conv_o, new_conv = _prompt_proj_call(
        x_prompt, g_pre, w_in_bf, mk, mv, mu, cw_tiles, *conv_args[1:])
    pad = ((0, META_BLK - N_META), (0, 0))
    attn_o = _attn_call(q_bf, k_bf, v_bf, jnp.pad(mk.astype(BF16), pad), jnp.pad(mv.astype(BF16), pad),
                        *lam_args[:4], subln_w[li][:, None])

    n_pool = cache_k.shape[1]
    qs, ks, vs, us = _proj_call(x_sample.reshape(Bd, D_MODEL), g_pre, w_in_bf)
    ck = jnp.transpose(cache_k[li], (0, 2, 3, 1)).reshape(n_pool, ATTN_WIDTH, PAGE_SIZE)
    cv = cache_v[li].reshape(n_pool, PAGE_SIZE * N_HEADS, V_HEAD_DIM)
    qcol = jnp.broadcast_to(qs[:, :, None], (Bd, ATTN_WIDTH, PAGE_SIZE))
    y_prompt, attn_s = _finish_decode_call(
        page_table, attn_o.reshape(B * S, 512), conv_o.reshape(B * S, 512),
        x_prompt.reshape(B * S, D_MODEL), *fin_args,
        qs[:, None], qcol, ks[:, None], vs[:, None], *lam_args, ck, cv, tm=B * S // Bd)
    conv_s = _sample_conv_call(state_conv[li], us, *conv_args)
    y_sample = _finish_call(attn_s.reshape(Bd, 512), conv_s, x_sample.reshape(Bd, D_MODEL),
                            *fin_args, tm=Bd)

    return (y_prompt.reshape(B, S, D_MODEL),
            y_sample.reshape(Bd, 1, D_MODEL),
            new_k.reshape(1, B, T, N_MAPS, HEAD_DIM),
            new_v.reshape(1, B, T, N_HEADS, V_HEAD_DIM),
            new_conv[None],
            ks.reshape(1, Bd, 1, N_MAPS, HEAD_DIM),
            vs.reshape(1, Bd, 1, N_HEADS, V_HEAD_DIM),
            jnp.concatenate([state_conv[li][:, 1:], us[:, None]], axis=1)[None])
```

```python
import functools
import math

import jax
import jax.numpy as jnp
from jax import lax
from jax.experimental import pallas as pl
from jax.experimental.pallas import tpu as pltpu

D_MODEL = 1024
N_META = 16
ATTN_WIDTH = 512
CONV_DIM = 512
N_HEADS = 4
HEAD_DIM = 64
V_HEAD_DIM = 128
N_MAPS = 2 * N_HEADS
CONV_WIDTH = 31
FFN_HIDDEN = 2816
PAGE_SIZE = 128
EPS = 1e-6
NEG_INF = -1e30
LAMBDA_INIT = 0.8 - 0.6 * math.exp(-0.3 * 0)
SCALE = HEAD_DIM ** -0.5
SLOPES = tuple(2.0 ** (-8.0 * (h + 1) / N_HEADS) for h in range(N_HEADS))

VMEM_LIMIT_BYTES = 56 * 1024 * 1024

F32 = jnp.float32
BF16 = jnp.bfloat16


def _rms(x, g):
    return x * lax.rsqrt(jnp.mean(x * x, axis=-1, keepdims=True) + EPS) * g


def _const_spec(shape):
    nd = len(shape)
    return pl.BlockSpec(shape, lambda *_: (0,) * nd, pipeline_mode=pl.Buffered(1))


def _project(x, g, w_ref):
    xn = _rms(x, g).astype(BF16)
    seg = lambda c: jnp.dot(xn, w_ref[:, c * 512:(c + 1) * 512], preferred_element_type=F32)
    q = seg(0) * SCALE
    k = seg(1)
    v = seg(2)
    u = seg(3) * jax.nn.sigmoid(seg(4))
    return q, k, v, u


def _proj_kernel(x_ref, g_ref, w_ref, q_ref, k_ref, v_ref, u_ref):
    q, k, v, u = _project(x_ref[...], g_ref[...], w_ref)
    q_ref[...] = q
    k_ref[...] = k
    v_ref[...] = v
    u_ref[...] = u


def _proj_call(x, g, w_bf):
    rows = x.shape[0]
    out = jax.ShapeDtypeStruct((rows, 512), F32)
    return pl.pallas_call(
        _proj_kernel,
        out_shape=(out, out, out, out),
        name="proj_small",
        compiler_params=pltpu.CompilerParams(vmem_limit_bytes=VMEM_LIMIT_BYTES),
    )(x, g, w_bf)


PROJ_TM = 512
U_PAD = 48
U_ROWS_OFF = U_PAD - (CONV_WIDTH - 1)


CONV_SUB = 32
PROJ_RB = 256


def _ln_swish(y, w, b):
    mu = jnp.mean(y, axis=-1, keepdims=True)
    yc = y - mu
    z = yc * lax.rsqrt(jnp.mean(yc * yc, axis=-1, keepdims=True) + EPS) * w + b
    return z * jax.nn.sigmoid(z)


def _zero_after(x):
    bits = pltpu.bitcast(x[0:8, 0:128], jnp.uint32)
    z = ((bits >> 16) >> 16).astype(F32)[0:1, :]
    return jnp.concatenate([z] * (CONV_DIM // 128), axis=1)


def _conv_rows(u_ref, win_start, cw_ref, bias):
    lead = U_ROWS_OFF % 8
    win = u_ref[pl.ds(win_start, CONV_SUB + 32), :]
    acc = jnp.broadcast_to(bias, (CONV_SUB, CONV_DIM))
    for res in range(8):
        rows = CONV_SUB + (8 if res else 0)
        part = None
        for w in range(CONV_WIDTH):
            if (lead + w) % 8 == res:
                off = lead + w - res
                term = win[off:off + rows, :].reshape(rows // 8, 8, CONV_DIM) * cw_ref[w]
                part = term if part is None else part + term
        acc = acc + part.reshape(rows, CONV_DIM)[res:res + CONV_SUB, :]
    return acc


def _prompt_proj_kernel(x_ref, g_ref, w_ref, mk_ref, mv_ref, mu_ref,
                        cw_ref, cb_ref, lw_ref, lb_ref,
                        q_ref, kb_ref, vb_ref, nk_ref, nv_ref, co_ref, nc_ref,
                        u_ref, y_ref):
    i = pl.program_id(1)
    nt = pl.num_programs(1)

    @pl.when(i == 0)
    def _():
        nk_ref[0, 0:N_META, :] = mk_ref[...]
        nv_ref[0, 0:N_META, :] = mv_ref[...]
        u_ref[0:U_PAD - N_META, :] = jnp.zeros((U_PAD - N_META, CONV_DIM), F32)
        u_ref[U_PAD - N_META:U_PAD, :] = mu_ref[...]

    xn = _rms(x_ref[0], g_ref[...]).astype(BF16)
    seg = lambda c: jnp.dot(xn, w_ref[:, c * 512:(c + 1) * 512], preferred_element_type=F32)
    r = i * PROJ_TM
    u_ref[pl.ds(pl.multiple_of(r + U_PAD, 8), PROJ_TM), :] = seg(3) * jax.nn.sigmoid(seg(4))

    win0 = r + U_ROWS_OFF - U_ROWS_OFF % 8
    n_sub = PROJ_TM // CONV_SUB

    def conv_passes(lo, hi, bias):
        for s in range(lo, hi):
            y_ref[s * CONV_SUB:(s + 1) * CONV_SUB, :] = _conv_rows(
                u_ref, pl.multiple_of(win0 + s * CONV_SUB, 8), cw_ref, bias)

    n_blk = PROJ_TM // PROJ_RB
    for rb in range(n_blk):
        rows = slice(rb * PROJ_RB, (rb + 1) * PROJ_RB)
        out_rows = pl.ds(pl.multiple_of(r + N_META + rb * PROJ_RB, 8), PROJ_RB)
        blk = lambda c: jnp.dot(xn[rows], w_ref[:, c * 512:(c + 1) * 512],
                                preferred_element_type=F32)
        q_ref[0, rows, :] = (blk(0) * SCALE).astype(BF16)
        k = blk(1)
        kb_ref[0, rows, :] = k.astype(BF16)
        nk_ref[0, out_rows, :] = k
        v = blk(2)
        vb_ref[0, rows, :] = v.astype(BF16)
        nv_ref[0, out_rows, :] = v
        conv_passes(rb * n_sub // n_blk, (rb + 1) * n_sub // n_blk, cb_ref[...] + _zero_after(v))
    co_ref[0] = _ln_swish(y_ref[...], lw_ref[...], lb_ref[...]).astype(BF16)

    @pl.when(i == nt - 1)
    def _():
        last = nt * PROJ_TM + U_PAD
        nc_ref[0] = u_ref[last - (CONV_WIDTH - 1):last, :]


def _prompt_proj_call(x, g, w_bf, mk, mv, mu, cw_tiles, cb, lw, lb):
    B, S, _ = x.shape
    T = S + N_META
    nt = S // PROJ_TM
    tile = pl.BlockSpec((1, PROJ_TM, 512), lambda b, i: (b, i, 0))
    full = lambda rows: pl.BlockSpec((1, rows, 512), lambda b, i: (b, 0, 0))
    bf = jax.ShapeDtypeStruct((B, S, 512), BF16)
    vec = _const_spec((1, CONV_DIM))
    return pl.pallas_call(
        _prompt_proj_kernel,
        grid=(B, nt),
        in_specs=[
            pl.BlockSpec((1, PROJ_TM, D_MODEL), lambda b, i: (b, i, 0)),
            _const_spec((1, D_MODEL)),
            _const_spec(w_bf.shape),
            _const_spec((N_META, 512)),
            _const_spec((N_META, 512)),
            _const_spec((N_META, 512)),
            _const_spec((CONV_WIDTH, 8, CONV_DIM)), vec, vec, vec,
        ],
        out_specs=(tile, tile, tile, full(T), full(T), tile, full(CONV_WIDTH - 1)),
        out_shape=(bf, bf, bf,
                   jax.ShapeDtypeStruct((B, T, 512), F32),
                   jax.ShapeDtypeStruct((B, T, 512), F32),
                   bf,
                   jax.ShapeDtypeStruct((B, CONV_WIDTH - 1, CONV_DIM), F32)),
        scratch_shapes=[pltpu.VMEM((S + U_PAD, CONV_DIM), F32),
                        pltpu.VMEM((PROJ_TM, CONV_DIM), F32)],
        name="prompt_proj",
        compiler_params=pltpu.CompilerParams(
            dimension_semantics=("arbitrary", "arbitrary"),
            vmem_limit_bytes=VMEM_LIMIT_BYTES),
    )(x, g, w_bf, mk, mv, mu, cw_tiles, cb, lw, lb)


def _sample_conv_kernel(st_ref, u_ref, cw_ref, cb_ref, lw_ref, lb_ref, o_ref, y_ref):
    past_w = cw_ref[0:CONV_WIDTH - 1, :]
    for b in range(st_ref.shape[0]):
        y_ref[b:b + 1, :] = jnp.sum(st_ref[b] * past_w, axis=0, keepdims=True)
    acc = y_ref[...] + cb_ref[...] + u_ref[...] * cw_ref[CONV_WIDTH - 1:CONV_WIDTH, :]
    o_ref[...] = _ln_swish(acc, lw_ref[...], lb_ref[...]).astype(BF16)


def _sample_conv_call(state, u, cw, cb, lw, lb):
    return pl.pallas_call(
        _sample_conv_kernel,
        out_shape=jax.ShapeDtypeStruct(u.shape, BF16),
        scratch_shapes=[pltpu.VMEM(u.shape, F32)],
        name="sample_conv",
    )(state, u, cw, cb, lw, lb)


def _lambda(lq1, lk1, lq2, lk2):
    s1 = jnp.sum(lq1 * lk1, axis=-1, keepdims=True)
    s2 = jnp.sum(lq2 * lk2, axis=-1, keepdims=True)
    return jnp.exp(s1) - jnp.exp(s2) + LAMBDA_INIT


def _head_out(o0, o1, lam, subln):
    o = o0 - lam * o1
    return _rms(o, subln) * (1.0 - LAMBDA_INIT)


ATT_TQ = 256
ATT_TK = 256
ATT_TK_WIDE = 512
META_BLK = 128
QK_AHEAD = 4
VT_ROWS = V_HEAD_DIM + 16


def _augment(x, pos, is_query, slope):
    lane = lax.broadcasted_iota(jnp.int32, x.shape, 1)
    hi = (pos >> 6).astype(F32)
    lo = (pos & 63).astype(F32)
    out = []
    for j in range(2):
        a = lane - (64 if j == 0 else 0)
        if is_query:
            aug = jnp.where(a == 0, hi * (-64.0 * slope),
                  jnp.where(a == 1, lo * (-slope),
                  jnp.where(a == 2, 64.0 * slope,
                  jnp.where(a == 3, slope, 0.0))))
        else:
            aug = jnp.where(a == 0, 1.0,
                  jnp.where(a == 1, 1.0,
                  jnp.where(a == 2, hi,
                  jnp.where(a == 3, lo, 0.0))))
        own = (lane < 64) if j == 0 else (lane >= 64)
        out.append(jnp.where(own, x, aug).astype(BF16))
    return out


def _attn_kernel(q_ref, kb_ref, vb_ref, mkb_ref, mvb_ref,
                 lq1_ref, lk1_ref, lq2_ref, lk2_ref, subcol_ref,
                 o_ref, kexp_ref, vt_ref, qexp_ref, m_ref, acc_ref):
    qi = pl.program_id(1)
    S = kb_ref.shape[1]

    def chunk_step(parts, first):
        def scores(mm):
            mcols = slice(mm * 128, (mm + 1) * 128)
            out = []
            for k_rows, _, mask in parts:
                s = lax.dot_general(kexp_ref[k_rows, mcols], qexp_ref[:, mcols],
                                    (((1,), (1,)), ((), ())), preferred_element_type=F32)
                out.append(s if mask is None else jnp.where(mask, s, NEG_INF))
            return out

        pending = [scores(mm) for mm in range(QK_AHEAD)]
        for mm in range(N_MAPS):
            h = mm // 2
            if mm + QK_AHEAD < N_MAPS:
                pending.append(scores(mm + QK_AHEAD))
            m_new = None if first else m_ref[mm]
            for s in pending[mm]:
                m_cur = jnp.max(s, axis=0, keepdims=True)
                m_new = m_cur if m_new is None else jnp.maximum(m_new, m_cur)
            pv = None
            for s, (_, v_cols, _) in zip(pending[mm], parts):
                vt = vt_ref[h * VT_ROWS:(h + 1) * VT_ROWS, v_cols]
                pb = jnp.exp(s - m_new).astype(BF16)
                if pb.shape[0] < vt.shape[1]:
                    pb = jnp.concatenate(
                        [pb, jnp.zeros((vt.shape[1] - pb.shape[0], pb.shape[1]), BF16)], axis=0)
                term = jnp.dot(vt, pb, preferred_element_type=F32)
                pv = term if pv is None else pv + term
            if first:
                acc_ref[mm] = pv
            else:
                acc_ref[mm] = jnp.exp(m_ref[mm] - m_new) * acc_ref[mm] + pv
            m_ref[mm] = m_new

    @pl.when(qi == 0)
    def _():
        ones_rows = (lax.broadcasted_iota(jnp.int32, (VT_ROWS - V_HEAD_DIM, ATT_TK), 0) == 0)

        def put_vt(cols_off, vc):
            n = vc.shape[0]
            for h in range(N_HEADS):
                vt_ref[h * VT_ROWS:h * VT_ROWS + V_HEAD_DIM, cols_off:cols_off + n] = (
                    vc[:, h * 128:(h + 1) * 128].astype(F32).T.astype(BF16))
                vt_ref[h * VT_ROWS + V_HEAD_DIM:(h + 1) * VT_ROWS, cols_off:cols_off + n] = (
                    ones_rows[:, :n].astype(F32).astype(BF16))

        put_vt(0, mvb_ref[...])
        for c in range(S // ATT_TK):
            put_vt(META_BLK + c * ATT_TK, vb_ref[0, c * ATT_TK:(c + 1) * ATT_TK, :])

        def put(rows_off, kc, pos0):
            rows = kc.shape[0]
            pos = pos0 + lax.broadcasted_iota(jnp.int32, (rows, 128), 0)
            for h in range(N_HEADS):
                e0, e1 = _augment(kc[:, h * 128:(h + 1) * 128].astype(F32), pos, False, 0.0)
                kexp_ref[rows_off:rows_off + rows, (2 * h) * 128:(2 * h + 1) * 128] = e0
                kexp_ref[rows_off:rows_off + rows, (2 * h + 1) * 128:(2 * h + 2) * 128] = e1

        put(0, mkb_ref[...], 0)
        for c in range(S // ATT_TK):
            put(META_BLK + c * ATT_TK, kb_ref[0, c * ATT_TK:(c + 1) * ATT_TK, :],
                N_META + c * ATT_TK)

    qpos = N_META + qi * ATT_TQ + lax.broadcasted_iota(jnp.int32, (ATT_TQ, 128), 0)
    q = q_ref[0].astype(F32)
    for h in range(N_HEADS):
        e0, e1 = _augment(q[:, h * 128:(h + 1) * 128], qpos, True, SLOPES[h])
        qexp_ref[:, (2 * h) * 128:(2 * h + 1) * 128] = e0
        qexp_ref[:, (2 * h + 1) * 128:(2 * h + 2) * 128] = e1

    lam = _lambda(lq1_ref[...], lk1_ref[...], lq2_ref[...], lk2_ref[...])
    key = lax.broadcasted_iota(jnp.int32, (ATT_TK, ATT_TQ), 0)
    qry = lax.broadcasted_iota(jnp.int32, (ATT_TK, ATT_TQ), 1)
    causal = key <= qry

    def real_part(first_key, width, mask):
        r = pl.multiple_of(META_BLK + first_key, 128)
        return (pl.ds(r, width), pl.ds(r, width), mask)

    chunk_step([(slice(0, N_META), slice(0, META_BLK), None),
                real_part(qi * ATT_TK, ATT_TK, causal)], True)

    def body(c, carry):
        chunk_step([real_part(c * ATT_TK_WIDE, ATT_TK_WIDE, None)], False)
        return carry

    n_wide = lax.div(qi * ATT_TK, ATT_TK_WIDE)
    lax.fori_loop(0, n_wide, body, 0)

    @pl.when(n_wide * ATT_TK_WIDE < qi * ATT_TK)
    def _():
        chunk_step([real_part(n_wide * ATT_TK_WIDE, ATT_TK, None)], False)

    for h in range(N_HEADS):
        a0, a1 = acc_ref[2 * h], acc_ref[2 * h + 1]
        o0 = a0[:V_HEAD_DIM] / a0[V_HEAD_DIM:V_HEAD_DIM + 1]
        o1 = a1[:V_HEAD_DIM] / a1[V_HEAD_DIM:V_HEAD_DIM + 1]
        o = o0 - lam * o1
        o = o * lax.rsqrt(jnp.mean(o * o, axis=0, keepdims=True) + EPS)
        o = o * subcol_ref[...] * (1.0 - LAMBDA_INIT)
        o_ref[0, :, h * 128:(h + 1) * 128] = o.T.astype(BF16)


def _attn_call(q, kb, vb, mkb, mvb, lq1, lk1, lq2, lk2, subln):
    B, S, _ = q.shape
    per_b = pl.BlockSpec((1, S, 512), lambda b, i: (b, 0, 0))
    tile = pl.BlockSpec((1, ATT_TQ, 512), lambda b, i: (b, i, 0))
    small = _const_spec((1, HEAD_DIM))
    return pl.pallas_call(
        _attn_kernel,
        grid=(B, S // ATT_TQ),
        in_specs=[tile, per_b, per_b,
                  _const_spec((META_BLK, 512)), _const_spec((META_BLK, 512)),
                  small, small, small, small, _const_spec((V_HEAD_DIM, 1))],
        out_specs=tile,
        out_shape=jax.ShapeDtypeStruct((B, S, 512), BF16),
        scratch_shapes=[pltpu.VMEM((META_BLK + S, N_MAPS * 128), BF16),
                        pltpu.VMEM((N_HEADS * VT_ROWS, META_BLK + S), BF16),
                        pltpu.VMEM((ATT_TQ, N_MAPS * 128), BF16),
                        pltpu.VMEM((N_MAPS, 1, ATT_TQ), F32),
                        pltpu.VMEM((N_MAPS, VT_ROWS, ATT_TQ), F32)],
        name="attn",
        compiler_params=pltpu.CompilerParams(
            dimension_semantics=("arbitrary", "arbitrary"),
            vmem_limit_bytes=VMEM_LIMIT_BYTES),
    )(q, kb, vb, mkb, mvb, lq1, lk1, lq2, lk2, subln)


KEYS_PER_VBLK = 128 // N_HEADS


def _decode_consts():
    sub = lax.broadcasted_iota(jnp.int32, (N_MAPS, PAGE_SIZE), 0)
    lane = lax.broadcasted_iota(jnp.int32, (N_MAPS, PAGE_SIZE), 1)
    head = sub >> 1
    slope = jnp.where(head == 0, SLOPES[0],
            jnp.where(head == 1, SLOPES[1],
            jnp.where(head == 2, SLOPES[2], SLOPES[3])))
    return dict(lane=lane, head=head, slope=slope,
                own_head=(lane & (N_HEADS - 1)) == head, key_of_row=lane >> 2)


def _decode_init(q_row, k_new, v_new, head):
    sub_w = lax.broadcasted_iota(jnp.int32, (N_MAPS, ATTN_WIDTH), 0)
    lane_w = lax.broadcasted_iota(jnp.int32, (N_MAPS, ATTN_WIDTH), 1)
    qbd = jnp.where((lane_w >> 6) == sub_w, jnp.broadcast_to(q_row, (N_MAPS, ATTN_WIDTH)), 0.0)
    m = jnp.sum(qbd * k_new, axis=-1, keepdims=True)
    acc = jnp.zeros((N_MAPS, V_HEAD_DIM), F32)
    for h in range(N_HEADS):
        acc = jnp.where(head == h, v_new[:, h * 128:(h + 1) * 128], acc)
    return m, jnp.ones((N_MAPS, 1), F32), acc


def _decode_probs(m, l, qcol, k_pages, first_page, past, cst):
    partial = [[None] * N_MAPS for _ in k_pages]
    for mm in range(N_MAPS):
        for blk in range(HEAD_DIM // 8):
            rows = slice(mm * HEAD_DIM + blk * 8, mm * HEAD_DIM + blk * 8 + 8)
            qv = qcol[rows, :]
            for p, kp in enumerate(k_pages):
                t = kp[rows, :] * qv
                partial[p][mm] = t if partial[p][mm] is None else partial[p][mm] + t
    scores = []
    for p in range(len(k_pages)):
        s = jnp.sum(jnp.stack(partial[p]), axis=1)
        dist = past - ((first_page + p) * PAGE_SIZE + cst["lane"])
        scores.append(s - cst["slope"] * dist.astype(F32))
    s = jnp.concatenate(scores, axis=1)
    m_new = jnp.maximum(m, jnp.max(s, axis=-1, keepdims=True))
    alpha = jnp.exp(m - m_new)
    pr = jnp.exp(s - m_new)
    l = alpha * l + jnp.sum(pr, axis=-1, keepdims=True)
    prb = pr.astype(BF16).astype(F32)
    spread_probs = []
    for p in range(len(k_pages)):
        prp = prb[:, p * PAGE_SIZE:(p + 1) * PAGE_SIZE]
        spread = [jnp.where(cst["own_head"],
                            jnp.take_along_axis(prp, cst["key_of_row"] + c * KEYS_PER_VBLK, axis=1),
                            0.0)
                  for c in range(N_HEADS)]
        spread_probs.append(jnp.concatenate(spread, axis=1).astype(BF16))
    return m_new, l, alpha, spread_probs


def _decode_values(acc, alpha, spread_probs, v_pages):
    pv = jnp.zeros((N_MAPS, V_HEAD_DIM), F32)
    for pe, vp in zip(spread_probs, v_pages):
        pv = pv + jnp.dot(pe, vp.astype(BF16), preferred_element_type=F32)
    return alpha * acc + pv


def _decode_out(l, acc, lam, subln, o_ref):
    o = acc / l
    for h in range(N_HEADS):
        ho = _head_out(o[2 * h:2 * h + 1], o[2 * h + 1:2 * h + 2], lam, subln)
        o_ref[0, :, h * 128:(h + 1) * 128] = ho.astype(BF16)


FFN_CHUNK = 256


N_FFN_CHUNKS = FFN_HIDDEN // FFN_CHUNK


def _finish_rows(a_ref, c_ref, x_ref, wo_ref, gpost_ref, gpre_ref, gfpost_ref,
                 wg_ref, wu_ref, wd_ref, o_ref, before_chunk=None, after_chunk=None):
    mix = (jnp.dot(a_ref[...], wo_ref[0:ATTN_WIDTH, :], preferred_element_type=F32)
           + jnp.dot(c_ref[...], wo_ref[ATTN_WIDTH:, :], preferred_element_type=F32))
    x = x_ref[...] + _rms(mix, gpost_ref[...])
    hb = _rms(x, gpre_ref[...]).astype(BF16)
    f = jnp.zeros(x.shape, F32)
    for c in range(N_FFN_CHUNKS):
        cols = slice(c * FFN_CHUNK, (c + 1) * FFN_CHUNK)
        if before_chunk is not None:
            before_chunk(c)
        gate = jnp.dot(hb, wg_ref[:, cols], preferred_element_type=F32)
        up = jnp.dot(hb, wu_ref[:, cols], preferred_element_type=F32)
        act = (gate * jax.nn.sigmoid(gate) * up).astype(BF16)
        f = f + jnp.dot(act, wd_ref[cols, :], preferred_element_type=F32)
        if after_chunk is not None:
            after_chunk(c)
    o_ref[...] = x + _rms(f, gfpost_ref[...])


def _finish_kernel(*refs):
    _finish_rows(*refs)


DEC_GROUP = 8
DEC_AHEAD = 2
DEC_SLOTS = DEC_AHEAD + 1


def _finish_decode_kernel(pt_ref, a_ref, c_ref, x_ref, wo_ref, gpost_ref, gpre_ref, gfpost_ref,
                          wg_ref, wu_ref, wd_ref,
                          q_ref, qcol_ref, kn_ref, vn_ref,
                          lq1_ref, lk1_ref, lq2_ref, lk2_ref, sub_ref, ck_hbm, cv_hbm,
                          o_ref, os_ref, kbuf, vbuf, sem):
    i = pl.program_id(0)
    n_steps = pl.num_programs(0)
    n_pages = pt_ref.shape[1]
    n_groups = n_pages // DEC_GROUP
    past = n_pages * PAGE_SIZE

    def group_copies(b, g, slot):
        copies = []
        for p in range(DEC_GROUP):
            page = pt_ref[b, g * DEC_GROUP + p]
            copies.append(pltpu.make_async_copy(ck_hbm.at[page], kbuf.at[slot, p], sem.at[0, slot]))
            copies.append(pltpu.make_async_copy(cv_hbm.at[page], vbuf.at[slot, p], sem.at[1, slot]))
        return copies

    def wait_group(slot):
        pltpu.make_async_copy(ck_hbm.at[pl.ds(0, DEC_GROUP)], kbuf.at[slot], sem.at[0, slot]).wait()
        pltpu.make_async_copy(cv_hbm.at[pl.ds(0, DEC_GROUP)], vbuf.at[slot], sem.at[1, slot]).wait()

    def slot_of(g):
        return lax.rem(i * n_groups + g, DEC_SLOTS)

    def start_group(g):
        if g < n_groups:
            for cp in group_copies(i, g, slot_of(g)):
                cp.start()
        else:
            @pl.when(i + 1 < n_steps)
            def _():
                for cp in group_copies(i + 1, g - n_groups, slot_of(g)):
                    cp.start()

    @pl.when(i == 0)
    def _():
        for g in range(DEC_AHEAD):
            start_group(g)

    cst = _decode_consts()
    m, l, acc = _decode_init(q_ref[0], kn_ref[0], vn_ref[0], cst["head"])
    st = dict(m=m, l=l, acc=acc)

    def before_chunk(g):
        if g >= n_groups:
            return
        start_group(g + DEC_AHEAD)
        slot = slot_of(g)
        wait_group(slot)
        st["m"], st["l"], st["alpha"], st["probs"] = _decode_probs(
            st["m"], st["l"], qcol_ref.at[0], [kbuf.at[slot, p] for p in range(DEC_GROUP)],
            g * DEC_GROUP, past, cst)

    def after_chunk(g):
        if g >= n_groups:
            return
        slot = slot_of(g)
        st["acc"] = _decode_values(st["acc"], st["alpha"], st["probs"],
                                   [vbuf[slot, p] for p in range(DEC_GROUP)])

    _finish_rows(a_ref, c_ref, x_ref, wo_ref, gpost_ref, gpre_ref, gfpost_ref,
                 wg_ref, wu_ref, wd_ref, o_ref, before_chunk, after_chunk)
    lam = _lambda(lq1_ref[...], lk1_ref[...], lq2_ref[...], lk2_ref[...])
    _decode_out(st["l"], st["acc"], lam, sub_ref[...], os_ref)


def _finish_decode_call(page_table, a, c, x, wo, gpost, gpre, gfpost, wg, wu, wd,
                        q, qcol, kn, vn, lq1, lk1, lq2, lk2, subln, ck, cv, tm):
    rows = x.shape[0]
    Bd, n_pages = page_table.shape
    assert rows // tm == Bd and n_pages % DEC_GROUP == 0
    assert DEC_AHEAD <= n_pages // DEC_GROUP <= N_FFN_CHUNKS
    half = pl.BlockSpec((tm, 512), lambda i, pt: (i, 0))
    full = pl.BlockSpec((tm, D_MODEL), lambda i, pt: (i, 0))
    vec = _const_spec((1, D_MODEL))
    row = pl.BlockSpec((1, 1, ATTN_WIDTH), lambda i, pt: (i, 0, 0))
    small = _const_spec((1, HEAD_DIM))
    hbm = pl.BlockSpec(memory_space=pl.ANY)
    grid_spec = pltpu.PrefetchScalarGridSpec(
        num_scalar_prefetch=1,
        grid=(Bd,),
        in_specs=[half, half, full, _const_spec(wo.shape), vec, vec, vec,
                  _const_spec(wg.shape), _const_spec(wu.shape), _const_spec(wd.shape),
                  row, pl.BlockSpec((1, ATTN_WIDTH, PAGE_SIZE), lambda i, pt: (i, 0, 0)),
                  row, row, small, small, small, small, _const_spec((1, V_HEAD_DIM)),
                  hbm, hbm],
        out_specs=(full, row),
        scratch_shapes=[pltpu.VMEM((DEC_SLOTS, DEC_GROUP, ATTN_WIDTH, PAGE_SIZE), F32),
                        pltpu.VMEM((DEC_SLOTS, DEC_GROUP, ATTN_WIDTH, PAGE_SIZE), F32),
                        pltpu.SemaphoreType.DMA((2, DEC_SLOTS))],
    )
    return pl.pallas_call(
        _finish_decode_kernel,
        grid_spec=grid_spec,
        out_shape=(jax.ShapeDtypeStruct((rows, D_MODEL), F32),
                   jax.ShapeDtypeStruct((Bd, 1, ATTN_WIDTH), BF16)),
        name="finish_decode",
        compiler_params=pltpu.CompilerParams(
            dimension_semantics=("arbitrary",),
            vmem_limit_bytes=VMEM_LIMIT_BYTES),
    )(page_table, a, c, x, wo, gpost, gpre, gfpost, wg, wu, wd,
      q, qcol, kn, vn, lq1, lk1, lq2, lk2, subln, ck, cv)


def _finish_call(a, c, x, wo, gpost, gpre, gfpost, wg, wu, wd, tm):
    rows = x.shape[0]
    half = pl.BlockSpec((tm, 512), lambda i: (i, 0))
    full = pl.BlockSpec((tm, D_MODEL), lambda i: (i, 0))
    vec = _const_spec((1, D_MODEL))
    return pl.pallas_call(
        _finish_kernel,
        grid=(rows // tm,),
        in_specs=[half, half, full, _const_spec(wo.shape), vec, vec, vec,
                  _const_spec(wg.shape), _const_spec(wu.shape), _const_spec(wd.shape)],
        out_specs=full,
        out_shape=jax.ShapeDtypeStruct((rows, D_MODEL), F32),
        name="finish",
        compiler_params=pltpu.CompilerParams(
            dimension_semantics=("arbitrary",),
            vmem_limit_bytes=VMEM_LIMIT_BYTES),
    )(a, c, x, wo, gpost, gpre, gfpost, wg, wu, wd)


def kernel(x_prompt, x_sample, cache_k, cache_v, state_conv, page_table, meta_tokens,
           ln_mix_pre, ln_mix_post, w_in, lambda_q1, lambda_k1, lambda_q2, lambda_k2,
           subln_w, conv_w, conv_b, conv_ln_w, conv_ln_b, w_out, ln_ffn_pre, ln_ffn_post,
           w_gate, w_up, w_down):
    B, S, _ = x_prompt.shape
    Bd = x_sample.shape[0]
    T = S + N_META
    li = 0
    w_in_bf = w_in[li].astype(BF16)
    wo_bf = w_out[li].astype(BF16)
    wg_bf = w_gate[li].astype(BF16)
    wu_bf = w_up[li].astype(BF16)
    wd_bf = w_down[li].astype(BF16)
    g_pre = ln_mix_pre[li][None]
    lam_args = (lambda_q1[li][None], lambda_k1[li][None], lambda_q2[li][None],
                lambda_k2[li][None], subln_w[li][None])
    conv_args = (conv_w[li], conv_b[li][None], conv_ln_w[li][None], conv_ln_b[li][None])
    fin_args = (wo_bf, ln_mix_post[li][None], ln_ffn_pre[li][None], ln_ffn_post[li][None],
                wg_bf, wu_bf, wd_bf)

    _, mk, mv, mu = _proj_call(meta_tokens, g_pre, w_in_bf)
    cw_tiles = jnp.broadcast_to(conv_w[li][:, None, :], (CONV_WIDTH, 8, CONV_DIM))
    q_bf, k_bf, v_bf, new_k, new_v, conv_o, new_conv = _prompt_proj_call(
        x_prompt, g_pre, w_in_bf, mk, mv, mu, cw_tiles, *conv_args[1:])
    pad = ((0, META_BLK - N_META), (0, 0))
    attn_o = _attn_call(q_bf, k_bf, v_bf, jnp.pad(mk.astype(BF16), pad), jnp.pad(mv.astype(BF16), pad),
                        *lam_args[:4], subln_w[li][:, None])

    n_pool = cache_k.shape[1]
    qs, ks, vs, us = _proj_call(x_sample.reshape(Bd, D_MODEL), g_pre, w_in_bf)
    ck = jnp.transpose(cache_k[li], (0, 2, 3, 1)).reshape(n_pool, ATTN_WIDTH, PAGE_SIZE)
    cv = cache_v[li].reshape(n_pool, PAGE_SIZE * N_HEADS, V_HEAD_DIM)
    qcol = jnp.broadcast_to(qs[:, :, None], (Bd, ATTN_WIDTH, PAGE_SIZE))
    y_prompt, attn_s = _finish_decode_call(
        page_table, attn_o.reshape(B * S, 512), conv_o.reshape(B * S, 512),
        x_prompt.reshape(B * S, D_MODEL), *fin_args,
        qs[:, None], qcol, ks[:, None], vs[:, None], *lam_args, ck, cv, tm=B * S // Bd)
    conv_s = _sample_conv_call(state_conv[li], us, *conv_args)
    y_sample = _finish_call(attn_s.reshape(Bd, 512), conv_s, x_sample.reshape(Bd, D_MODEL),
                            *fin_args, tm=Bd)

    return (y_prompt.reshape(B, S, D_MODEL),
            y_sample.reshape(Bd, 1, D_MODEL),
            new_k.reshape(1, B, T, N_MAPS, HEAD_DIM),
            new_v.reshape(1, B, T, N_HEADS, V_HEAD_DIM),
            new_conv[None],
            ks.reshape(1, Bd, 1, N_MAPS, HEAD_DIM),
            vs.reshape(1, Bd, 1, N_HEADS, V_HEAD_DIM),
            jnp.concatenate([state_conv[li][:, 1:], us[:, None]], axis=1)[None])
```

```python
import functools
import math

import jax
import jax.numpy as jnp
from jax import lax
from jax.experimental import pallas as pl
from jax.experimental.pallas import tpu as pltpu

D_MODEL = 1024
N_META = 16
ATTN_WIDTH = 512
CONV_DIM = 512
N_HEADS = 4
HEAD_DIM = 64
V_HEAD_DIM = 128
N_MAPS = 2 * N_HEADS
CONV_WIDTH = 31
FFN_HIDDEN = 2816
PAGE_SIZE = 128
EPS = 1e-6
NEG_INF = -1e30
LAMBDA_INIT = 0.8 - 0.6 * math.exp(-0.3 * 0)
SCALE = HEAD_DIM ** -0.5
SLOPES = tuple(2.0 ** (-8.0 * (h + 1) / N_HEADS) for h in range(N_HEADS))

VMEM_LIMIT_BYTES = 56 * 1024 * 1024

F32 = jnp.float32
BF16 = jnp.bfloat16


def _rms(x, g):
    return x * lax.rsqrt(jnp.mean(x * x, axis=-1, keepdims=True) + EPS) * g


def _const_spec(shape):
    nd = len(shape)
    return pl.BlockSpec(shape, lambda *_: (0,) * nd, pipeline_mode=pl.Buffered(1))


def _project(x, g, w_ref):
    xn = _rms(x, g).astype(BF16)
    seg = lambda c: jnp.dot(xn, w_ref[:, c * 512:(c + 1) * 512], preferred_element_type=F32)
    q = seg(0) * SCALE
    k = seg(1)
    v = seg(2)
    u = seg(3) * jax.nn.sigmoid(seg(4))
    return q, k, v, u


def _proj_kernel(x_ref, g_ref, w_ref, q_ref, k_ref, v_ref, u_ref):
    q, k, v, u = _project(x_ref[...], g_ref[...], w_ref)
    q_ref[...] = q
    k_ref[...] = k
    v_ref[...] = v
    u_ref[...] = u


def _proj_call(x, g, w_bf):
    rows = x.shape[0]
    out = jax.ShapeDtypeStruct((rows, 512), F32)
    return pl.pallas_call(
        _proj_kernel,
        out_shape=(out, out, out, out),
        name="proj_small",
        compiler_params=pltpu.CompilerParams(vmem_limit_bytes=VMEM_LIMIT_BYTES),
    )(x, g, w_bf)


PROJ_TM = 512
U_PAD = 48
U_ROWS_OFF = U_PAD - (CONV_WIDTH - 1)


CONV_SUB = 32
PROJ_RB = 256


def _ln_swish(y, w, b):
    mu = jnp.mean(y, axis=-1, keepdims=True)
    yc = y - mu
    z = yc * lax.rsqrt(jnp.mean(yc * yc, axis=-1, keepdims=True) + EPS) * w + b
    return z * jax.nn.sigmoid(z)


def _zero_after(x):
    bits = pltpu.bitcast(x[0:8, 0:128], jnp.uint32)
    z = ((bits >> 16) >> 16).astype(F32)[0:1, :]
    return jnp.concatenate([z] * (CONV_DIM // 128), axis=1)


def _conv_rows(u_ref, win_start, cw_ref, bias):
    lead = U_ROWS_OFF % 8
    win = u_ref[pl.ds(win_start, CONV_SUB + 32), :]
    acc = jnp.broadcast_to(bias, (CONV_SUB, CONV_DIM))
    for res in range(8):
        rows = CONV_SUB + (8 if res else 0)
        part = None
        for w in range(CONV_WIDTH):
            if (lead + w) % 8 == res:
                off = lead + w - res
                term = win[off:off + rows, :].reshape(rows // 8, 8, CONV_DIM) * cw_ref[w]
                part = term if part is None else part + term
        acc = acc + part.reshape(rows, CONV_DIM)[res:res + CONV_SUB, :]
    return acc


def _store_value_rows(nv_ref, first_token, v):
    for h in range(N_HEADS):
        nv_ref[0, pl.ds(first_token * N_HEADS + h, v.shape[0], stride=N_HEADS), :] = (
            v[:, h * V_HEAD_DIM:(h + 1) * V_HEAD_DIM])


def _prompt_proj_kernel(x_ref, g_ref, w_ref, mk_ref, mv_ref, mu_ref,
                        cw_ref, cb_ref, lw_ref, lb_ref,
                        q_ref, kb_ref, vb_ref, nk_ref, nv_ref, co_ref, nc_ref,
                        u_ref, y_ref):
    i = pl.program_id(1)
    nt = pl.num_programs(1)

    @pl.when(i == 0)
    def _():
        nk_ref[0, 0:N_META, :] = mk_ref[...]
        _store_value_rows(nv_ref, 0, mv_ref[...])
        u_ref[0:U_PAD - N_META, :] = jnp.zeros((U_PAD - N_META, CONV_DIM), F32)
        u_ref[U_PAD - N_META:U_PAD, :] = mu_ref[...]

    xn = _rms(x_ref[0], g_ref[...]).astype(BF16)
    seg = lambda c: jnp.dot(xn, w_ref[:, c * 512:(c + 1) * 512], preferred_element_type=F32)
    r = i * PROJ_TM
    u_ref[pl.ds(pl.multiple_of(r + U_PAD, 8), PROJ_TM), :] = seg(3) * jax.nn.sigmoid(seg(4))

    win0 = r + U_ROWS_OFF - U_ROWS_OFF % 8
    n_sub = PROJ_TM // CONV_SUB

    def conv_passes(lo, hi, bias):
        for s in range(lo, hi):
            y_ref[s * CONV_SUB:(s + 1) * CONV_SUB, :] = _conv_rows(
                u_ref, pl.multiple_of(win0 + s * CONV_SUB, 8), cw_ref, bias)

    n_blk = PROJ_TM // PROJ_RB
    for rb in range(n_blk):
        rows = slice(rb * PROJ_RB, (rb + 1) * PROJ_RB)
        out_rows = pl.ds(pl.multiple_of(r + N_META + rb * PROJ_RB, 8), PROJ_RB)
        blk = lambda c: jnp.dot(xn[rows], w_ref[:, c * 512:(c + 1) * 512],
                                preferred_element_type=F32)
        q_ref[0, rows, :] = (blk(0) * SCALE).astype(BF16)
        k = blk(1)
        kb_ref[0, rows, :] = k.astype(BF16)
        nk_ref[0, out_rows, :] = k
        v = blk(2)
        vb_ref[0, rows, :] = v.astype(BF16)
        _store_value_rows(nv_ref, r + N_META + rb * PROJ_RB, v)
        conv_passes(rb * n_sub // n_blk, (rb + 1) * n_sub // n_blk, cb_ref[...] + _zero_after(v))
    co_ref[0] = _ln_swish(y_ref[...], lw_ref[...], lb_ref[...]).astype(BF16)

    @pl.when(i == nt - 1)
    def _():
        last = nt * PROJ_TM + U_PAD
        nc_ref[0] = u_ref[last - (CONV_WIDTH - 1):last, :]


def _prompt_proj_call(x, g, w_bf, mk, mv, mu, cw_tiles, cb, lw, lb):
    B, S, _ = x.shape
    T = S + N_META
    nt = S // PROJ_TM
    tile = pl.BlockSpec((1, PROJ_TM, 512), lambda b, i: (b, i, 0))
    full = lambda rows: pl.BlockSpec((1, rows, 512), lambda b, i: (b, 0, 0))
    bf = jax.ShapeDtypeStruct((B, S, 512), BF16)
    vec = _const_spec((1, CONV_DIM))
    return pl.pallas_call(
        _prompt_proj_kernel,
        grid=(B, nt),
        in_specs=[
            pl.BlockSpec((1, PROJ_TM, D_MODEL), lambda b, i: (b, i, 0)),
            _const_spec((1, D_MODEL)),
            _const_spec(w_bf.shape),
            _const_spec((N_META, 512)),
            _const_spec((N_META, 512)),
            _const_spec((N_META, 512)),
            _const_spec((CONV_WIDTH, 8, CONV_DIM)), vec, vec, vec,
        ],
        out_specs=(tile, tile, tile, full(T),
                   pl.BlockSpec((1, T * N_HEADS, V_HEAD_DIM), lambda b, i: (b, 0, 0)),
                   tile, full(CONV_WIDTH - 1)),
        out_shape=(bf, bf, bf,
                   jax.ShapeDtypeStruct((B, T, 512), F32),
                   jax.ShapeDtypeStruct((B, T * N_HEADS, V_HEAD_DIM), F32),
                   bf,
                   jax.ShapeDtypeStruct((B, CONV_WIDTH - 1, CONV_DIM), F32)),
        scratch_shapes=[pltpu.VMEM((S + U_PAD, CONV_DIM), F32),
                        pltpu.VMEM((PROJ_TM, CONV_DIM), F32)],
        name="prompt_proj",
        compiler_params=pltpu.CompilerParams(
            dimension_semantics=("arbitrary", "arbitrary"),
            vmem_limit_bytes=VMEM_LIMIT_BYTES),
    )(x, g, w_bf, mk, mv, mu, cw_tiles, cb, lw, lb)


def _sample_conv_kernel(st_ref, u_ref, cw_ref, cb_ref, lw_ref, lb_ref, o_ref, ns_ref):
    u = u_ref[...]
    acc = cb_ref[...] + u * cw_ref[CONV_WIDTH - 1:CONV_WIDTH, :]
    for w in range(CONV_WIDTH - 1):
        acc = acc + st_ref[w] * cw_ref[w:w + 1, :]
    o_ref[...] = _ln_swish(acc, lw_ref[...], lb_ref[...]).astype(BF16)
    ns_ref[0:CONV_WIDTH - 2] = st_ref[1:CONV_WIDTH - 1]
    ns_ref[CONV_WIDTH - 2] = u


def _sample_conv_call(state_wbc, u, cw, cb, lw, lb):
    return pl.pallas_call(
        _sample_conv_kernel,
        out_shape=(jax.ShapeDtypeStruct(u.shape, BF16),
                   jax.ShapeDtypeStruct(state_wbc.shape, F32)),
        name="sample_conv",
    )(state_wbc, u, cw, cb, lw, lb)


def _lambda(lq1, lk1, lq2, lk2):
    s1 = jnp.sum(lq1 * lk1, axis=-1, keepdims=True)
    s2 = jnp.sum(lq2 * lk2, axis=-1, keepdims=True)
    return jnp.exp(s1) - jnp.exp(s2) + LAMBDA_INIT


def _head_out(o0, o1, lam, subln):
    o = o0 - lam * o1
    return _rms(o, subln) * (1.0 - LAMBDA_INIT)


ATT_TQ = 256
ATT_TK = 256
ATT_TK_WIDE = 512
META_BLK = 128
QK_AHEAD = 4
VT_ROWS = V_HEAD_DIM + 16


def _augment(x, pos, is_query, slope):
    lane = lax.broadcasted_iota(jnp.int32, x.shape, 1)
    hi = (pos >> 6).astype(F32)
    lo = (pos & 63).astype(F32)
    out = []
    for j in range(2):
        a = lane - (64 if j == 0 else 0)
        if is_query:
            aug = jnp.where(a == 0, hi * (-64.0 * slope),
                  jnp.where(a == 1, lo * (-slope),
                  jnp.where(a == 2, 64.0 * slope,
                  jnp.where(a == 3, slope, 0.0))))
        else:
            aug = jnp.where(a == 0, 1.0,
                  jnp.where(a == 1, 1.0,
                  jnp.where(a == 2, hi,
                  jnp.where(a == 3, lo, 0.0))))
        own = (lane < 64) if j == 0 else (lane >= 64)
        out.append(jnp.where(own, x, aug).astype(BF16))
    return out


def _attn_kernel(q_ref, kb_ref, vb_ref, mkb_ref, mvb_ref,
                 lq1_ref, lk1_ref, lq2_ref, lk2_ref, subcol_ref,
                 o_ref, kexp_ref, vt_ref, qexp_ref, m_ref, acc_ref):
    qi = pl.program_id(1)
    S = kb_ref.shape[1]

    def chunk_step(parts, first):
        def scores(mm):
            mcols = slice(mm * 128, (mm + 1) * 128)
            out = []
            for k_rows, _, mask in parts:
                s = lax.dot_general(kexp_ref[k_rows, mcols], qexp_ref[:, mcols],
                                    (((1,), (1,)), ((), ())), preferred_element_type=F32)
                out.append(s if mask is None else jnp.where(mask, s, NEG_INF))
            return out

        pending = [scores(mm) for mm in range(QK_AHEAD)]
        for mm in range(N_MAPS):
            h = mm // 2
            if mm + QK_AHEAD < N_MAPS:
                pending.append(scores(mm + QK_AHEAD))
            m_new = None if first else m_ref[mm]
            for s in pending[mm]:
                m_cur = jnp.max(s, axis=0, keepdims=True)
                m_new = m_cur if m_new is None else jnp.maximum(m_new, m_cur)
            pv = None
            for s, (_, v_cols, _) in zip(pending[mm], parts):
                vt = vt_ref[h * VT_ROWS:(h + 1) * VT_ROWS, v_cols]
                pb = jnp.exp(s - m_new).astype(BF16)
                if pb.shape[0] < vt.shape[1]:
                    pb = jnp.concatenate(
                        [pb, jnp.zeros((vt.shape[1] - pb.shape[0], pb.shape[1]), BF16)], axis=0)
                term = jnp.dot(vt, pb, preferred_element_type=F32)
                pv = term if pv is None else pv + term
            if first:
                acc_ref[mm] = pv
            else:
                acc_ref[mm] = jnp.exp(m_ref[mm] - m_new) * acc_ref[mm] + pv
            m_ref[mm] = m_new

    @pl.when(qi == 0)
    def _():
        ones_rows = (lax.broadcasted_iota(jnp.int32, (VT_ROWS - V_HEAD_DIM, ATT_TK), 0) == 0)

        def put_vt(cols_off, vc):
            n = vc.shape[0]
            for h in range(N_HEADS):
                vt_ref[h * VT_ROWS:h * VT_ROWS + V_HEAD_DIM, cols_off:cols_off + n] = (
                    vc[:, h * 128:(h + 1) * 128].astype(F32).T.astype(BF16))
                vt_ref[h * VT_ROWS + V_HEAD_DIM:(h + 1) * VT_ROWS, cols_off:cols_off + n] = (
                    ones_rows[:, :n].astype(F32).astype(BF16))

        put_vt(0, mvb_ref[...])
        for c in range(S // ATT_TK):
            put_vt(META_BLK + c * ATT_TK, vb_ref[0, c * ATT_TK:(c + 1) * ATT_TK, :])

        def put(rows_off, kc, pos0):
            rows = kc.shape[0]
            pos = pos0 + lax.broadcasted_iota(jnp.int32, (rows, 128), 0)
            for h in range(N_HEADS):
                e0, e1 = _augment(kc[:, h * 128:(h + 1) * 128].astype(F32), pos, False, 0.0)
                kexp_ref[rows_off:rows_off + rows, (2 * h) * 128:(2 * h + 1) * 128] = e0
                kexp_ref[rows_off:rows_off + rows, (2 * h + 1) * 128:(2 * h + 2) * 128] = e1

        put(0, mkb_ref[...], 0)
        for c in range(S // ATT_TK):
            put(META_BLK + c * ATT_TK, kb_ref[0, c * ATT_TK:(c + 1) * ATT_TK, :],
                N_META + c * ATT_TK)

    qpos = N_META + qi * ATT_TQ + lax.broadcasted_iota(jnp.int32, (ATT_TQ, 128), 0)
    q = q_ref[0].astype(F32)
    for h in range(N_HEADS):
        e0, e1 = _augment(q[:, h * 128:(h + 1) * 128], qpos, True, SLOPES[h])
        qexp_ref[:, (2 * h) * 128:(2 * h + 1) * 128] = e0
        qexp_ref[:, (2 * h + 1) * 128:(2 * h + 2) * 128] = e1

    lam = _lambda(lq1_ref[...], lk1_ref[...], lq2_ref[...], lk2_ref[...])
    key = lax.broadcasted_iota(jnp.int32, (ATT_TK, ATT_TQ), 0)
    qry = lax.broadcasted_iota(jnp.int32, (ATT_TK, ATT_TQ), 1)
    causal = key <= qry

    def real_part(first_key, width, mask):
        r = pl.multiple_of(META_BLK + first_key, 128)
        return (pl.ds(r, width), pl.ds(r, width), mask)

    chunk_step([(slice(0, N_META), slice(0, META_BLK), None),
                real_part(qi * ATT_TK, ATT_TK, causal)], True)

    def body(c, carry):
        chunk_step([real_part(c * ATT_TK_WIDE, ATT_TK_WIDE, None)], False)
        return carry

    n_wide = lax.div(qi * ATT_TK, ATT_TK_WIDE)
    lax.fori_loop(0, n_wide, body, 0)

    @pl.when(n_wide * ATT_TK_WIDE < qi * ATT_TK)
    def _():
        chunk_step([real_part(n_wide * ATT_TK_WIDE, ATT_TK, None)], False)

    for h in range(N_HEADS):
        a0, a1 = acc_ref[2 * h], acc_ref[2 * h + 1]
        o0 = a0[:V_HEAD_DIM] / a0[V_HEAD_DIM:V_HEAD_DIM + 1]
        o1 = a1[:V_HEAD_DIM] / a1[V_HEAD_DIM:V_HEAD_DIM + 1]
        o = o0 - lam * o1
        o = o * lax.rsqrt(jnp.mean(o * o, axis=0, keepdims=True) + EPS)
        o = o * subcol_ref[...] * (1.0 - LAMBDA_INIT)
        o_ref[0, :, h * 128:(h + 1) * 128] = o.T.astype(BF16)


def _attn_call(q, kb, vb, mkb, mvb, lq1, lk1, lq2, lk2, subln):
    B, S, _ = q.shape
    per_b = pl.BlockSpec((1, S, 512), lambda b, i: (b, 0, 0))
    tile = pl.BlockSpec((1, ATT_TQ, 512), lambda b, i: (b, i, 0))
    small = _const_spec((1, HEAD_DIM))
    return pl.pallas_call(
        _attn_kernel,
        grid=(B, S // ATT_TQ),
        in_specs=[tile, per_b, per_b,
                  _const_spec((META_BLK, 512)), _const_spec((META_BLK, 512)),
                  small, small, small, small, _const_spec((V_HEAD_DIM, 1))],
        out_specs=tile,
        out_shape=jax.ShapeDtypeStruct((B, S, 512), BF16),
        scratch_shapes=[pltpu.VMEM((META_BLK + S, N_MAPS * 128), BF16),
                        pltpu.VMEM((N_HEADS * VT_ROWS, META_BLK + S), BF16),
                        pltpu.VMEM((ATT_TQ, N_MAPS * 128), BF16),
                        pltpu.VMEM((N_MAPS, 1, ATT_TQ), F32),
                        pltpu.VMEM((N_MAPS, VT_ROWS, ATT_TQ), F32)],
        name="attn",
        compiler_params=pltpu.CompilerParams(
            dimension_semantics=("arbitrary", "arbitrary"),
            vmem_limit_bytes=VMEM_LIMIT_BYTES),
    )(q, kb, vb, mkb, mvb, lq1, lk1, lq2, lk2, subln)


KEYS_PER_VBLK = 128 // N_HEADS


def _decode_consts():
    sub = lax.broadcasted_iota(jnp.int32, (N_MAPS, PAGE_SIZE), 0)
    lane = lax.broadcasted_iota(jnp.int32, (N_MAPS, PAGE_SIZE), 1)
    head = sub >> 1
    slope = jnp.where(head == 0, SLOPES[0],
            jnp.where(head == 1, SLOPES[1],
            jnp.where(head == 2, SLOPES[2], SLOPES[3])))
    return dict(lane=lane, head=head, slope=slope,
                own_head=(lane & (N_HEADS - 1)) == head, key_of_row=lane >> 2)


def _decode_init(q_row, k_new, v_new, head):
    sub_w = lax.broadcasted_iota(jnp.int32, (N_MAPS, ATTN_WIDTH), 0)
    lane_w = lax.broadcasted_iota(jnp.int32, (N_MAPS, ATTN_WIDTH), 1)
    qbd = jnp.where((lane_w >> 6) == sub_w, jnp.broadcast_to(q_row, (N_MAPS, ATTN_WIDTH)), 0.0)
    m = jnp.sum(qbd * k_new, axis=-1, keepdims=True)
    acc = jnp.zeros((N_MAPS, V_HEAD_DIM), F32)
    for h in range(N_HEADS):
        acc = jnp.where(head == h, v_new[:, h * 128:(h + 1) * 128], acc)
    return m, jnp.ones((N_MAPS, 1), F32), acc


def _decode_probs(m, l, qcol, k_pages, first_page, past, cst):
    partial = [[None] * N_MAPS for _ in k_pages]
    for mm in range(N_MAPS):
        for blk in range(HEAD_DIM // 8):
            rows = slice(mm * HEAD_DIM + blk * 8, mm * HEAD_DIM + blk * 8 + 8)
            qv = qcol[rows, :]
            for p, kp in enumerate(k_pages):
                t = kp[rows, :] * qv
                partial[p][mm] = t if partial[p][mm] is None else partial[p][mm] + t
    scores = []
    for p in range(len(k_pages)):
        s = jnp.sum(jnp.stack(partial[p]), axis=1)
        dist = past - ((first_page + p) * PAGE_SIZE + cst["lane"])
        scores.append(s - cst["slope"] * dist.astype(F32))
    s = jnp.concatenate(scores, axis=1)
    m_new = jnp.maximum(m, jnp.max(s, axis=-1, keepdims=True))
    alpha = jnp.exp(m - m_new)
    pr = jnp.exp(s - m_new)
    l = alpha * l + jnp.sum(pr, axis=-1, keepdims=True)
    prb = pr.astype(BF16).astype(F32)
    spread_probs = []
    for p in range(len(k_pages)):
        prp = prb[:, p * PAGE_SIZE:(p + 1) * PAGE_SIZE]
        spread = [jnp.where(cst["own_head"],
                            jnp.take_along_axis(prp, cst["key_of_row"] + c * KEYS_PER_VBLK, axis=1),
                            0.0)
                  for c in range(N_HEADS)]
        spread_probs.append(jnp.concatenate(spread, axis=1).astype(BF16))
    return m_new, l, alpha, spread_probs


def _decode_values(acc, alpha, spread_probs, v_pages):
    pv = jnp.zeros((N_MAPS, V_HEAD_DIM), F32)
    for pe, vp in zip(spread_probs, v_pages):
        pv = pv + jnp.dot(pe, vp.astype(BF16), preferred_element_type=F32)
    return alpha * acc + pv


def _decode_out(l, acc, lam, subln, o_ref):
    o = acc / l
    for h in range(N_HEADS):
        ho = _head_out(o[2 * h:2 * h + 1], o[2 * h + 1:2 * h + 2], lam, subln)
        o_ref[0, :, h * 128:(h + 1) * 128] = ho.astype(BF16)


FFN_CHUNK = 256


N_FFN_CHUNKS = FFN_HIDDEN // FFN_CHUNK


def _finish_rows(a_ref, c_ref, x_ref, wo_ref, gpost_ref, gpre_ref, gfpost_ref,
                 wg_ref, wu_ref, wd_ref, o_ref, before_chunk=None, after_chunk=None):
    mix = (jnp.dot(a_ref[...], wo_ref[0:ATTN_WIDTH, :], preferred_element_type=F32)
           + jnp.dot(c_ref[...], wo_ref[ATTN_WIDTH:, :], preferred_element_type=F32))
    x = x_ref[...] + _rms(mix, gpost_ref[...])
    hb = _rms(x, gpre_ref[...]).astype(BF16)
    f = jnp.zeros(x.shape, F32)
    for c in range(N_FFN_CHUNKS):
        cols = slice(c * FFN_CHUNK, (c + 1) * FFN_CHUNK)
        if before_chunk is not None:
            before_chunk(c)
        gate = jnp.dot(hb, wg_ref[:, cols], preferred_element_type=F32)
        up = jnp.dot(hb, wu_ref[:, cols], preferred_element_type=F32)
        act = (gate * jax.nn.sigmoid(gate) * up).astype(BF16)
        f = f + jnp.dot(act, wd_ref[cols, :], preferred_element_type=F32)
        if after_chunk is not None:
            after_chunk(c)
    o_ref[...] = x + _rms(f, gfpost_ref[...])


def _finish_kernel(*refs):
    _finish_rows(*refs)


DEC_GROUP = 8
DEC_AHEAD = 2
DEC_SLOTS = DEC_AHEAD + 1


def _finish_decode_kernel(pt_ref, a_ref, c_ref, x_ref, wo_ref, gpost_ref, gpre_ref, gfpost_ref,
                          wg_ref, wu_ref, wd_ref,
                          q_ref, qcol_ref, kn_ref, vn_ref,
                          lq1_ref, lk1_ref, lq2_ref, lk2_ref, sub_ref, ck_hbm, cv_hbm,
                          o_ref, os_ref, kbuf, vbuf, sem):
    i = pl.program_id(0)
    n_steps = pl.num_programs(0)
    n_pages = pt_ref.shape[1]
    n_groups = n_pages // DEC_GROUP
    past = n_pages * PAGE_SIZE

    def group_copies(b, g, slot):
        copies = []
        for p in range(DEC_GROUP):
            page = pt_ref[b, g * DEC_GROUP + p]
            copies.append(pltpu.make_async_copy(ck_hbm.at[page], kbuf.at[slot, p], sem.at[0, slot]))
            copies.append(pltpu.make_async_copy(cv_hbm.at[page], vbuf.at[slot, p], sem.at[1, slot]))
        return copies

    def wait_group(slot):
        pltpu.make_async_copy(ck_hbm.at[pl.ds(0, DEC_GROUP)], kbuf.at[slot], sem.at[0, slot]).wait()
        pltpu.make_async_copy(cv_hbm.at[pl.ds(0, DEC_GROUP)], vbuf.at[slot], sem.at[1, slot]).wait()

    def slot_of(g):
        return lax.rem(i * n_groups + g, DEC_SLOTS)

    def start_group(g):
        if g < n_groups:
            for cp in group_copies(i, g, slot_of(g)):
                cp.start()
        else:
            @pl.when(i + 1 < n_steps)
            def _():
                for cp in group_copies(i + 1, g - n_groups, slot_of(g)):
                    cp.start()

    @pl.when(i == 0)
    def _():
        for g in range(DEC_AHEAD):
            start_group(g)

    cst = _decode_consts()
    m, l, acc = _decode_init(q_ref[0], kn_ref[0], vn_ref[0], cst["head"])
    st = dict(m=m, l=l, acc=acc)

    def before_chunk(g):
        if g >= n_groups:
            return
        start_group(g + DEC_AHEAD)
        slot = slot_of(g)
        wait_group(slot)
        st["m"], st["l"], st["alpha"], st["probs"] = _decode_probs(
            st["m"], st["l"], qcol_ref.at[0], [kbuf.at[slot, p] for p in range(DEC_GROUP)],
            g * DEC_GROUP, past, cst)

    def after_chunk(g):
        if g >= n_groups:
            return
        slot = slot_of(g)
        st["acc"] = _decode_values(st["acc"], st["alpha"], st["probs"],
                                   [vbuf[slot, p] for p in range(DEC_GROUP)])

    _finish_rows(a_ref, c_ref, x_ref, wo_ref, gpost_ref, gpre_ref, gfpost_ref,
                 wg_ref, wu_ref, wd_ref, o_ref, before_chunk, after_chunk)
    lam = _lambda(lq1_ref[...], lk1_ref[...], lq2_ref[...], lk2_ref[...])
    _decode_out(st["l"], st["acc"], lam, sub_ref[...], os_ref)


def _finish_decode_call(page_table, a, c, x, wo, gpost, gpre, gfpost, wg, wu, wd,
                        q, qcol, kn, vn, lq1, lk1, lq2, lk2, subln, ck, cv, tm):
    rows = x.shape[0]
    Bd, n_pages = page_table.shape
    assert rows // tm == Bd and n_pages % DEC_GROUP == 0
    assert DEC_AHEAD <= n_pages // DEC_GROUP <= N_FFN_CHUNKS
    half = pl.BlockSpec((tm, 512), lambda i, pt: (i, 0))
    full = pl.BlockSpec((tm, D_MODEL), lambda i, pt: (i, 0))
    vec = _const_spec((1, D_MODEL))
    row = pl.BlockSpec((1, 1, ATTN_WIDTH), lambda i, pt: (i, 0, 0))
    small = _const_spec((1, HEAD_DIM))
    hbm = pl.BlockSpec(memory_space=pl.ANY)
    grid_spec = pltpu.PrefetchScalarGridSpec(
        num_scalar_prefetch=1,
        grid=(Bd,),
        in_specs=[half, half, full, _const_spec(wo.shape), vec, vec, vec,
                  _const_spec(wg.shape), _const_spec(wu.shape), _const_spec(wd.shape),
                  row, pl.BlockSpec((1, ATTN_WIDTH, PAGE_SIZE), lambda i, pt: (i, 0, 0)),
                  row, row, small, small, small, small, _const_spec((1, V_HEAD_DIM)),
                  hbm, hbm],
        out_specs=(full, row),
        scratch_shapes=[pltpu.VMEM((DEC_SLOTS, DEC_GROUP, ATTN_WIDTH, PAGE_SIZE), F32),
                        pltpu.VMEM((DEC_SLOTS, DEC_GROUP, ATTN_WIDTH, PAGE_SIZE), F32),
                        pltpu.SemaphoreType.DMA((2, DEC_SLOTS))],
    )
    return pl.pallas_call(
        _finish_decode_kernel,
        grid_spec=grid_spec,
        out_shape=(jax.ShapeDtypeStruct((rows, D_MODEL), F32),
                   jax.ShapeDtypeStruct((Bd, 1, ATTN_WIDTH), BF16)),
        name="finish_decode",
        compiler_params=pltpu.CompilerParams(
            dimension_semantics=("arbitrary",),
            vmem_limit_bytes=VMEM_LIMIT_BYTES),
    )(page_table, a, c, x, wo, gpost, gpre, gfpost, wg, wu, wd,
      q, qcol, kn, vn, lq1, lk1, lq2, lk2, subln, ck, cv)


def _finish_call(a, c, x, wo, gpost, gpre, gfpost, wg, wu, wd, tm):
    rows = x.shape[0]
    half = pl.BlockSpec((tm, 512), lambda i: (i, 0))
    full = pl.BlockSpec((tm, D_MODEL), lambda i: (i, 0))
    vec = _const_spec((1, D_MODEL))
    return pl.pallas_call(
        _finish_kernel,
        grid=(rows // tm,),
        in_specs=[half, half, full, _const_spec(wo.shape), vec, vec, vec,
                  _const_spec(wg.shape), _const_spec(wu.shape), _const_spec(wd.shape)],
        out_specs=full,
        out_shape=jax.ShapeDtypeStruct((rows, D_MODEL), F32),
        name="finish",
        compiler_params=pltpu.CompilerParams(
            dimension_semantics=("arbitrary",),
            vmem_limit_bytes=VMEM_LIMIT_BYTES),
    )(a, c, x, wo, gpost, gpre, gfpost, wg, wu, wd)


def kernel(x_prompt, x_sample, cache_k, cache_v, state_conv, page_table, meta_tokens,
           ln_mix_pre, ln_mix_post, w_in, lambda_q1, lambda_k1, lambda_q2, lambda_k2,
           subln_w, conv_w, conv_b, conv_ln_w, conv_ln_b, w_out, ln_ffn_pre, ln_ffn_post,
           w_gate, w_up, w_down):
    B, S, _ = x_prompt.shape
    Bd = x_sample.shape[0]
    T = S + N_META
    li = 0
    w_in_bf = w_in[li].astype(BF16)
    wo_bf = w_out[li].astype(BF16)
    wg_bf = w_gate[li].astype(BF16)
    wu_bf = w_up[li].astype(BF16)
    wd_bf = w_down[li].astype(BF16)
    g_pre = ln_mix_pre[li][None]
    lam_args = (lambda_q1[li][None], lambda_k1[li][None], lambda_q2[li][None],
                lambda_k2[li][None], subln_w[li][None])
    conv_args = (conv_w[li], conv_b[li][None], conv_ln_w[li][None], conv_ln_b[li][None])
    fin_args = (wo_bf, ln_mix_post[li][None], ln_ffn_pre[li][None], ln_ffn_post[li][None],
                wg_bf, wu_bf, wd_bf)

    small = _proj_call(jnp.concatenate([meta_tokens, x_sample.reshape(Bd, D_MODEL)], axis=0),
                       g_pre, w_in_bf)
    _, mk, mv, mu = (t[:N_META] for t in small)
    qs, ks, vs, us = (t[N_META:] for t in small)
    cw_tiles = jnp.broadcast_to(conv_w[li][:, None, :], (CONV_WIDTH, 8, CONV_DIM))
    q_bf, k_bf, v_bf, new_k, new_v, conv_o, new_conv = _prompt_proj_call(
        x_prompt, g_pre, w_in_bf, mk, mv, mu, cw_tiles, *conv_args[1:])
    pad = ((0, META_BLK - N_META), (0, 0))
    attn_o = _attn_call(q_bf, k_bf, v_bf, jnp.pad(mk.astype(BF16), pad), jnp.pad(mv.astype(BF16), pad),
                        *lam_args[:4], subln_w[li][:, None])

    n_pool = cache_k.shape[1]
    ck = jnp.transpose(cache_k[li], (0, 2, 3, 1)).reshape(n_pool, ATTN_WIDTH, PAGE_SIZE)
    cv = cache_v[li].reshape(n_pool, PAGE_SIZE * N_HEADS, V_HEAD_DIM)
    qcol = jnp.broadcast_to(qs[:, :, None], (Bd, ATTN_WIDTH, PAGE_SIZE))
    y_prompt, attn_s = _finish_decode_call(
        page_table, attn_o.reshape(B * S, 512), conv_o.reshape(B * S, 512),
        x_prompt.reshape(B * S, D_MODEL), *fin_args,
        qs[:, None], qcol, ks[:, None], vs[:, None], *lam_args, ck, cv, tm=B * S // Bd)
    conv_s, new_state = _sample_conv_call(jnp.transpose(state_conv[li], (1, 0, 2)), us, *conv_args)
    y_sample = _finish_call(attn_s.reshape(Bd, 512), conv_s, x_sample.reshape(Bd, D_MODEL),
                            *fin_args, tm=Bd)

    return (y_prompt.reshape(B, S, D_MODEL),
            y_sample.reshape(Bd, 1, D_MODEL),
            new_k.reshape(1, B, T, N_MAPS, HEAD_DIM),
            new_v.reshape(1, B, T, N_HEADS, V_HEAD_DIM),
            new_conv[None],
            ks.reshape(1, Bd, 1, N_MAPS, HEAD_DIM),
            vs.reshape(1, Bd, 1, N_HEADS, V_HEAD_DIM),
            jnp.transpose(new_state, (1, 0, 2))[None])
```

```python
import functools
import math

import jax
import jax.numpy as jnp
from jax import lax
from jax.experimental import pallas as pl
from jax.experimental.pallas import tpu as pltpu

D_MODEL = 1024
N_META = 16
ATTN_WIDTH = 512
CONV_DIM = 512
N_HEADS = 4
HEAD_DIM = 64
V_HEAD_DIM = 128
N_MAPS = 2 * N_HEADS
CONV_WIDTH = 31
FFN_HIDDEN = 2816
PAGE_SIZE = 128
EPS = 1e-6
NEG_INF = -1e30
LAMBDA_INIT = 0.8 - 0.6 * math.exp(-0.3 * 0)
SCALE = HEAD_DIM ** -0.5
SLOPES = tuple(2.0 ** (-8.0 * (h + 1) / N_HEADS) for h in range(N_HEADS))

VMEM_LIMIT_BYTES = 56 * 1024 * 1024

F32 = jnp.float32
BF16 = jnp.bfloat16


def _rms(x, g):
    return x * lax.rsqrt(jnp.mean(x * x, axis=-1, keepdims=True) + EPS) * g


def _const_spec(shape):
    nd = len(shape)
    return pl.BlockSpec(shape, lambda *_: (0,) * nd, pipeline_mode=pl.Buffered(1))


def _project(x, g, w_ref):
    xn = _rms(x, g).astype(BF16)
    seg = lambda c: jnp.dot(xn, w_ref[:, c * 512:(c + 1) * 512], preferred_element_type=F32)
    q = seg(0) * SCALE
    k = seg(1)
    v = seg(2)
    u = seg(3) * jax.nn.sigmoid(seg(4))
    return q, k, v, u


def _proj_kernel(x_ref, g_ref, w_ref, q_ref, k_ref, v_ref, u_ref):
    q, k, v, u = _project(x_ref[...], g_ref[...], w_ref)
    q_ref[...] = q
    k_ref[...] = k
    v_ref[...] = v
    u_ref[...] = u


def _proj_call(x, g, w_bf):
    rows = x.shape[0]
    out = jax.ShapeDtypeStruct((rows, 512), F32)
    return pl.pallas_call(
        _proj_kernel,
        out_shape=(out, out, out, out),
        name="proj_small",
        compiler_params=pltpu.CompilerParams(vmem_limit_bytes=VMEM_LIMIT_BYTES),
    )(x, g, w_bf)


PROJ_TM = 512
U_PAD = 48
U_ROWS_OFF = U_PAD - (CONV_WIDTH - 1)


CONV_SUB = 32
PROJ_RB = 256


def _ln_swish(y, w, b):
    mu = jnp.mean(y, axis=-1, keepdims=True)
    yc = y - mu
    z = yc * lax.rsqrt(jnp.mean(yc * yc, axis=-1, keepdims=True) + EPS) * w + b
    return z * jax.nn.sigmoid(z)


def _zero_after(x):
    bits = pltpu.bitcast(x[0:8, 0:128], jnp.uint32)
    z = ((bits >> 16) >> 16).astype(F32)[0:1, :]
    return jnp.concatenate([z] * (CONV_DIM // 128), axis=1)


def _conv_rows(u_ref, win_start, cw_ref, bias):
    lead = U_ROWS_OFF % 8
    win = u_ref[pl.ds(win_start, CONV_SUB + 32), :]
    acc = jnp.broadcast_to(bias, (CONV_SUB, CONV_DIM))
    for res in range(8):
        rows = CONV_SUB + (8 if res else 0)
        part = None
        for w in range(CONV_WIDTH):
            if (lead + w) % 8 == res:
                off = lead + w - res
                term = win[off:off + rows, :].reshape(rows // 8, 8, CONV_DIM) * cw_ref[w]
                part = term if part is None else part + term
        acc = acc + part.reshape(rows, CONV_DIM)[res:res + CONV_SUB, :]
    return acc


def _store_value_rows(nv_ref, first_token, v):
    for h in range(N_HEADS):
        nv_ref[0, pl.ds(first_token * N_HEADS + h, v.shape[0], stride=N_HEADS), :] = (
            v[:, h * V_HEAD_DIM:(h + 1) * V_HEAD_DIM])


def _prompt_proj_kernel(x_ref, g_ref, w_ref, mk_ref, mv_ref, mu_ref,
                        cw_ref, cb_ref, lw_ref, lb_ref,
                        q_ref, kb_ref, vb_ref, nk_ref, nv_ref, co_ref, nc_ref,
                        u_ref, y_ref):
    i = pl.program_id(1)
    nt = pl.num_programs(1)

    @pl.when(i == 0)
    def _():
        nk_ref[0, 0:N_META, :] = mk_ref[...]
        _store_value_rows(nv_ref, 0, mv_ref[...])
        u_ref[0:U_PAD - N_META, :] = jnp.zeros((U_PAD - N_META, CONV_DIM), F32)
        u_ref[U_PAD - N_META:U_PAD, :] = mu_ref[...]

    xn = _rms(x_ref[0], g_ref[...]).astype(BF16)
    seg = lambda c: jnp.dot(xn, w_ref[:, c * 512:(c + 1) * 512], preferred_element_type=F32)
    r = i * PROJ_TM
    u_ref[pl.ds(pl.multiple_of(r + U_PAD, 8), PROJ_TM), :] = seg(3) * jax.nn.sigmoid(seg(4))

    win0 = r + U_ROWS_OFF - U_ROWS_OFF % 8
    n_sub = PROJ_TM // CONV_SUB

    def conv_passes(lo, hi, bias):
        for s in range(lo, hi):
            y_ref[s * CONV_SUB:(s + 1) * CONV_SUB, :] = _conv_rows(
                u_ref, pl.multiple_of(win0 + s * CONV_SUB, 8), cw_ref, bias)

    n_blk = PROJ_TM // PROJ_RB
    for rb in range(n_blk):
        rows = slice(rb * PROJ_RB, (rb + 1) * PROJ_RB)
        out_rows = pl.ds(pl.multiple_of(r + N_META + rb * PROJ_RB, 8), PROJ_RB)
        blk = lambda c: jnp.dot(xn[rows], w_ref[:, c * 512:(c + 1) * 512],
                                preferred_element_type=F32)
        q_ref[0, rows, :] = (blk(0) * SCALE).astype(BF16)
        k = blk(1)
        kb_ref[0, rows, :] = k.astype(BF16)
        nk_ref[0, out_rows, :] = k
        v = blk(2)
        vb_ref[0, rows, :] = v.astype(BF16)
        _store_value_rows(nv_ref, r + N_META + rb * PROJ_RB, v)
        conv_passes(rb * n_sub // n_blk, (rb + 1) * n_sub // n_blk, cb_ref[...] + _zero_after(v))
    co_ref[0] = _ln_swish(y_ref[...], lw_ref[...], lb_ref[...]).astype(BF16)

    @pl.when(i == nt - 1)
    def _():
        last = nt * PROJ_TM + U_PAD
        nc_ref[0] = u_ref[last - (CONV_WIDTH - 1):last, :]


def _prompt_proj_call(x, g, w_bf, mk, mv, mu, cw_tiles, cb, lw, lb):
    B, S, _ = x.shape
    T = S + N_META
    nt = S // PROJ_TM
    tile = pl.BlockSpec((1, PROJ_TM, 512), lambda b, i: (b, i, 0))
    full = lambda rows: pl.BlockSpec((1, rows, 512), lambda b, i: (b, 0, 0))
    bf = jax.ShapeDtypeStruct((B, S, 512), BF16)
    vec = _const_spec((1, CONV_DIM))
    return pl.pallas_call(
        _prompt_proj_kernel,
        grid=(B, nt),
        in_specs=[
            pl.BlockSpec((1, PROJ_TM, D_MODEL), lambda b, i: (b, i, 0)),
            _const_spec((1, D_MODEL)),
            _const_spec(w_bf.shape),
            _const_spec((N_META, 512)),
            _const_spec((N_META, 512)),
            _const_spec((N_META, 512)),
            _const_spec((CONV_WIDTH, 8, CONV_DIM)), vec, vec, vec,
        ],
        out_specs=(tile, tile, tile, full(T),
                   pl.BlockSpec((1, T * N_HEADS, V_HEAD_DIM), lambda b, i: (b, 0, 0)),
                   tile, full(CONV_WIDTH - 1)),
        out_shape=(bf, bf, bf,
                   jax.ShapeDtypeStruct((B, T, 512), F32),
                   jax.ShapeDtypeStruct((B, T * N_HEADS, V_HEAD_DIM), F32),
                   bf,
                   jax.ShapeDtypeStruct((B, CONV_WIDTH - 1, CONV_DIM), F32)),
        scratch_shapes=[pltpu.VMEM((S + U_PAD, CONV_DIM), F32),
                        pltpu.VMEM((PROJ_TM, CONV_DIM), F32)],
        name="prompt_proj",
        compiler_params=pltpu.CompilerParams(
            dimension_semantics=("arbitrary", "arbitrary"),
            vmem_limit_bytes=VMEM_LIMIT_BYTES),
    )(x, g, w_bf, mk, mv, mu, cw_tiles, cb, lw, lb)


def _sample_conv_kernel(st_ref, u_ref, cw_ref, cb_ref, lw_ref, lb_ref, o_ref, ns_ref):
    u = u_ref[...]
    acc = cb_ref[...] + u * cw_ref[CONV_WIDTH - 1:CONV_WIDTH, :]
    for w in range(CONV_WIDTH - 1):
        acc = acc + st_ref[w] * cw_ref[w:w + 1, :]
    o_ref[...] = _ln_swish(acc, lw_ref[...], lb_ref[...]).astype(BF16)
    ns_ref[0:CONV_WIDTH - 2] = st_ref[1:CONV_WIDTH - 1]
    ns_ref[CONV_WIDTH - 2] = u


def _sample_conv_call(state_wbc, u, cw, cb, lw, lb):
    return pl.pallas_call(
        _sample_conv_kernel,
        out_shape=(jax.ShapeDtypeStruct(u.shape, BF16),
                   jax.ShapeDtypeStruct(state_wbc.shape, F32)),
        name="sample_conv",
    )(state_wbc, u, cw, cb, lw, lb)


def _lambda(lq1, lk1, lq2, lk2):
    s1 = jnp.sum(lq1 * lk1, axis=-1, keepdims=True)
    s2 = jnp.sum(lq2 * lk2, axis=-1, keepdims=True)
    return jnp.exp(s1) - jnp.exp(s2) + LAMBDA_INIT


def _head_out(o0, o1, lam, subln):
    o = o0 - lam * o1
    return _rms(o, subln) * (1.0 - LAMBDA_INIT)


ATT_TQ = 256
ATT_TK = 256
ATT_TK_WIDE = 512
META_BLK = 128
QK_AHEAD = 4
VT_ROWS = V_HEAD_DIM + 16


def _augment(x, pos, is_query, slope):
    lane = lax.broadcasted_iota(jnp.int32, x.shape, 1)
    hi = (pos >> 6).astype(F32)
    lo = (pos & 63).astype(F32)
    out = []
    for j in range(2):
        a = lane - (64 if j == 0 else 0)
        if is_query:
            aug = jnp.where(a == 0, hi * (-64.0 * slope),
                  jnp.where(a == 1, lo * (-slope),
                  jnp.where(a == 2, 64.0 * slope,
                  jnp.where(a == 3, slope, 0.0))))
        else:
            aug = jnp.where(a == 0, 1.0,
                  jnp.where(a == 1, 1.0,
                  jnp.where(a == 2, hi,
                  jnp.where(a == 3, lo, 0.0))))
        own = (lane < 64) if j == 0 else (lane >= 64)
        out.append(jnp.where(own, x, aug).astype(BF16))
    return out


N_CAST = 4


def _attn_kernel(q_ref, kb_ref, vb_ref, mkb_ref, mvb_ref,
                 lq1_ref, lk1_ref, lq2_ref, lk2_ref, subcol_ref, *rest):
    w_f32 = rest[:N_CAST]
    o_ref = rest[N_CAST]
    w_bf16 = rest[N_CAST + 1:2 * N_CAST + 1]
    kexp_ref, vt_ref, qexp_ref, m_ref, acc_ref = rest[2 * N_CAST + 1:]
    qi = pl.program_id(1)
    S = kb_ref.shape[1]

    for src, dst in zip(w_f32, w_bf16):
        dst[...] = src[...].astype(BF16)

    def chunk_step(parts, first):
        def scores(mm):
            mcols = slice(mm * 128, (mm + 1) * 128)
            out = []
            for k_rows, _, mask in parts:
                s = lax.dot_general(kexp_ref[k_rows, mcols], qexp_ref[:, mcols],
                                    (((1,), (1,)), ((), ())), preferred_element_type=F32)
                out.append(s if mask is None else jnp.where(mask, s, NEG_INF))
            return out

        pending = [scores(mm) for mm in range(QK_AHEAD)]
        for mm in range(N_MAPS):
            h = mm // 2
            if mm + QK_AHEAD < N_MAPS:
                pending.append(scores(mm + QK_AHEAD))
            m_new = None if first else m_ref[mm]
            for s in pending[mm]:
                m_cur = jnp.max(s, axis=0, keepdims=True)
                m_new = m_cur if m_new is None else jnp.maximum(m_new, m_cur)
            pv = None
            for s, (_, v_cols, _) in zip(pending[mm], parts):
                vt = vt_ref[h * VT_ROWS:(h + 1) * VT_ROWS, v_cols]
                pb = jnp.exp(s - m_new).astype(BF16)
                if pb.shape[0] < vt.shape[1]:
                    pb = jnp.concatenate(
                        [pb, jnp.zeros((vt.shape[1] - pb.shape[0], pb.shape[1]), BF16)], axis=0)
                term = jnp.dot(vt, pb, preferred_element_type=F32)
                pv = term if pv is None else pv + term
            if first:
                acc_ref[mm] = pv
            else:
                acc_ref[mm] = jnp.exp(m_ref[mm] - m_new) * acc_ref[mm] + pv
            m_ref[mm] = m_new

    @pl.when(qi == 0)
    def _():
        ones_rows = (lax.broadcasted_iota(jnp.int32, (VT_ROWS - V_HEAD_DIM, ATT_TK), 0) == 0)

        def put_vt(cols_off, vc):
            n = vc.shape[0]
            for h in range(N_HEADS):
                vt_ref[h * VT_ROWS:h * VT_ROWS + V_HEAD_DIM, cols_off:cols_off + n] = (
                    vc[:, h * 128:(h + 1) * 128].astype(F32).T.astype(BF16))
                vt_ref[h * VT_ROWS + V_HEAD_DIM:(h + 1) * VT_ROWS, cols_off:cols_off + n] = (
                    ones_rows[:, :n].astype(F32).astype(BF16))

        put_vt(0, mvb_ref[...])
        for c in range(S // ATT_TK):
            put_vt(META_BLK + c * ATT_TK, vb_ref[0, c * ATT_TK:(c + 1) * ATT_TK, :])

        def put(rows_off, kc, pos0):
            rows = kc.shape[0]
            pos = pos0 + lax.broadcasted_iota(jnp.int32, (rows, 128), 0)
            for h in range(N_HEADS):
                e0, e1 = _augment(kc[:, h * 128:(h + 1) * 128].astype(F32), pos, False, 0.0)
                kexp_ref[rows_off:rows_off + rows, (2 * h) * 128:(2 * h + 1) * 128] = e0
                kexp_ref[rows_off:rows_off + rows, (2 * h + 1) * 128:(2 * h + 2) * 128] = e1

        put(0, mkb_ref[...], 0)
        for c in range(S // ATT_TK):
            put(META_BLK + c * ATT_TK, kb_ref[0, c * ATT_TK:(c + 1) * ATT_TK, :],
                N_META + c * ATT_TK)

    qpos = N_META + qi * ATT_TQ + lax.broadcasted_iota(jnp.int32, (ATT_TQ, 128), 0)
    q = q_ref[0].astype(F32)
    for h in range(N_HEADS):
        e0, e1 = _augment(q[:, h * 128:(h + 1) * 128], qpos, True, SLOPES[h])
        qexp_ref[:, (2 * h) * 128:(2 * h + 1) * 128] = e0
        qexp_ref[:, (2 * h + 1) * 128:(2 * h + 2) * 128] = e1

    lam = _lambda(lq1_ref[...], lk1_ref[...], lq2_ref[...], lk2_ref[...])
    key = lax.broadcasted_iota(jnp.int32, (ATT_TK, ATT_TQ), 0)
    qry = lax.broadcasted_iota(jnp.int32, (ATT_TK, ATT_TQ), 1)
    causal = key <= qry

    def real_part(first_key, width, mask):
        r = pl.multiple_of(META_BLK + first_key, 128)
        return (pl.ds(r, width), pl.ds(r, width), mask)

    chunk_step([(slice(0, N_META), slice(0, META_BLK), None),
                real_part(qi * ATT_TK, ATT_TK, causal)], True)

    def body(c, carry):
        chunk_step([real_part(c * ATT_TK_WIDE, ATT_TK_WIDE, None)], False)
        return carry

    n_wide = lax.div(qi * ATT_TK, ATT_TK_WIDE)
    lax.fori_loop(0, n_wide, body, 0)

    @pl.when(n_wide * ATT_TK_WIDE < qi * ATT_TK)
    def _():
        chunk_step([real_part(n_wide * ATT_TK_WIDE, ATT_TK, None)], False)

    for h in range(N_HEADS):
        a0, a1 = acc_ref[2 * h], acc_ref[2 * h + 1]
        o0 = a0[:V_HEAD_DIM] / a0[V_HEAD_DIM:V_HEAD_DIM + 1]
        o1 = a1[:V_HEAD_DIM] / a1[V_HEAD_DIM:V_HEAD_DIM + 1]
        o = o0 - lam * o1
        o = o * lax.rsqrt(jnp.mean(o * o, axis=0, keepdims=True) + EPS)
        o = o * subcol_ref[...] * (1.0 - LAMBDA_INIT)
        o_ref[0, :, h * 128:(h + 1) * 128] = o.T.astype(BF16)


def _attn_call(q, kb, vb, mkb, mvb, lq1, lk1, lq2, lk2, subln, weights):
    B, S, _ = q.shape
    nq = S // ATT_TQ
    steps = B * nq
    assert len(weights) == N_CAST
    per_b = pl.BlockSpec((1, S, 512), lambda b, i: (b, 0, 0))
    tile = pl.BlockSpec((1, ATT_TQ, 512), lambda b, i: (b, i, 0))
    small = _const_spec((1, HEAD_DIM))

    def slab_spec(w):
        rows = next(r for r in range(16, w.shape[0] + 1, 16)
                    if w.shape[0] % r == 0 and w.shape[0] // r <= steps)
        last = w.shape[0] // rows - 1
        return pl.BlockSpec((rows, w.shape[1]), lambda b, i: (jnp.minimum(b * nq + i, last), 0))

    slab_specs = [slab_spec(w) for w in weights]
    out = pl.pallas_call(
        _attn_kernel,
        grid=(B, nq),
        in_specs=[tile, per_b, per_b,
                  _const_spec((META_BLK, 512)), _const_spec((META_BLK, 512)),
                  small, small, small, small, _const_spec((V_HEAD_DIM, 1))] + slab_specs,
        out_specs=[tile] + slab_specs,
        out_shape=[jax.ShapeDtypeStruct((B, S, 512), BF16)]
                  + [jax.ShapeDtypeStruct(w.shape, BF16) for w in weights],
        scratch_shapes=[pltpu.VMEM((META_BLK + S, N_MAPS * 128), BF16),
                        pltpu.VMEM((N_HEADS * VT_ROWS, META_BLK + S), BF16),
                        pltpu.VMEM((ATT_TQ, N_MAPS * 128), BF16),
                        pltpu.VMEM((N_MAPS, 1, ATT_TQ), F32),
                        pltpu.VMEM((N_MAPS, VT_ROWS, ATT_TQ), F32)],
        name="attn",
        compiler_params=pltpu.CompilerParams(
            dimension_semantics=("arbitrary", "arbitrary"),
            vmem_limit_bytes=VMEM_LIMIT_BYTES),
    )(q, kb, vb, mkb, mvb, lq1, lk1, lq2, lk2, subln, *weights)
    return out[0], out[1:]


KEYS_PER_VBLK = 128 // N_HEADS


def _decode_consts():
    sub = lax.broadcasted_iota(jnp.int32, (N_MAPS, PAGE_SIZE), 0)
    lane = lax.broadcasted_iota(jnp.int32, (N_MAPS, PAGE_SIZE), 1)
    head = sub >> 1
    slope = jnp.where(head == 0, SLOPES[0],
            jnp.where(head == 1, SLOPES[1],
            jnp.where(head == 2, SLOPES[2], SLOPES[3])))
    return dict(lane=lane, head=head, slope=slope,
                own_head=(lane & (N_HEADS - 1)) == head, key_of_row=lane >> 2)


def _decode_init(q_row, k_new, v_new, head):
    sub_w = lax.broadcasted_iota(jnp.int32, (N_MAPS, ATTN_WIDTH), 0)
    lane_w = lax.broadcasted_iota(jnp.int32, (N_MAPS, ATTN_WIDTH), 1)
    qbd = jnp.where((lane_w >> 6) == sub_w, jnp.broadcast_to(q_row, (N_MAPS, ATTN_WIDTH)), 0.0)
    m = jnp.sum(qbd * k_new, axis=-1, keepdims=True)
    acc = jnp.zeros((N_MAPS, V_HEAD_DIM), F32)
    for h in range(N_HEADS):
        acc = jnp.where(head == h, v_new[:, h * 128:(h + 1) * 128], acc)
    return m, jnp.ones((N_MAPS, 1), F32), acc


def _decode_probs(m, l, qcol, k_pages, first_page, past, cst):
    partial = [[None] * N_MAPS for _ in k_pages]
    for mm in range(N_MAPS):
        for blk in range(HEAD_DIM // 8):
            rows = slice(mm * HEAD_DIM + blk * 8, mm * HEAD_DIM + blk * 8 + 8)
            qv = qcol[rows, :]
            for p, kp in enumerate(k_pages):
                t = kp[rows, :] * qv
                partial[p][mm] = t if partial[p][mm] is None else partial[p][mm] + t
    scores = []
    for p in range(len(k_pages)):
        s = jnp.sum(jnp.stack(partial[p]), axis=1)
        dist = past - ((first_page + p) * PAGE_SIZE + cst["lane"])
        scores.append(s - cst["slope"] * dist.astype(F32))
    s = jnp.concatenate(scores, axis=1)
    m_new = jnp.maximum(m, jnp.max(s, axis=-1, keepdims=True))
    alpha = jnp.exp(m - m_new)
    pr = jnp.exp(s - m_new)
    l = alpha * l + jnp.sum(pr, axis=-1, keepdims=True)
    prb = pr.astype(BF16).astype(F32)
    spread_probs = []
    for p in range(len(k_pages)):
        prp = prb[:, p * PAGE_SIZE:(p + 1) * PAGE_SIZE]
        spread = [jnp.where(cst["own_head"],
                            jnp.take_along_axis(prp, cst["key_of_row"] + c * KEYS_PER_VBLK, axis=1),
                            0.0)
                  for c in range(N_HEADS)]
        spread_probs.append(jnp.concatenate(spread, axis=1).astype(BF16))
    return m_new, l, alpha, spread_probs


def _decode_values(acc, alpha, spread_probs, v_pages):
    pv = jnp.zeros((N_MAPS, V_HEAD_DIM), F32)
    for pe, vp in zip(spread_probs, v_pages):
        pv = pv + jnp.dot(pe, vp.astype(BF16), preferred_element_type=F32)
    return alpha * acc + pv


def _decode_out(l, acc, lam, subln, o_ref):
    o = acc / l
    for h in range(N_HEADS):
        ho = _head_out(o[2 * h:2 * h + 1], o[2 * h + 1:2 * h + 2], lam, subln)
        o_ref[0, :, h * 128:(h + 1) * 128] = ho.astype(BF16)


FFN_CHUNK = 256


N_FFN_CHUNKS = FFN_HIDDEN // FFN_CHUNK


def _finish_rows(a_ref, c_ref, x_ref, wo_ref, gpost_ref, gpre_ref, gfpost_ref,
                 wg_ref, wu_ref, wd_ref, o_ref, before_chunk=None, after_chunk=None):
    mix = (jnp.dot(a_ref[...], wo_ref[0:ATTN_WIDTH, :], preferred_element_type=F32)
           + jnp.dot(c_ref[...], wo_ref[ATTN_WIDTH:, :], preferred_element_type=F32))
    x = x_ref[...] + _rms(mix, gpost_ref[...])
    hb = _rms(x, gpre_ref[...]).astype(BF16)
    f = jnp.zeros(x.shape, F32)
    for c in range(N_FFN_CHUNKS):
        cols = slice(c * FFN_CHUNK, (c + 1) * FFN_CHUNK)
        if before_chunk is not None:
            before_chunk(c)
        gate = jnp.dot(hb, wg_ref[:, cols], preferred_element_type=F32)
        up = jnp.dot(hb, wu_ref[:, cols], preferred_element_type=F32)
        act = (gate * jax.nn.sigmoid(gate) * up).astype(BF16)
        f = f + jnp.dot(act, wd_ref[cols, :], preferred_element_type=F32)
        if after_chunk is not None:
            after_chunk(c)
    o_ref[...] = x + _rms(f, gfpost_ref[...])


def _finish_kernel(*refs):
    _finish_rows(*refs)


DEC_GROUP = 8
DEC_AHEAD = 2
DEC_SLOTS = DEC_AHEAD + 1


def _finish_decode_kernel(pt_ref, a_ref, c_ref, x_ref, wo_ref, gpost_ref, gpre_ref, gfpost_ref,
                          wg_ref, wu_ref, wd_ref,
                          q_ref, qcol_ref, kn_ref, vn_ref,
                          lq1_ref, lk1_ref, lq2_ref, lk2_ref, sub_ref, ck_hbm, cv_hbm,
                          o_ref, os_ref, kbuf, vbuf, sem):
    i = pl.program_id(0)
    n_steps = pl.num_programs(0)
    n_pages = pt_ref.shape[1]
    n_groups = n_pages // DEC_GROUP
    past = n_pages * PAGE_SIZE

    def group_copies(b, g, slot):
        copies = []
        for p in range(DEC_GROUP):
            page = pt_ref[b, g * DEC_GROUP + p]
            copies.append(pltpu.make_async_copy(ck_hbm.at[page], kbuf.at[slot, p], sem.at[0, slot]))
            copies.append(pltpu.make_async_copy(cv_hbm.at[page], vbuf.at[slot, p], sem.at[1, slot]))
        return copies

    def wait_group(slot):
        pltpu.make_async_copy(ck_hbm.at[pl.ds(0, DEC_GROUP)], kbuf.at[slot], sem.at[0, slot]).wait()
        pltpu.make_async_copy(cv_hbm.at[pl.ds(0, DEC_GROUP)], vbuf.at[slot], sem.at[1, slot]).wait()

    def slot_of(g):
        return lax.rem(i * n_groups + g, DEC_SLOTS)

    def start_group(g):
        if g < n_groups:
            for cp in group_copies(i, g, slot_of(g)):
                cp.start()
        else:
            @pl.when(i + 1 < n_steps)
            def _():
                for cp in group_copies(i + 1, g - n_groups, slot_of(g)):
                    cp.start()

    @pl.when(i == 0)
    def _():
        for g in range(DEC_AHEAD):
            start_group(g)

    cst = _decode_consts()
    m, l, acc = _decode_init(q_ref[0], kn_ref[0], vn_ref[0], cst["head"])
    st = dict(m=m, l=l, acc=acc)

    def before_chunk(g):
        if g >= n_groups:
            return
        start_group(g + DEC_AHEAD)
        slot = slot_of(g)
        wait_group(slot)
        st["m"], st["l"], st["alpha"], st["probs"] = _decode_probs(
            st["m"], st["l"], qcol_ref.at[0], [kbuf.at[slot, p] for p in range(DEC_GROUP)],
            g * DEC_GROUP, past, cst)

    def after_chunk(g):
        if g >= n_groups:
            return
        slot = slot_of(g)
        st["acc"] = _decode_values(st["acc"], st["alpha"], st["probs"],
                                   [vbuf[slot, p] for p in range(DEC_GROUP)])

    _finish_rows(a_ref, c_ref, x_ref, wo_ref, gpost_ref, gpre_ref, gfpost_ref,
                 wg_ref, wu_ref, wd_ref, o_ref, before_chunk, after_chunk)
    lam = _lambda(lq1_ref[...], lk1_ref[...], lq2_ref[...], lk2_ref[...])
    _decode_out(st["l"], st["acc"], lam, sub_ref[...], os_ref)


def _finish_decode_call(page_table, a, c, x, wo, gpost, gpre, gfpost, wg, wu, wd,
                        q, qcol, kn, vn, lq1, lk1, lq2, lk2, subln, ck, cv, tm):
    rows = x.shape[0]
    Bd, n_pages = page_table.shape
    assert rows // tm == Bd and n_pages % DEC_GROUP == 0
    assert DEC_AHEAD <= n_pages // DEC_GROUP <= N_FFN_CHUNKS
    half = pl.BlockSpec((tm, 512), lambda i, pt: (i, 0))
    full = pl.BlockSpec((tm, D_MODEL), lambda i, pt: (i, 0))
    vec = _const_spec((1, D_MODEL))
    row = pl.BlockSpec((1, 1, ATTN_WIDTH), lambda i, pt: (i, 0, 0))
    small = _const_spec((1, HEAD_DIM))
    hbm = pl.BlockSpec(memory_space=pl.ANY)
    grid_spec = pltpu.PrefetchScalarGridSpec(
        num_scalar_prefetch=1,
        grid=(Bd,),
        in_specs=[half, half, full, _const_spec(wo.shape), vec, vec, vec,
                  _const_spec(wg.shape), _const_spec(wu.shape), _const_spec(wd.shape),
                  row, pl.BlockSpec((1, ATTN_WIDTH, PAGE_SIZE), lambda i, pt: (i, 0, 0)),
                  row, row, small, small, small, small, _const_spec((1, V_HEAD_DIM)),
                  hbm, hbm],
        out_specs=(full, row),
        scratch_shapes=[pltpu.VMEM((DEC_SLOTS, DEC_GROUP, ATTN_WIDTH, PAGE_SIZE), F32),
                        pltpu.VMEM((DEC_SLOTS, DEC_GROUP, ATTN_WIDTH, PAGE_SIZE), F32),
                        pltpu.SemaphoreType.DMA((2, DEC_SLOTS))],
    )
    return pl.pallas_call(
        _finish_decode_kernel,
        grid_spec=grid_spec,
        out_shape=(jax.ShapeDtypeStruct((rows, D_MODEL), F32),
                   jax.ShapeDtypeStruct((Bd, 1, ATTN_WIDTH), BF16)),
        name="finish_decode",
        compiler_params=pltpu.CompilerParams(
            dimension_semantics=("arbitrary",),
            vmem_limit_bytes=VMEM_LIMIT_BYTES),
    )(page_table, a, c, x, wo, gpost, gpre, gfpost, wg, wu, wd,
      q, qcol, kn, vn, lq1, lk1, lq2, lk2, subln, ck, cv)


def _finish_call(a, c, x, wo, gpost, gpre, gfpost, wg, wu, wd, tm):
    rows = x.shape[0]
    half = pl.BlockSpec((tm, 512), lambda i: (i, 0))
    full = pl.BlockSpec((tm, D_MODEL), lambda i: (i, 0))
    vec = _const_spec((1, D_MODEL))
    return pl.pallas_call(
        _finish_kernel,
        grid=(rows // tm,),
        in_specs=[half, half, full, _const_spec(wo.shape), vec, vec, vec,
                  _const_spec(wg.shape), _const_spec(wu.shape), _const_spec(wd.shape)],
        out_specs=full,
        out_shape=jax.ShapeDtypeStruct((rows, D_MODEL), F32),
        name="finish",
        compiler_params=pltpu.CompilerParams(
            dimension_semantics=("arbitrary",),
            vmem_limit_bytes=VMEM_LIMIT_BYTES),
    )(a, c, x, wo, gpost, gpre, gfpost, wg, wu, wd)


def kernel(x_prompt, x_sample, cache_k, cache_v, state_conv, page_table, meta_tokens,
           ln_mix_pre, ln_mix_post, w_in, lambda_q1, lambda_k1, lambda_q2, lambda_k2,
           subln_w, conv_w, conv_b, conv_ln_w, conv_ln_b, w_out, ln_ffn_pre, ln_ffn_post,
           w_gate, w_up, w_down):
    B, S, _ = x_prompt.shape
    Bd = x_sample.shape[0]
    T = S + N_META
    li = 0
    w_in_bf = w_in[li].astype(BF16)
    g_pre = ln_mix_pre[li][None]
    lam_args = (lambda_q1[li][None], lambda_k1[li][None], lambda_q2[li][None],
                lambda_k2[li][None], subln_w[li][None])
    conv_args = (conv_w[li], conv_b[li][None], conv_ln_w[li][None], conv_ln_b[li][None])

    small = _proj_call(jnp.concatenate([meta_tokens, x_sample.reshape(Bd, D_MODEL)], axis=0),
                       g_pre, w_in_bf)
    _, mk, mv, mu = (t[:N_META] for t in small)
    qs, ks, vs, us = (t[N_META:] for t in small)
    cw_tiles = jnp.broadcast_to(conv_w[li][:, None, :], (CONV_WIDTH, 8, CONV_DIM))
    q_bf, k_bf, v_bf, new_k, new_v, conv_o, new_conv = _prompt_proj_call(
        x_prompt, g_pre, w_in_bf, mk, mv, mu, cw_tiles, *conv_args[1:])
    pad = ((0, META_BLK - N_META), (0, 0))
    attn_o, (wo_bf, wg_bf, wu_bf, wd_bf) = _attn_call(
        q_bf, k_bf, v_bf, jnp.pad(mk.astype(BF16), pad), jnp.pad(mv.astype(BF16), pad),
        *lam_args[:4], subln_w[li][:, None],
        weights=(w_out[li], w_gate[li], w_up[li], w_down[li]))
    fin_args = (wo_bf, ln_mix_post[li][None], ln_ffn_pre[li][None], ln_ffn_post[li][None],
                wg_bf, wu_bf, wd_bf)

    n_pool = cache_k.shape[1]
    ck = jnp.transpose(cache_k[li], (0, 2, 3, 1)).reshape(n_pool, ATTN_WIDTH, PAGE_SIZE)
    cv = cache_v[li].reshape(n_pool, PAGE_SIZE * N_HEADS, V_HEAD_DIM)
    qcol = jnp.broadcast_to(qs[:, :, None], (Bd, ATTN_WIDTH, PAGE_SIZE))
    y_prompt, attn_s = _finish_decode_call(
        page_table, attn_o.reshape(B * S, 512), conv_o.reshape(B * S, 512),
        x_prompt.reshape(B * S, D_MODEL), *fin_args,
        qs[:, None], qcol, ks[:, None], vs[:, None], *lam_args, ck, cv, tm=B * S // Bd)
    conv_s, new_state = _sample_conv_call(jnp.transpose(state_conv[li], (1, 0, 2)), us, *conv_args)
    y_sample = _finish_call(attn_s.reshape(Bd, 512), conv_s, x_sample.reshape(Bd, D_MODEL),
                            *fin_args, tm=Bd)

    return (y_prompt.reshape(B, S, D_MODEL),
            y_sample.reshape(Bd, 1, D_MODEL),
            new_k.reshape(1, B, T, N_MAPS, HEAD_DIM),
            new_v.reshape(1, B, T, N_HEADS, V_HEAD_DIM),
            new_conv[None],
            ks.reshape(1, Bd, 1, N_MAPS, HEAD_DIM),
            vs.reshape(1, Bd, 1, N_HEADS, V_HEAD_DIM),
            jnp.transpose(new_state, (1, 0, 2))[None])
```

```python
import functools
import math

import jax
import jax.numpy as jnp
from jax import lax
from jax.experimental import pallas as pl
from jax.experimental.pallas import tpu as pltpu

D_MODEL = 1024
N_META = 16
ATTN_WIDTH = 512
CONV_DIM = 512
N_HEADS = 4
HEAD_DIM = 64
V_HEAD_DIM = 128
N_MAPS = 2 * N_HEADS
CONV_WIDTH = 31
FFN_HIDDEN = 2816
PAGE_SIZE = 128
EPS = 1e-6
NEG_INF = -1e30
LAMBDA_INIT = 0.8 - 0.6 * math.exp(-0.3 * 0)
SCALE = HEAD_DIM ** -0.5
SLOPES = tuple(2.0 ** (-8.0 * (h + 1) / N_HEADS) for h in range(N_HEADS))

VMEM_LIMIT_BYTES = 56 * 1024 * 1024

F32 = jnp.float32
BF16 = jnp.bfloat16


def _rms(x, g):
    return x * lax.rsqrt(jnp.mean(x * x, axis=-1, keepdims=True) + EPS) * g


def _const_spec(shape):
    nd = len(shape)
    return pl.BlockSpec(shape, lambda *_: (0,) * nd, pipeline_mode=pl.Buffered(1))


def _project(x, g, w_ref):
    xn = _rms(x, g).astype(BF16)
    seg = lambda c: jnp.dot(xn, w_ref[:, c * 512:(c + 1) * 512], preferred_element_type=F32)
    q = seg(0) * SCALE
    k = seg(1)
    v = seg(2)
    u = seg(3) * jax.nn.sigmoid(seg(4))
    return q, k, v, u


def _proj_kernel(x_ref, g_ref, w_ref, q_ref, k_ref, v_ref, u_ref):
    q, k, v, u = _project(x_ref[...], g_ref[...], w_ref)
    q_ref[...] = q
    k_ref[...] = k
    v_ref[...] = v
    u_ref[...] = u


def _proj_call(x, g, w_bf):
    rows = x.shape[0]
    out = jax.ShapeDtypeStruct((rows, 512), F32)
    return pl.pallas_call(
        _proj_kernel,
        out_shape=(out, out, out, out),
        name="proj_small",
        compiler_params=pltpu.CompilerParams(vmem_limit_bytes=VMEM_LIMIT_BYTES),
    )(x, g, w_bf)


PROJ_TM = 512
U_PAD = 48
U_ROWS_OFF = U_PAD - (CONV_WIDTH - 1)


CONV_SUB = 128
PROJ_RB = 256


def _ln_swish(y, w, b):
    mu = jnp.mean(y, axis=-1, keepdims=True)
    yc = y - mu
    z = yc * lax.rsqrt(jnp.mean(yc * yc, axis=-1, keepdims=True) + EPS) * w + b
    return z * jax.nn.sigmoid(z)


def _zero_after(x):
    bits = pltpu.bitcast(x[0:8, 0:128], jnp.uint32)
    z = ((bits >> 16) >> 16).astype(F32)[0:1, :]
    return jnp.concatenate([z] * (CONV_DIM // 128), axis=1)


def _conv_rows(u_ref, win_start, cw_ref, bias):
    lead = U_ROWS_OFF % 8
    win = u_ref[pl.ds(win_start, CONV_SUB + 32), :]
    acc = jnp.broadcast_to(bias, (CONV_SUB, CONV_DIM))
    for res in range(8):
        rows = CONV_SUB + (8 if res else 0)
        part = None
        for w in range(CONV_WIDTH):
            if (lead + w) % 8 == res:
                off = lead + w - res
                term = win[off:off + rows, :].reshape(rows // 8, 8, CONV_DIM) * cw_ref[w]
                part = term if part is None else part + term
        acc = acc + part.reshape(rows, CONV_DIM)[res:res + CONV_SUB, :]
    return acc


def _store_value_rows(nv_ref, first_token, v):
    for h in range(N_HEADS):
        nv_ref[0, pl.ds(first_token * N_HEADS + h, v.shape[0], stride=N_HEADS), :] = (
            v[:, h * V_HEAD_DIM:(h + 1) * V_HEAD_DIM])


def _prompt_proj_kernel(x_ref, g_ref, w_ref, mk_ref, mv_ref, mu_ref,
                        cw_ref, cb_ref, lw_ref, lb_ref,
                        q_ref, kb_ref, vb_ref, nk_ref, nv_ref, co_ref, nc_ref,
                        u_ref, y_ref):
    i = pl.program_id(1)
    nt = pl.num_programs(1)

    @pl.when(i == 0)
    def _():
        nk_ref[0, 0:N_META, :] = mk_ref[...]
        _store_value_rows(nv_ref, 0, mv_ref[...])
        u_ref[0:U_PAD - N_META, :] = jnp.zeros((U_PAD - N_META, CONV_DIM), F32)
        u_ref[U_PAD - N_META:U_PAD, :] = mu_ref[...]

    xn = _rms(x_ref[0], g_ref[...]).astype(BF16)
    seg = lambda c: jnp.dot(xn, w_ref[:, c * 512:(c + 1) * 512], preferred_element_type=F32)
    r = i * PROJ_TM
    u_ref[pl.ds(pl.multiple_of(r + U_PAD, 8), PROJ_TM), :] = seg(3) * jax.nn.sigmoid(seg(4))

    win0 = r + U_ROWS_OFF - U_ROWS_OFF % 8
    n_sub = PROJ_TM // CONV_SUB

    def conv_passes(lo, hi, bias):
        for s in range(lo, hi):
            y_ref[s * CONV_SUB:(s + 1) * CONV_SUB, :] = _conv_rows(
                u_ref, pl.multiple_of(win0 + s * CONV_SUB, 8), cw_ref, bias)

    n_blk = PROJ_TM // PROJ_RB
    for rb in range(n_blk):
        rows = slice(rb * PROJ_RB, (rb + 1) * PROJ_RB)
        out_rows = pl.ds(pl.multiple_of(r + N_META + rb * PROJ_RB, 8), PROJ_RB)
        blk = lambda c: jnp.dot(xn[rows], w_ref[:, c * 512:(c + 1) * 512],
                                preferred_element_type=F32)
        q_ref[0, rows, :] = (blk(0) * SCALE).astype(BF16)
        k = blk(1)
        kb_ref[0, rows, :] = k.astype(BF16)
        nk_ref[0, out_rows, :] = k
        v = blk(2)
        vb_ref[0, rows, :] = v.astype(BF16)
        _store_value_rows(nv_ref, r + N_META + rb * PROJ_RB, v)
        conv_passes(rb * n_sub // n_blk, (rb + 1) * n_sub // n_blk, cb_ref[...] + _zero_after(v))
    co_ref[0] = _ln_swish(y_ref[...], lw_ref[...], lb_ref[...]).astype(BF16)

    @pl.when(i == nt - 1)
    def _():
        last = nt * PROJ_TM + U_PAD
        nc_ref[0] = u_ref[last - (CONV_WIDTH - 1):last, :]


def _prompt_proj_call(x, g, w_bf, mk, mv, mu, cw_tiles, cb, lw, lb):
    B, S, _ = x.shape
    T = S + N_META
    nt = S // PROJ_TM
    tile = pl.BlockSpec((1, PROJ_TM, 512), lambda b, i: (b, i, 0))
    full = lambda rows: pl.BlockSpec((1, rows, 512), lambda b, i: (b, 0, 0))
    bf = jax.ShapeDtypeStruct((B, S, 512), BF16)
    vec = _const_spec((1, CONV_DIM))
    return pl.pallas_call(
        _prompt_proj_kernel,
        grid=(B, nt),
        in_specs=[
            pl.BlockSpec((1, PROJ_TM, D_MODEL), lambda b, i: (b, i, 0)),
            _const_spec((1, D_MODEL)),
            _const_spec(w_bf.shape),
            _const_spec((N_META, 512)),
            _const_spec((N_META, 512)),
            _const_spec((N_META, 512)),
            _const_spec((CONV_WIDTH, 8, CONV_DIM)), vec, vec, vec,
        ],
        out_specs=(tile, tile, tile, full(T),
                   pl.BlockSpec((1, T * N_HEADS, V_HEAD_DIM), lambda b, i: (b, 0, 0)),
                   tile, full(CONV_WIDTH - 1)),
        out_shape=(bf, bf, bf,
                   jax.ShapeDtypeStruct((B, T, 512), F32),
                   jax.ShapeDtypeStruct((B, T * N_HEADS, V_HEAD_DIM), F32),
                   bf,
                   jax.ShapeDtypeStruct((B, CONV_WIDTH - 1, CONV_DIM), F32)),
        scratch_shapes=[pltpu.VMEM((S + U_PAD, CONV_DIM), F32),
                        pltpu.VMEM((PROJ_TM, CONV_DIM), F32)],
        name="prompt_proj",
        compiler_params=pltpu.CompilerParams(
            dimension_semantics=("arbitrary", "arbitrary"),
            vmem_limit_bytes=VMEM_LIMIT_BYTES),
    )(x, g, w_bf, mk, mv, mu, cw_tiles, cb, lw, lb)


def _sample_conv_kernel(st_ref, u_ref, cw_ref, cb_ref, lw_ref, lb_ref, o_ref, ns_ref):
    u = u_ref[...]
    acc = cb_ref[...] + u * cw_ref[CONV_WIDTH - 1:CONV_WIDTH, :]
    for w in range(CONV_WIDTH - 1):
        acc = acc + st_ref[w] * cw_ref[w:w + 1, :]
    o_ref[...] = _ln_swish(acc, lw_ref[...], lb_ref[...]).astype(BF16)
    ns_ref[0:CONV_WIDTH - 2] = st_ref[1:CONV_WIDTH - 1]
    ns_ref[CONV_WIDTH - 2] = u


def _sample_conv_call(state_wbc, u, cw, cb, lw, lb):
    return pl.pallas_call(
        _sample_conv_kernel,
        out_shape=(jax.ShapeDtypeStruct(u.shape, BF16),
                   jax.ShapeDtypeStruct(state_wbc.shape, F32)),
        name="sample_conv",
    )(state_wbc, u, cw, cb, lw, lb)


def _lambda(lq1, lk1, lq2, lk2):
    s1 = jnp.sum(lq1 * lk1, axis=-1, keepdims=True)
    s2 = jnp.sum(lq2 * lk2, axis=-1, keepdims=True)
    return jnp.exp(s1) - jnp.exp(s2) + LAMBDA_INIT


def _head_out(o0, o1, lam, subln):
    o = o0 - lam * o1
    return _rms(o, subln) * (1.0 - LAMBDA_INIT)


ATT_TQ = 256
ATT_TK = 256
ATT_TK_WIDE = 512
META_BLK = 128
QK_AHEAD = 6
VT_ROWS = V_HEAD_DIM + 16


def _augment(x, pos, is_query, slope):
    lane = lax.broadcasted_iota(jnp.int32, x.shape, 1)
    hi = (pos >> 6).astype(F32)
    lo = (pos & 63).astype(F32)
    out = []
    for j in range(2):
        a = lane - (64 if j == 0 else 0)
        if is_query:
            aug = jnp.where(a == 0, hi * (-64.0 * slope),
                  jnp.where(a == 1, lo * (-slope),
                  jnp.where(a == 2, 64.0 * slope,
                  jnp.where(a == 3, slope, 0.0))))
        else:
            aug = jnp.where(a == 0, 1.0,
                  jnp.where(a == 1, 1.0,
                  jnp.where(a == 2, hi,
                  jnp.where(a == 3, lo, 0.0))))
        own = (lane < 64) if j == 0 else (lane >= 64)
        out.append(jnp.where(own, x, aug).astype(BF16))
    return out


N_CAST = 4


def _attn_kernel(q_ref, kb_ref, vb_ref, mkb_ref, mvb_ref,
                 lq1_ref, lk1_ref, lq2_ref, lk2_ref, subcol_ref, *rest):
    w_f32 = rest[:N_CAST]
    o_ref = rest[N_CAST]
    w_bf16 = rest[N_CAST + 1:2 * N_CAST + 1]
    kexp_ref, vt_ref, qexp_ref, m_ref, acc_ref = rest[2 * N_CAST + 1:]
    qi = pl.program_id(1)
    S = kb_ref.shape[1]

    for src, dst in zip(w_f32, w_bf16):
        dst[...] = src[...].astype(BF16)

    def chunk_step(parts, first):
        def scores(mm):
            mcols = slice(mm * 128, (mm + 1) * 128)
            out = []
            for k_rows, _, mask in parts:
                s = lax.dot_general(kexp_ref[k_rows, mcols], qexp_ref[:, mcols],
                                    (((1,), (1,)), ((), ())), preferred_element_type=F32)
                out.append(s if mask is None else jnp.where(mask, s, NEG_INF))
            return out

        pending = [scores(mm) for mm in range(QK_AHEAD)]
        for mm in range(N_MAPS):
            h = mm // 2
            if mm + QK_AHEAD < N_MAPS:
                pending.append(scores(mm + QK_AHEAD))
            m_new = None if first else m_ref[mm]
            for s in pending[mm]:
                m_cur = jnp.max(s, axis=0, keepdims=True)
                m_new = m_cur if m_new is None else jnp.maximum(m_new, m_cur)
            pv = None
            for s, (_, v_cols, _) in zip(pending[mm], parts):
                vt = vt_ref[h * VT_ROWS:(h + 1) * VT_ROWS, v_cols]
                pb = jnp.exp(s - m_new).astype(BF16)
                if pb.shape[0] < vt.shape[1]:
                    pb = jnp.concatenate(
                        [pb, jnp.zeros((vt.shape[1] - pb.shape[0], pb.shape[1]), BF16)], axis=0)
                term = jnp.dot(vt, pb, preferred_element_type=F32)
                pv = term if pv is None else pv + term
            if first:
                acc_ref[mm] = pv
            else:
                acc_ref[mm] = jnp.exp(m_ref[mm] - m_new) * acc_ref[mm] + pv
            m_ref[mm] = m_new

    @pl.when(qi == 0)
    def _():
        ones_rows = (lax.broadcasted_iota(jnp.int32, (VT_ROWS - V_HEAD_DIM, ATT_TK), 0) == 0)

        def put_vt(cols_off, vc):
            n = vc.shape[0]
            for h in range(N_HEADS):
                vt_ref[h * VT_ROWS:h * VT_ROWS + V_HEAD_DIM, cols_off:cols_off + n] = (
                    vc[:, h * 128:(h + 1) * 128].astype(F32).T.astype(BF16))
                vt_ref[h * VT_ROWS + V_HEAD_DIM:(h + 1) * VT_ROWS, cols_off:cols_off + n] = (
                    ones_rows[:, :n].astype(F32).astype(BF16))

        put_vt(0, mvb_ref[...])
        for c in range(S // ATT_TK):
            put_vt(META_BLK + c * ATT_TK, vb_ref[0, c * ATT_TK:(c + 1) * ATT_TK, :])

        def put(rows_off, kc, pos0):
            rows = kc.shape[0]
            pos = pos0 + lax.broadcasted_iota(jnp.int32, (rows, 128), 0)
            for h in range(N_HEADS):
                e0, e1 = _augment(kc[:, h * 128:(h + 1) * 128].astype(F32), pos, False, 0.0)
                kexp_ref[rows_off:rows_off + rows, (2 * h) * 128:(2 * h + 1) * 128] = e0
                kexp_ref[rows_off:rows_off + rows, (2 * h + 1) * 128:(2 * h + 2) * 128] = e1

        put(0, mkb_ref[...], 0)
        for c in range(S // ATT_TK):
            put(META_BLK + c * ATT_TK, kb_ref[0, c * ATT_TK:(c + 1) * ATT_TK, :],
                N_META + c * ATT_TK)

    qpos = N_META + qi * ATT_TQ + lax.broadcasted_iota(jnp.int32, (ATT_TQ, 128), 0)
    q = q_ref[0].astype(F32)
    for h in range(N_HEADS):
        e0, e1 = _augment(q[:, h * 128:(h + 1) * 128], qpos, True, SLOPES[h])
        qexp_ref[:, (2 * h) * 128:(2 * h + 1) * 128] = e0
        qexp_ref[:, (2 * h + 1) * 128:(2 * h + 2) * 128] = e1

    lam = _lambda(lq1_ref[...], lk1_ref[...], lq2_ref[...], lk2_ref[...])
    key = lax.broadcasted_iota(jnp.int32, (ATT_TK, ATT_TQ), 0)
    qry = lax.broadcasted_iota(jnp.int32, (ATT_TK, ATT_TQ), 1)
    causal = key <= qry

    def real_part(first_key, width, mask):
        r = pl.multiple_of(META_BLK + first_key, 128)
        return (pl.ds(r, width), pl.ds(r, width), mask)

    chunk_step([(slice(0, N_META), slice(0, META_BLK), None),
                real_part(qi * ATT_TK, ATT_TK, causal)], True)

    def body(c, carry):
        chunk_step([real_part(c * ATT_TK_WIDE, ATT_TK_WIDE, None)], False)
        return carry

    n_wide = lax.div(qi * ATT_TK, ATT_TK_WIDE)
    lax.fori_loop(0, n_wide, body, 0)

    @pl.when(n_wide * ATT_TK_WIDE < qi * ATT_TK)
    def _():
        chunk_step([real_part(n_wide * ATT_TK_WIDE, ATT_TK, None)], False)

    for h in range(N_HEADS):
        a0, a1 = acc_ref[2 * h], acc_ref[2 * h + 1]
        o0 = a0[:V_HEAD_DIM] / a0[V_HEAD_DIM:V_HEAD_DIM + 1]
        o1 = a1[:V_HEAD_DIM] / a1[V_HEAD_DIM:V_HEAD_DIM + 1]
        o = o0 - lam * o1
        o = o * lax.rsqrt(jnp.mean(o * o, axis=0, keepdims=True) + EPS)
        o = o * subcol_ref[...] * (1.0 - LAMBDA_INIT)
        o_ref[0, :, h * 128:(h + 1) * 128] = o.T.astype(BF16)


def _attn_call(q, kb, vb, mkb, mvb, lq1, lk1, lq2, lk2, subln, weights):
    B, S, _ = q.shape
    nq = S // ATT_TQ
    steps = B * nq
    assert len(weights) == N_CAST
    per_b = pl.BlockSpec((1, S, 512), lambda b, i: (b, 0, 0))
    tile = pl.BlockSpec((1, ATT_TQ, 512), lambda b, i: (b, i, 0))
    small = _const_spec((1, HEAD_DIM))

    def slab_spec(w):
        rows = next(r for r in range(16, w.shape[0] + 1, 16)
                    if w.shape[0] % r == 0 and w.shape[0] // r <= steps)
        last = w.shape[0] // rows - 1
        return pl.BlockSpec((rows, w.shape[1]), lambda b, i: (jnp.minimum(b * nq + i, last), 0))

    slab_specs = [slab_spec(w) for w in weights]
    out = pl.pallas_call(
        _attn_kernel,
        grid=(B, nq),
        in_specs=[tile, per_b, per_b,
                  _const_spec((META_BLK, 512)), _const_spec((META_BLK, 512)),
                  small, small, small, small, _const_spec((V_HEAD_DIM, 1))] + slab_specs,
        out_specs=[tile] + slab_specs,
        out_shape=[jax.ShapeDtypeStruct((B, S, 512), BF16)]
                  + [jax.ShapeDtypeStruct(w.shape, BF16) for w in weights],
        scratch_shapes=[pltpu.VMEM((META_BLK + S, N_MAPS * 128), BF16),
                        pltpu.VMEM((N_HEADS * VT_ROWS, META_BLK + S), BF16),
                        pltpu.VMEM((ATT_TQ, N_MAPS * 128), BF16),
                        pltpu.VMEM((N_MAPS, 1, ATT_TQ), F32),
                        pltpu.VMEM((N_MAPS, VT_ROWS, ATT_TQ), F32)],
        name="attn",
        compiler_params=pltpu.CompilerParams(
            dimension_semantics=("arbitrary", "arbitrary"),
            vmem_limit_bytes=VMEM_LIMIT_BYTES),
    )(q, kb, vb, mkb, mvb, lq1, lk1, lq2, lk2, subln, *weights)
    return out[0], out[1:]


KEYS_PER_VBLK = 128 // N_HEADS


def _decode_consts():
    sub = lax.broadcasted_iota(jnp.int32, (N_MAPS, PAGE_SIZE), 0)
    lane = lax.broadcasted_iota(jnp.int32, (N_MAPS, PAGE_SIZE), 1)
    head = sub >> 1
    slope = jnp.where(head == 0, SLOPES[0],
            jnp.where(head == 1, SLOPES[1],
            jnp.where(head == 2, SLOPES[2], SLOPES[3])))
    return dict(lane=lane, head=head, slope=slope,
                own_head=(lane & (N_HEADS - 1)) == head, key_of_row=lane >> 2)


def _decode_init(q_row, k_new, v_new, head):
    sub_w = lax.broadcasted_iota(jnp.int32, (N_MAPS, ATTN_WIDTH), 0)
    lane_w = lax.broadcasted_iota(jnp.int32, (N_MAPS, ATTN_WIDTH), 1)
    qbd = jnp.where((lane_w >> 6) == sub_w, jnp.broadcast_to(q_row, (N_MAPS, ATTN_WIDTH)), 0.0)
    m = jnp.sum(qbd * k_new, axis=-1, keepdims=True)
    acc = jnp.zeros((N_MAPS, V_HEAD_DIM), F32)
    for h in range(N_HEADS):
        acc = jnp.where(head == h, v_new[:, h * 128:(h + 1) * 128], acc)
    return m, jnp.ones((N_MAPS, 1), F32), acc


def _decode_probs(m, l, qcol, k_pages, first_page, past, cst):
    partial = [[None] * N_MAPS for _ in k_pages]
    for mm in range(N_MAPS):
        for blk in range(HEAD_DIM // 8):
            rows = slice(mm * HEAD_DIM + blk * 8, mm * HEAD_DIM + blk * 8 + 8)
            qv = qcol[rows, :]
            for p, kp in enumerate(k_pages):
                t = kp[rows, :] * qv
                partial[p][mm] = t if partial[p][mm] is None else partial[p][mm] + t
    scores = []
    for p in range(len(k_pages)):
        s = jnp.sum(jnp.stack(partial[p]), axis=1)
        dist = past - ((first_page + p) * PAGE_SIZE + cst["lane"])
        scores.append(s - cst["slope"] * dist.astype(F32))
    s = jnp.concatenate(scores, axis=1)
    m_new = jnp.maximum(m, jnp.max(s, axis=-1, keepdims=True))
    alpha = jnp.exp(m - m_new)
    pr = jnp.exp(s - m_new)
    l = alpha * l + jnp.sum(pr, axis=-1, keepdims=True)
    prb = pr.astype(BF16).astype(F32)
    spread_probs = []
    for p in range(len(k_pages)):
        prp = prb[:, p * PAGE_SIZE:(p + 1) * PAGE_SIZE]
        spread = [jnp.where(cst["own_head"],
                            jnp.take_along_axis(prp, cst["key_of_row"] + c * KEYS_PER_VBLK, axis=1),
                            0.0)
                  for c in range(N_HEADS)]
        spread_probs.append(jnp.concatenate(spread, axis=1).astype(BF16))
    return m_new, l, alpha, spread_probs


def _decode_values(acc, alpha, spread_probs, v_pages):
    pv = jnp.zeros((N_MAPS, V_HEAD_DIM), F32)
    for pe, vp in zip(spread_probs, v_pages):
        pv = pv + jnp.dot(pe, vp.astype(BF16), preferred_element_type=F32)
    return alpha * acc + pv


def _decode_out(l, acc, lam, subln, o_ref):
    o = acc / l
    for h in range(N_HEADS):
        ho = _head_out(o[2 * h:2 * h + 1], o[2 * h + 1:2 * h + 2], lam, subln)
        o_ref[0, :, h * 128:(h + 1) * 128] = ho.astype(BF16)


FFN_CHUNK = 256


N_FFN_CHUNKS = FFN_HIDDEN // FFN_CHUNK


def _finish_rows(a_ref, c_ref, x_ref, wo_ref, gpost_ref, gpre_ref, gfpost_ref,
                 wg_ref, wu_ref, wd_ref, o_ref, before_chunk=None, after_chunk=None):
    mix = (jnp.dot(a_ref[...], wo_ref[0:ATTN_WIDTH, :], preferred_element_type=F32)
           + jnp.dot(c_ref[...], wo_ref[ATTN_WIDTH:, :], preferred_element_type=F32))
    x = x_ref[...] + _rms(mix, gpost_ref[...])
    hb = _rms(x, gpre_ref[...]).astype(BF16)
    f = jnp.zeros(x.shape, F32)
    for c in range(N_FFN_CHUNKS):
        cols = slice(c * FFN_CHUNK, (c + 1) * FFN_CHUNK)
        if before_chunk is not None:
            before_chunk(c)
        gate = jnp.dot(hb, wg_ref[:, cols], preferred_element_type=F32)
        up = jnp.dot(hb, wu_ref[:, cols], preferred_element_type=F32)
        act = (gate * jax.nn.sigmoid(gate) * up).astype(BF16)
        f = f + jnp.dot(act, wd_ref[cols, :], preferred_element_type=F32)
        if after_chunk is not None:
            after_chunk(c)
    o_ref[...] = x + _rms(f, gfpost_ref[...])


def _finish_kernel(*refs):
    _finish_rows(*refs)


DEC_GROUP = 8
DEC_AHEAD = 2
DEC_SLOTS = DEC_AHEAD + 1


def _finish_decode_kernel(pt_ref, a_ref, c_ref, x_ref, wo_ref, gpost_ref, gpre_ref, gfpost_ref,
                          wg_ref, wu_ref, wd_ref,
                          q_ref, qcol_ref, kn_ref, vn_ref,
                          lq1_ref, lk1_ref, lq2_ref, lk2_ref, sub_ref, ck_hbm, cv_hbm,
                          o_ref, os_ref, kbuf, vbuf, sem):
    i = pl.program_id(0)
    n_steps = pl.num_programs(0)
    n_pages = pt_ref.shape[1]
    n_groups = n_pages // DEC_GROUP
    past = n_pages * PAGE_SIZE

    def group_copies(b, g, slot):
        copies = []
        for p in range(DEC_GROUP):
            page = pt_ref[b, g * DEC_GROUP + p]
            copies.append(pltpu.make_async_copy(ck_hbm.at[page], kbuf.at[slot, p], sem.at[0, slot]))
            copies.append(pltpu.make_async_copy(cv_hbm.at[page], vbuf.at[slot, p], sem.at[1, slot]))
        return copies

    def wait_group(slot):
        pltpu.make_async_copy(ck_hbm.at[pl.ds(0, DEC_GROUP)], kbuf.at[slot], sem.at[0, slot]).wait()
        pltpu.make_async_copy(cv_hbm.at[pl.ds(0, DEC_GROUP)], vbuf.at[slot], sem.at[1, slot]).wait()

    def slot_of(g):
        return lax.rem(i * n_groups + g, DEC_SLOTS)

    def start_group(g):
        if g < n_groups:
            for cp in group_copies(i, g, slot_of(g)):
                cp.start()
        else:
            @pl.when(i + 1 < n_steps)
            def _():
                for cp in group_copies(i + 1, g - n_groups, slot_of(g)):
                    cp.start()

    @pl.when(i == 0)
    def _():
        for g in range(DEC_AHEAD):
            start_group(g)

    cst = _decode_consts()
    m, l, acc = _decode_init(q_ref[0], kn_ref[0], vn_ref[0], cst["head"])
    st = dict(m=m, l=l, acc=acc)

    def before_chunk(g):
        if g >= n_groups:
            return
        start_group(g + DEC_AHEAD)
        slot = slot_of(g)
        wait_group(slot)
        st["m"], st["l"], st["alpha"], st["probs"] = _decode_probs(
            st["m"], st["l"], qcol_ref.at[0], [kbuf.at[slot, p] for p in range(DEC_GROUP)],
            g * DEC_GROUP, past, cst)

    def after_chunk(g):
        if g >= n_groups:
            return
        slot = slot_of(g)
        st["acc"] = _decode_values(st["acc"], st["alpha"], st["probs"],
                                   [vbuf[slot, p] for p in range(DEC_GROUP)])

    _finish_rows(a_ref, c_ref, x_ref, wo_ref, gpost_ref, gpre_ref, gfpost_ref,
                 wg_ref, wu_ref, wd_ref, o_ref, before_chunk, after_chunk)
    lam = _lambda(lq1_ref[...], lk1_ref[...], lq2_ref[...], lk2_ref[...])
    _decode_out(st["l"], st["acc"], lam, sub_ref[...], os_ref)


def _finish_decode_call(page_table, a, c, x, wo, gpost, gpre, gfpost, wg, wu, wd,
                        q, qcol, kn, vn, lq1, lk1, lq2, lk2, subln, ck, cv, tm):
    rows = x.shape[0]
    Bd, n_pages = page_table.shape
    assert rows // tm == Bd and n_pages % DEC_GROUP == 0
    assert DEC_AHEAD <= n_pages // DEC_GROUP <= N_FFN_CHUNKS
    half = pl.BlockSpec((tm, 512), lambda i, pt: (i, 0))
    full = pl.BlockSpec((tm, D_MODEL), lambda i, pt: (i, 0))
    vec = _const_spec((1, D_MODEL))
    row = pl.BlockSpec((1, 1, ATTN_WIDTH), lambda i, pt: (i, 0, 0))
    small = _const_spec((1, HEAD_DIM))
    hbm = pl.BlockSpec(memory_space=pl.ANY)
    grid_spec = pltpu.PrefetchScalarGridSpec(
        num_scalar_prefetch=1,
        grid=(Bd,),
        in_specs=[half, half, full, _const_spec(wo.shape), vec, vec, vec,
                  _const_spec(wg.shape), _const_spec(wu.shape), _const_spec(wd.shape),
                  row, pl.BlockSpec((1, ATTN_WIDTH, PAGE_SIZE), lambda i, pt: (i, 0, 0)),
                  row, row, small, small, small, small, _const_spec((1, V_HEAD_DIM)),
                  hbm, hbm],
        out_specs=(full, row),
        scratch_shapes=[pltpu.VMEM((DEC_SLOTS, DEC_GROUP, ATTN_WIDTH, PAGE_SIZE), F32),
                        pltpu.VMEM((DEC_SLOTS, DEC_GROUP, ATTN_WIDTH, PAGE_SIZE), F32),
                        pltpu.SemaphoreType.DMA((2, DEC_SLOTS))],
    )
    return pl.pallas_call(
        _finish_decode_kernel,
        grid_spec=grid_spec,
        out_shape=(jax.ShapeDtypeStruct((rows, D_MODEL), F32),
                   jax.ShapeDtypeStruct((Bd, 1, ATTN_WIDTH), BF16)),
        name="finish_decode",
        compiler_params=pltpu.CompilerParams(
            dimension_semantics=("arbitrary",),
            vmem_limit_bytes=VMEM_LIMIT_BYTES),
    )(page_table, a, c, x, wo, gpost, gpre, gfpost, wg, wu, wd,
      q, qcol, kn, vn, lq1, lk1, lq2, lk2, subln, ck, cv)


def _finish_call(a, c, x, wo, gpost, gpre, gfpost, wg, wu, wd, tm):
    rows = x.shape[0]
    half = pl.BlockSpec((tm, 512), lambda i: (i, 0))
    full = pl.BlockSpec((tm, D_MODEL), lambda i: (i, 0))
    vec = _const_spec((1, D_MODEL))
    return pl.pallas_call(
        _finish_kernel,
        grid=(rows // tm,),
        in_specs=[half, half, full, _const_spec(wo.shape), vec, vec, vec,
                  _const_spec(wg.shape), _const_spec(wu.shape), _const_spec(wd.shape)],
        out_specs=full,
        out_shape=jax.ShapeDtypeStruct((rows, D_MODEL), F32),
        name="finish",
        compiler_params=pltpu.CompilerParams(
            dimension_semantics=("arbitrary",),
            vmem_limit_bytes=VMEM_LIMIT_BYTES),
    )(a, c, x, wo, gpost, gpre, gfpost, wg, wu, wd)


def kernel(x_prompt, x_sample, cache_k, cache_v, state_conv, page_table, meta_tokens,
           ln_mix_pre, ln_mix_post, w_in, lambda_q1, lambda_k1, lambda_q2, lambda_k2,
           subln_w, conv_w, conv_b, conv_ln_w, conv_ln_b, w_out, ln_ffn_pre, ln_ffn_post,
           w_gate, w_up, w_down):
    B, S, _ = x_prompt.shape
    Bd = x_sample.shape[0]
    T = S + N_META
    li = 0
    w_in_bf = w_in[li].astype(BF16)
    g_pre = ln_mix_pre[li][None]
    lam_args = (lambda_q1[li][None], lambda_k1[li][None], lambda_q2[li][None],
                lambda_k2[li][None], subln_w[li][None])
    conv_args = (conv_w[li], conv_b[li][None], conv_ln_w[li][None], conv_ln_b[li][None])

    small = _proj_call(jnp.concatenate([meta_tokens, x_sample.reshape(Bd, D_MODEL)], axis=0),
                       g_pre, w_in_bf)
    _, mk, mv, mu = (t[:N_META] for t in small)
    qs, ks, vs, us = (t[N_META:] for t in small)
    cw_tiles = jnp.broadcast_to(conv_w[li][:, None, :], (CONV_WIDTH, 8, CONV_DIM))
    q_bf, k_bf, v_bf, new_k, new_v, conv_o, new_conv = _prompt_proj_call(
        x_prompt, g_pre, w_in_bf, mk, mv, mu, cw_tiles, *conv_args[1:])
    pad = ((0, META_BLK - N_META), (0, 0))
    attn_o, (wo_bf, wg_bf, wu_bf, wd_bf) = _attn_call(
        q_bf, k_bf, v_bf, jnp.pad(mk.astype(BF16), pad), jnp.pad(mv.astype(BF16), pad),
        *lam_args[:4], subln_w[li][:, None],
        weights=(w_out[li], w_gate[li], w_up[li], w_down[li]))
    fin_args = (wo_bf, ln_mix_post[li][None], ln_ffn_pre[li][None], ln_ffn_post[li][None],
                wg_bf, wu_bf, wd_bf)

    n_pool = cache_k.shape[1]
    ck = jnp.transpose(cache_k[li], (0, 2, 3, 1)).reshape(n_pool, ATTN_WIDTH, PAGE_SIZE)
    cv = cache_v[li].reshape(n_pool, PAGE_SIZE * N_HEADS, V_HEAD_DIM)
    qcol = jnp.broadcast_to(qs[:, :, None], (Bd, ATTN_WIDTH, PAGE_SIZE))
    y_prompt, attn_s = _finish_decode_call(
        page_table, attn_o.reshape(B * S, 512), conv_o.reshape(B * S, 512),
        x_prompt.reshape(B * S, D_MODEL), *fin_args,
        qs[:, None], qcol, ks[:, None], vs[:, None], *lam_args, ck, cv, tm=B * S // Bd)
    conv_s, new_state = _sample_conv_call(jnp.transpose(state_conv[li], (1, 0, 2)), us, *conv_args)
    y_sample = _finish_call(attn_s.reshape(Bd, 512), conv_s, x_sample.reshape(Bd, D_MODEL),
                            *fin_args, tm=Bd)

    return (y_prompt.reshape(B, S, D_MODEL),
            y_sample.reshape(Bd, 1, D_MODEL),
            new_k.reshape(1, B, T, N_MAPS, HEAD_DIM),
            new_v.reshape(1, B, T, N_HEADS, V_HEAD_DIM),
            new_conv[None],
            ks.reshape(1, Bd, 1, N_MAPS, HEAD_DIM),
            vs.reshape(1, Bd, 1, N_HEADS, V_HEAD_DIM),
            jnp.transpose(new_state, (1, 0, 2))[None])
```

```python
import functools
import math

import jax
import jax.numpy as jnp
from jax import lax
from jax.experimental import pallas as pl
from jax.experimental.pallas import tpu as pltpu

D_MODEL = 1024
N_META = 16
ATTN_WIDTH = 512
CONV_DIM = 512
N_HEADS = 4
HEAD_DIM = 64
V_HEAD_DIM = 128
N_MAPS = 2 * N_HEADS
CONV_WIDTH = 31
FFN_HIDDEN = 2816
PAGE_SIZE = 128
EPS = 1e-6
NEG_INF = -1e30
LAMBDA_INIT = 0.8 - 0.6 * math.exp(-0.3 * 0)
SCALE = HEAD_DIM ** -0.5
SLOPES = tuple(2.0 ** (-8.0 * (h + 1) / N_HEADS) for h in range(N_HEADS))

VMEM_LIMIT_BYTES = 56 * 1024 * 1024

F32 = jnp.float32
BF16 = jnp.bfloat16


def _rms(x, g):
    return x * lax.rsqrt(jnp.mean(x * x, axis=-1, keepdims=True) + EPS) * g


def _const_spec(shape):
    nd = len(shape)
    return pl.BlockSpec(shape, lambda *_: (0,) * nd, pipeline_mode=pl.Buffered(1))


def _project(x, g, w_ref):
    xn = _rms(x, g).astype(BF16)
    seg = lambda c: jnp.dot(xn, w_ref[:, c * 512:(c + 1) * 512], preferred_element_type=F32)
    q = seg(0) * SCALE
    k = seg(1)
    v = seg(2)
    u = seg(3) * jax.nn.sigmoid(seg(4))
    return q, k, v, u


def _proj_kernel(x_ref, g_ref, w_ref, q_ref, k_ref, v_ref, u_ref):
    q, k, v, u = _project(x_ref[...], g_ref[...], w_ref)
    q_ref[...] = q
    k_ref[...] = k
    v_ref[...] = v
    u_ref[...] = u


def _proj_call(x, g, w_bf):
    rows = x.shape[0]
    out = jax.ShapeDtypeStruct((rows, 512), F32)
    return pl.pallas_call(
        _proj_kernel,
        out_shape=(out, out, out, out),
        name="proj_small",
        compiler_params=pltpu.CompilerParams(vmem_limit_bytes=VMEM_LIMIT_BYTES),
    )(x, g, w_bf)


PROJ_TM = 512
U_PAD = 48
U_ROWS_OFF = U_PAD - (CONV_WIDTH - 1)


CONV_SUB = 128
PROJ_RB = 256


def _ln_swish(y, w, b):
    mu = jnp.mean(y, axis=-1, keepdims=True)
    yc = y - mu
    z = yc * lax.rsqrt(jnp.mean(yc * yc, axis=-1, keepdims=True) + EPS) * w + b
    return z * jax.nn.sigmoid(z)


def _zero_after(x):
    bits = pltpu.bitcast(x[0:8, 0:128], jnp.uint32)
    z = ((bits >> 16) >> 16).astype(F32)[0:1, :]
    return jnp.concatenate([z] * (CONV_DIM // 128), axis=1)


def _conv_rows(u_ref, win_start, cw_ref, bias):
    lead = U_ROWS_OFF % 8
    win = u_ref[pl.ds(win_start, CONV_SUB + 32), :]
    acc = jnp.broadcast_to(bias, (CONV_SUB, CONV_DIM))
    for res in range(8):
        rows = CONV_SUB + (8 if res else 0)
        part = None
        for w in range(CONV_WIDTH):
            if (lead + w) % 8 == res:
                off = lead + w - res
                term = win[off:off + rows, :].reshape(rows // 8, 8, CONV_DIM) * cw_ref[w]
                part = term if part is None else part + term
        acc = acc + part.reshape(rows, CONV_DIM)[res:res + CONV_SUB, :]
    return acc


def _store_value_rows(nv_ref, first_token, v):
    for h in range(N_HEADS):
        nv_ref[0, pl.ds(first_token * N_HEADS + h, v.shape[0], stride=N_HEADS), :] = (
            v[:, h * V_HEAD_DIM:(h + 1) * V_HEAD_DIM])


def _prompt_proj_kernel(x_ref, g_ref, w_ref, mk_ref, mv_ref, mu_ref,
                        cw_ref, cb_ref, lw_ref, lb_ref,
                        q_ref, kb_ref, vb_ref, nk_ref, nv_ref, co_ref, nc_ref,
                        u_ref, y_ref):
    i = pl.program_id(1)
    nt = pl.num_programs(1)

    @pl.when(i == 0)
    def _():
        nk_ref[0, 0:N_META, :] = mk_ref[...]
        _store_value_rows(nv_ref, 0, mv_ref[...])
        u_ref[0:U_PAD - N_META, :] = jnp.zeros((U_PAD - N_META, CONV_DIM), F32)
        u_ref[U_PAD - N_META:U_PAD, :] = mu_ref[...]

    xn = _rms(x_ref[0], g_ref[...]).astype(BF16)
    seg = lambda c: jnp.dot(xn, w_ref[:, c * 512:(c + 1) * 512], preferred_element_type=F32)
    r = i * PROJ_TM
    u_ref[pl.ds(pl.multiple_of(r + U_PAD, 8), PROJ_TM), :] = seg(3) * jax.nn.sigmoid(seg(4))

    win0 = r + U_ROWS_OFF - U_ROWS_OFF % 8
    n_sub = PROJ_TM // CONV_SUB

    def conv_passes(lo, hi, bias):
        for s in range(lo, hi):
            y_ref[s * CONV_SUB:(s + 1) * CONV_SUB, :] = _conv_rows(
                u_ref, pl.multiple_of(win0 + s * CONV_SUB, 8), cw_ref, bias)

    n_blk = PROJ_TM // PROJ_RB
    for rb in range(n_blk):
        rows = slice(rb * PROJ_RB, (rb + 1) * PROJ_RB)
        out_rows = pl.ds(pl.multiple_of(r + N_META + rb * PROJ_RB, 8), PROJ_RB)
        blk = lambda c: jnp.dot(xn[rows], w_ref[:, c * 512:(c + 1) * 512],
                                preferred_element_type=F32)
        q_ref[0, rows, :] = (blk(0) * SCALE).astype(BF16)
        k = blk(1)
        kb_ref[0, rows, :] = k.astype(BF16)
        nk_ref[0, out_rows, :] = k
        v = blk(2)
        vb_ref[0, rows, :] = v.astype(BF16)
        _store_value_rows(nv_ref, r + N_META + rb * PROJ_RB, v)
        conv_passes(rb * n_sub // n_blk, (rb + 1) * n_sub // n_blk, cb_ref[...] + _zero_after(v))
    co_ref[0] = _ln_swish(y_ref[...], lw_ref[...], lb_ref[...]).astype(BF16)

    @pl.when(i == nt - 1)
    def _():
        last = nt * PROJ_TM + U_PAD
        nc_ref[0] = u_ref[last - (CONV_WIDTH - 1):last, :]


def _prompt_proj_call(x, g, w_bf, mk, mv, mu, cw_tiles, cb, lw, lb):
    B, S, _ = x.shape
    T = S + N_META
    nt = S // PROJ_TM
    tile = pl.BlockSpec((1, PROJ_TM, 512), lambda b, i: (b, i, 0))
    full = lambda rows: pl.BlockSpec((1, rows, 512), lambda b, i: (b, 0, 0))
    bf = jax.ShapeDtypeStruct((B, S, 512), BF16)
    vec = _const_spec((1, CONV_DIM))
    return pl.pallas_call(
        _prompt_proj_kernel,
        grid=(B, nt),
        in_specs=[
            pl.BlockSpec((1, PROJ_TM, D_MODEL), lambda b, i: (b, i, 0)),
            _const_spec((1, D_MODEL)),
            _const_spec(w_bf.shape),
            _const_spec((N_META, 512)),
            _const_spec((N_META, 512)),
            _const_spec((N_META, 512)),
            _const_spec((CONV_WIDTH, 8, CONV_DIM)), vec, vec, vec,
        ],
        out_specs=(tile, tile, tile, full(T),
                   pl.BlockSpec((1, T * N_HEADS, V_HEAD_DIM), lambda b, i: (b, 0, 0)),
                   tile, full(CONV_WIDTH - 1)),
        out_shape=(bf, bf, bf,
                   jax.ShapeDtypeStruct((B, T, 512), F32),
                   jax.ShapeDtypeStruct((B, T * N_HEADS, V_HEAD_DIM), F32),
                   bf,
                   jax.ShapeDtypeStruct((B, CONV_WIDTH - 1, CONV_DIM), F32)),
        scratch_shapes=[pltpu.VMEM((S + U_PAD, CONV_DIM), F32),
                        pltpu.VMEM((PROJ_TM, CONV_DIM), F32)],
        name="prompt_proj",
        compiler_params=pltpu.CompilerParams(
            dimension_semantics=("arbitrary", "arbitrary"),
            vmem_limit_bytes=VMEM_LIMIT_BYTES),
    )(x, g, w_bf, mk, mv, mu, cw_tiles, cb, lw, lb)


def _sample_conv_kernel(st_ref, u_ref, cw_ref, cb_ref, lw_ref, lb_ref, o_ref, ns_ref):
    u = u_ref[...]
    acc = cb_ref[...] + u * cw_ref[CONV_WIDTH - 1:CONV_WIDTH, :]
    for w in range(CONV_WIDTH - 1):
        acc = acc + st_ref[w] * cw_ref[w:w + 1, :]
    o_ref[...] = _ln_swish(acc, lw_ref[...], lb_ref[...]).astype(BF16)
    ns_ref[0:CONV_WIDTH - 2] = st_ref[1:CONV_WIDTH - 1]
    ns_ref[CONV_WIDTH - 2] = u


def _sample_conv_call(state_wbc, u, cw, cb, lw, lb):
    return pl.pallas_call(
        _sample_conv_kernel,
        out_shape=(jax.ShapeDtypeStruct(u.shape, BF16),
                   jax.ShapeDtypeStruct(state_wbc.shape, F32)),
        name="sample_conv",
    )(state_wbc, u, cw, cb, lw, lb)


def _lambda(lq1, lk1, lq2, lk2):
    s1 = jnp.sum(lq1 * lk1, axis=-1, keepdims=True)
    s2 = jnp.sum(lq2 * lk2, axis=-1, keepdims=True)
    return jnp.exp(s1) - jnp.exp(s2) + LAMBDA_INIT


def _head_out(o0, o1, lam, subln):
    o = o0 - lam * o1
    return _rms(o, subln) * (1.0 - LAMBDA_INIT)


ATT_TQ = 256
ATT_TK = 256
ATT_TK_WIDE = 512
META_BLK = 128
QK_AHEAD = 6
VT_ROWS = V_HEAD_DIM + 16


def _augment(x, pos, is_query, slope):
    lane = lax.broadcasted_iota(jnp.int32, x.shape, 1)
    hi = (pos >> 6).astype(F32)
    lo = (pos & 63).astype(F32)
    out = []
    for j in range(2):
        a = lane - (64 if j == 0 else 0)
        if is_query:
            aug = jnp.where(a == 0, hi * (-64.0 * slope),
                  jnp.where(a == 1, lo * (-slope),
                  jnp.where(a == 2, 64.0 * slope,
                  jnp.where(a == 3, slope, 0.0))))
        else:
            aug = jnp.where(a == 0, 1.0,
                  jnp.where(a == 1, 1.0,
                  jnp.where(a == 2, hi,
                  jnp.where(a == 3, lo, 0.0))))
        own = (lane < 64) if j == 0 else (lane >= 64)
        out.append(jnp.where(own, x, aug).astype(BF16))
    return out


N_CAST = 4


def _attn_kernel(q_ref, kb_ref, vb_ref, mkb_ref, mvb_ref,
                 lq1_ref, lk1_ref, lq2_ref, lk2_ref, subcol_ref, *rest):
    w_f32 = rest[:N_CAST]
    o_ref = rest[N_CAST]
    w_bf16 = rest[N_CAST + 1:2 * N_CAST + 1]
    kexp_ref, vt_ref, qexp_ref, m_ref, acc_ref = rest[2 * N_CAST + 1:]
    qi = pl.program_id(1)
    S = kb_ref.shape[1]

    for src, dst in zip(w_f32, w_bf16):
        dst[...] = src[...].astype(BF16)

    def chunk_step(parts, first):
        def scores(mm):
            mcols = slice(mm * 128, (mm + 1) * 128)
            out = []
            for k_rows, _, mask in parts:
                s = lax.dot_general(kexp_ref[k_rows, mcols], qexp_ref[:, mcols],
                                    (((1,), (1,)), ((), ())), preferred_element_type=F32)
                out.append(s if mask is None else jnp.where(mask, s, NEG_INF))
            return out

        pending = [scores(mm) for mm in range(QK_AHEAD)]
        for mm in range(N_MAPS):
            h = mm // 2
            if mm + QK_AHEAD < N_MAPS:
                pending.append(scores(mm + QK_AHEAD))
            m_new = None if first else m_ref[mm]
            for s in pending[mm]:
                m_cur = jnp.max(s, axis=0, keepdims=True)
                m_new = m_cur if m_new is None else jnp.maximum(m_new, m_cur)
            pv = None
            for s, (_, v_cols, _) in zip(pending[mm], parts):
                vt = vt_ref[h * VT_ROWS:(h + 1) * VT_ROWS, v_cols]
                pb = jnp.exp(s - m_new).astype(BF16)
                if pb.shape[0] < vt.shape[1]:
                    pb = jnp.concatenate(
                        [pb, jnp.zeros((vt.shape[1] - pb.shape[0], pb.shape[1]), BF16)], axis=0)
                term = jnp.dot(vt, pb, preferred_element_type=F32)
                pv = term if pv is None else pv + term
            if first:
                acc_ref[mm] = pv
            else:
                acc_ref[mm] = jnp.exp(m_ref[mm] - m_new) * acc_ref[mm] + pv
            m_ref[mm] = m_new

    @pl.when(qi == 0)
    def _():
        ones_rows = (lax.broadcasted_iota(jnp.int32, (VT_ROWS - V_HEAD_DIM, ATT_TK), 0) == 0)

        def put_vt(cols_off, vc):
            n = vc.shape[0]
            for h in range(N_HEADS):
                vt_ref[h * VT_ROWS:h * VT_ROWS + V_HEAD_DIM, cols_off:cols_off + n] = (
                    vc[:, h * 128:(h + 1) * 128].astype(F32).T.astype(BF16))
                vt_ref[h * VT_ROWS + V_HEAD_DIM:(h + 1) * VT_ROWS, cols_off:cols_off + n] = (
                    ones_rows[:, :n].astype(F32).astype(BF16))

        put_vt(0, mvb_ref[...])
        for c in range(S // ATT_TK):
            put_vt(META_BLK + c * ATT_TK, vb_ref[0, c * ATT_TK:(c + 1) * ATT_TK, :])

        def put(rows_off, kc, pos0):
            rows = kc.shape[0]
            pos = pos0 + lax.broadcasted_iota(jnp.int32, (rows, 128), 0)
            for h in range(N_HEADS):
                e0, e1 = _augment(kc[:, h * 128:(h + 1) * 128].astype(F32), pos, False, 0.0)
                kexp_ref[rows_off:rows_off + rows, (2 * h) * 128:(2 * h + 1) * 128] = e0
                kexp_ref[rows_off:rows_off + rows, (2 * h + 1) * 128:(2 * h + 2) * 128] = e1

        put(0, mkb_ref[...], 0)
        for c in range(S // ATT_TK):
            put(META_BLK + c * ATT_TK, kb_ref[0, c * ATT_TK:(c + 1) * ATT_TK, :],
                N_META + c * ATT_TK)

    qpos = N_META + qi * ATT_TQ + lax.broadcasted_iota(jnp.int32, (ATT_TQ, 128), 0)
    q = q_ref[0].astype(F32)
    for h in range(N_HEADS):
        e0, e1 = _augment(q[:, h * 128:(h + 1) * 128], qpos, True, SLOPES[h])
        qexp_ref[:, (2 * h) * 128:(2 * h + 1) * 128] = e0
        qexp_ref[:, (2 * h + 1) * 128:(2 * h + 2) * 128] = e1

    lam = _lambda(lq1_ref[...], lk1_ref[...], lq2_ref[...], lk2_ref[...])
    key = lax.broadcasted_iota(jnp.int32, (ATT_TK, ATT_TQ), 0)
    qry = lax.broadcasted_iota(jnp.int32, (ATT_TK, ATT_TQ), 1)
    causal = key <= qry

    def real_part(first_key, width, mask):
        r = pl.multiple_of(META_BLK + first_key, 128)
        return (pl.ds(r, width), pl.ds(r, width), mask)

    chunk_step([(slice(0, N_META), slice(0, META_BLK), None),
                real_part(qi * ATT_TK, ATT_TK, causal)], True)

    def body(c, carry):
        chunk_step([real_part(c * ATT_TK_WIDE, ATT_TK_WIDE, None)], False)
        return carry

    n_wide = lax.div(qi * ATT_TK, ATT_TK_WIDE)
    lax.fori_loop(0, n_wide, body, 0)

    @pl.when(n_wide * ATT_TK_WIDE < qi * ATT_TK)
    def _():
        chunk_step([real_part(n_wide * ATT_TK_WIDE, ATT_TK, None)], False)

    for h in range(N_HEADS):
        a0, a1 = acc_ref[2 * h], acc_ref[2 * h + 1]
        o0 = a0[:V_HEAD_DIM] / a0[V_HEAD_DIM:V_HEAD_DIM + 1]
        o1 = a1[:V_HEAD_DIM] / a1[V_HEAD_DIM:V_HEAD_DIM + 1]
        o = o0 - lam * o1
        o = o * lax.rsqrt(jnp.mean(o * o, axis=0, keepdims=True) + EPS)
        o = o * subcol_ref[...] * (1.0 - LAMBDA_INIT)
        o_ref[0, :, h * 128:(h + 1) * 128] = o.T.astype(BF16)


def _attn_call(q, kb, vb, mkb, mvb, lq1, lk1, lq2, lk2, subln, weights):
    B, S, _ = q.shape
    nq = S // ATT_TQ
    steps = B * nq
    assert len(weights) == N_CAST
    per_b = pl.BlockSpec((1, S, 512), lambda b, i: (b, 0, 0))
    tile = pl.BlockSpec((1, ATT_TQ, 512), lambda b, i: (b, i, 0))
    small = _const_spec((1, HEAD_DIM))

    def slab_spec(w):
        rows = next(r for r in range(16, w.shape[0] + 1, 16)
                    if w.shape[0] % r == 0 and w.shape[0] // r <= steps)
        last = w.shape[0] // rows - 1
        return pl.BlockSpec((rows, w.shape[1]), lambda b, i: (jnp.minimum(b * nq + i, last), 0))

    slab_specs = [slab_spec(w) for w in weights]
    out = pl.pallas_call(
        _attn_kernel,
        grid=(B, nq),
        in_specs=[tile, per_b, per_b,
                  _const_spec((META_BLK, 512)), _const_spec((META_BLK, 512)),
                  small, small, small, small, _const_spec((V_HEAD_DIM, 1))] + slab_specs,
        out_specs=[tile] + slab_specs,
        out_shape=[jax.ShapeDtypeStruct((B, S, 512), BF16)]
                  + [jax.ShapeDtypeStruct(w.shape, BF16) for w in weights],
        scratch_shapes=[pltpu.VMEM((META_BLK + S, N_MAPS * 128), BF16),
                        pltpu.VMEM((N_HEADS * VT_ROWS, META_BLK + S), BF16),
                        pltpu.VMEM((ATT_TQ, N_MAPS * 128), BF16),
                        pltpu.VMEM((N_MAPS, 1, ATT_TQ), F32),
                        pltpu.VMEM((N_MAPS, VT_ROWS, ATT_TQ), F32)],
        name="attn",
        compiler_params=pltpu.CompilerParams(
            dimension_semantics=("arbitrary", "arbitrary"),
            vmem_limit_bytes=VMEM_LIMIT_BYTES),
    )(q, kb, vb, mkb, mvb, lq1, lk1, lq2, lk2, subln, *weights)
    return out[0], out[1:]


KEYS_PER_VBLK = 128 // N_HEADS


def _decode_consts():
    sub = lax.broadcasted_iota(jnp.int32, (N_MAPS, PAGE_SIZE), 0)
    lane = lax.broadcasted_iota(jnp.int32, (N_MAPS, PAGE_SIZE), 1)
    head = sub >> 1
    slope = jnp.where(head == 0, SLOPES[0],
            jnp.where(head == 1, SLOPES[1],
            jnp.where(head == 2, SLOPES[2], SLOPES[3])))
    return dict(lane=lane, head=head, slope=slope,
                own_head=(lane & (N_HEADS - 1)) == head, key_of_row=lane >> 2)


def _decode_init(q_row, k_new, v_new, head):
    sub_w = lax.broadcasted_iota(jnp.int32, (N_MAPS, ATTN_WIDTH), 0)
    lane_w = lax.broadcasted_iota(jnp.int32, (N_MAPS, ATTN_WIDTH), 1)
    qbd = jnp.where((lane_w >> 6) == sub_w, jnp.broadcast_to(q_row, (N_MAPS, ATTN_WIDTH)), 0.0)
    m = jnp.sum(qbd * k_new, axis=-1, keepdims=True)
    acc = jnp.zeros((N_MAPS, V_HEAD_DIM), F32)
    for h in range(N_HEADS):
        acc = jnp.where(head == h, v_new[:, h * 128:(h + 1) * 128], acc)
    return m, jnp.ones((N_MAPS, 1), F32), acc


def _decode_probs(m, l, qcol, k_pages, first_page, past, cst):
    partial = [[None] * N_MAPS for _ in k_pages]
    for mm in range(N_MAPS):
        for blk in range(HEAD_DIM // 8):
            rows = slice(mm * HEAD_DIM + blk * 8, mm * HEAD_DIM + blk * 8 + 8)
            qv = qcol[rows, :]
            for p, kp in enumerate(k_pages):
                t = kp[rows, :] * qv
                partial[p][mm] = t if partial[p][mm] is None else partial[p][mm] + t
    scores = []
    for p in range(len(k_pages)):
        s = jnp.sum(jnp.stack(partial[p]), axis=1)
        dist = past - ((first_page + p) * PAGE_SIZE + cst["lane"])
        scores.append(s - cst["slope"] * dist.astype(F32))
    s = jnp.concatenate(scores, axis=1)
    m_new = jnp.maximum(m, jnp.max(s, axis=-1, keepdims=True))
    alpha = jnp.exp(m - m_new)
    pr = jnp.exp(s - m_new)
    l = alpha * l + jnp.sum(pr, axis=-1, keepdims=True)
    prb = pr.astype(BF16).astype(F32)
    spread_probs = []
    for p in range(len(k_pages)):
        prp = prb[:, p * PAGE_SIZE:(p + 1) * PAGE_SIZE]
        spread = [jnp.where(cst["own_head"],
                            jnp.take_along_axis(prp, cst["key_of_row"] + c * KEYS_PER_VBLK, axis=1),
                            0.0)
                  for c in range(N_HEADS)]
        spread_probs.append(jnp.concatenate(spread, axis=1).astype(BF16))
    return m_new, l, alpha, spread_probs


def _decode_values(acc, alpha, spread_probs, v_pages):
    pv = jnp.zeros((N_MAPS, V_HEAD_DIM), F32)
    for pe, vp in zip(spread_probs, v_pages):
        pv = pv + jnp.dot(pe, vp.astype(BF16), preferred_element_type=F32)
    return alpha * acc + pv


def _decode_out(l, acc, lam, subln, rows_ref, row):
    o = acc / l
    for h in range(N_HEADS):
        ho = _head_out(o[2 * h:2 * h + 1], o[2 * h + 1:2 * h + 2], lam, subln)
        rows_ref[row, :, h * 128:(h + 1) * 128] = ho


FFN_CHUNK = 256


N_FFN_CHUNKS = FFN_HIDDEN // FFN_CHUNK


def _finish_rows(a, c, x_res, wo_ref, gpost_ref, gpre_ref, gfpost_ref,
                 wg_ref, wu_ref, wd_ref, before_chunk=None, after_chunk=None):
    mix = (jnp.dot(a, wo_ref[0:ATTN_WIDTH, :], preferred_element_type=F32)
           + jnp.dot(c, wo_ref[ATTN_WIDTH:, :], preferred_element_type=F32))
    x = x_res + _rms(mix, gpost_ref[...])
    hb = _rms(x, gpre_ref[...]).astype(BF16)
    f = jnp.zeros(x.shape, F32)
    for c in range(N_FFN_CHUNKS):
        cols = slice(c * FFN_CHUNK, (c + 1) * FFN_CHUNK)
        if before_chunk is not None:
            before_chunk(c)
        gate = jnp.dot(hb, wg_ref[:, cols], preferred_element_type=F32)
        up = jnp.dot(hb, wu_ref[:, cols], preferred_element_type=F32)
        act = (gate * jax.nn.sigmoid(gate) * up).astype(BF16)
        f = f + jnp.dot(act, wd_ref[cols, :], preferred_element_type=F32)
        if after_chunk is not None:
            after_chunk(c)
    return x + _rms(f, gfpost_ref[...])


DEC_GROUP = 8
DEC_AHEAD = 2
DEC_SLOTS = DEC_AHEAD + 1
DEC_FIRST_CHUNK = 2


def _finish_decode_kernel(pt_ref, a_ref, c_ref, x_ref, wo_ref, gpost_ref, gpre_ref, gfpost_ref,
                          wg_ref, wu_ref, wd_ref,
                          q_ref, qcol_ref, kn_ref, vn_ref,
                          lq1_ref, lk1_ref, lq2_ref, lk2_ref, sub_ref, cs_ref, xs_ref,
                          ck_hbm, cv_hbm,
                          o_ref, ys_ref, kbuf, vbuf, sem, as_ref):
    i = pl.program_id(0)
    n_steps = pl.num_programs(0)
    n_pages = pt_ref.shape[1]
    n_groups = n_pages // DEC_GROUP
    past = n_pages * PAGE_SIZE

    def group_copies(b, g, slot):
        copies = []
        for p in range(DEC_GROUP):
            page = pt_ref[b, g * DEC_GROUP + p]
            copies.append(pltpu.make_async_copy(ck_hbm.at[page], kbuf.at[slot, p], sem.at[0, slot]))
            copies.append(pltpu.make_async_copy(cv_hbm.at[page], vbuf.at[slot, p], sem.at[1, slot]))
        return copies

    def wait_group(slot):
        pltpu.make_async_copy(ck_hbm.at[pl.ds(0, DEC_GROUP)], kbuf.at[slot], sem.at[0, slot]).wait()
        pltpu.make_async_copy(cv_hbm.at[pl.ds(0, DEC_GROUP)], vbuf.at[slot], sem.at[1, slot]).wait()

    def slot_of(g):
        return lax.rem(i * n_groups + g, DEC_SLOTS)

    def start_group(g):
        if g < n_groups:
            for cp in group_copies(i, g, slot_of(g)):
                cp.start()
        else:
            @pl.when(i + 1 < n_steps)
            def _():
                for cp in group_copies(i + 1, g - n_groups, slot_of(g)):
                    cp.start()

    @pl.when(i == 0)
    def _():
        for g in range(DEC_AHEAD):
            start_group(g)

    cst = _decode_consts()
    m, l, acc = _decode_init(q_ref[0], kn_ref[0], vn_ref[0], cst["head"])
    st = dict(m=m, l=l, acc=acc)

    def before_chunk(c):
        g = c - DEC_FIRST_CHUNK
        if not 0 <= g < n_groups:
            return
        start_group(g + DEC_AHEAD)
        slot = slot_of(g)
        wait_group(slot)
        st["m"], st["l"], st["alpha"], st["probs"] = _decode_probs(
            st["m"], st["l"], qcol_ref.at[0], [kbuf.at[slot, p] for p in range(DEC_GROUP)],
            g * DEC_GROUP, past, cst)

    def after_chunk(c):
        g = c - DEC_FIRST_CHUNK
        if not 0 <= g < n_groups:
            return
        slot = slot_of(g)
        st["acc"] = _decode_values(st["acc"], st["alpha"], st["probs"],
                                   [vbuf[slot, p] for p in range(DEC_GROUP)])

    weights = (wo_ref, gpost_ref, gpre_ref, gfpost_ref, wg_ref, wu_ref, wd_ref)
    o_ref[...] = _finish_rows(a_ref[...], c_ref[...], x_ref[...], *weights,
                              before_chunk, after_chunk)
    lam = _lambda(lq1_ref[...], lk1_ref[...], lq2_ref[...], lk2_ref[...])
    _decode_out(st["l"], st["acc"], lam, sub_ref[...], as_ref, i)

    @pl.when(i == n_steps - 1)
    def _():
        attn_rows = as_ref[...].reshape(as_ref.shape[0], ATTN_WIDTH).astype(BF16)
        ys_ref[...] = _finish_rows(attn_rows, cs_ref[...], xs_ref[...], *weights)


def _finish_decode_call(page_table, a, c, x, wo, gpost, gpre, gfpost, wg, wu, wd,
                        q, qcol, kn, vn, lq1, lk1, lq2, lk2, subln, conv_s, x_s, ck, cv, tm):
    rows = x.shape[0]
    Bd, n_pages = page_table.shape
    assert rows // tm == Bd and n_pages % DEC_GROUP == 0
    assert DEC_AHEAD <= n_pages // DEC_GROUP <= N_FFN_CHUNKS - DEC_FIRST_CHUNK
    half = pl.BlockSpec((tm, 512), lambda i, pt: (i, 0))
    full = pl.BlockSpec((tm, D_MODEL), lambda i, pt: (i, 0))
    vec = _const_spec((1, D_MODEL))
    row = pl.BlockSpec((1, 1, ATTN_WIDTH), lambda i, pt: (i, 0, 0))
    small = _const_spec((1, HEAD_DIM))
    hbm = pl.BlockSpec(memory_space=pl.ANY)
    grid_spec = pltpu.PrefetchScalarGridSpec(
        num_scalar_prefetch=1,
        grid=(Bd,),
        in_specs=[half, half, full, _const_spec(wo.shape), vec, vec, vec,
                  _const_spec(wg.shape), _const_spec(wu.shape), _const_spec(wd.shape),
                  row, pl.BlockSpec((1, ATTN_WIDTH, PAGE_SIZE), lambda i, pt: (i, 0, 0)),
                  row, row, small, small, small, small, _const_spec((1, V_HEAD_DIM)),
                  _const_spec(conv_s.shape), _const_spec(x_s.shape),
                  hbm, hbm],
        out_specs=(full, pl.BlockSpec((Bd, D_MODEL), lambda i, pt: (0, 0))),
        scratch_shapes=[pltpu.VMEM((DEC_SLOTS, DEC_GROUP, ATTN_WIDTH, PAGE_SIZE), F32),
                        pltpu.VMEM((DEC_SLOTS, DEC_GROUP, ATTN_WIDTH, PAGE_SIZE), F32),
                        pltpu.SemaphoreType.DMA((2, DEC_SLOTS)),
                        pltpu.VMEM((Bd, 1, ATTN_WIDTH), F32)],
    )
    return pl.pallas_call(
        _finish_decode_kernel,
        grid_spec=grid_spec,
        out_shape=(jax.ShapeDtypeStruct((rows, D_MODEL), F32),
                   jax.ShapeDtypeStruct((Bd, D_MODEL), F32)),
        name="finish_decode",
        compiler_params=pltpu.CompilerParams(
            dimension_semantics=("arbitrary",),
            vmem_limit_bytes=VMEM_LIMIT_BYTES),
    )(page_table, a, c, x, wo, gpost, gpre, gfpost, wg, wu, wd,
      q, qcol, kn, vn, lq1, lk1, lq2, lk2, subln, conv_s, x_s, ck, cv)


def kernel(x_prompt, x_sample, cache_k, cache_v, state_conv, page_table, meta_tokens,
           ln_mix_pre, ln_mix_post, w_in, lambda_q1, lambda_k1, lambda_q2, lambda_k2,
           subln_w, conv_w, conv_b, conv_ln_w, conv_ln_b, w_out, ln_ffn_pre, ln_ffn_post,
           w_gate, w_up, w_down):
    B, S, _ = x_prompt.shape
    Bd = x_sample.shape[0]
    T = S + N_META
    li = 0
    w_in_bf = w_in[li].astype(BF16)
    g_pre = ln_mix_pre[li][None]
    lam_args = (lambda_q1[li][None], lambda_k1[li][None], lambda_q2[li][None],
                lambda_k2[li][None], subln_w[li][None])
    conv_args = (conv_w[li], conv_b[li][None], conv_ln_w[li][None], conv_ln_b[li][None])

    small = _proj_call(jnp.concatenate([meta_tokens, x_sample.reshape(Bd, D_MODEL)], axis=0),
                       g_pre, w_in_bf)
    _, mk, mv, mu = (t[:N_META] for t in small)
    qs, ks, vs, us = (t[N_META:] for t in small)
    cw_tiles = jnp.broadcast_to(conv_w[li][:, None, :], (CONV_WIDTH, 8, CONV_DIM))
    q_bf, k_bf, v_bf, new_k, new_v, conv_o, new_conv = _prompt_proj_call(
        x_prompt, g_pre, w_in_bf, mk, mv, mu, cw_tiles, *conv_args[1:])
    pad = ((0, META_BLK - N_META), (0, 0))
    attn_o, (wo_bf, wg_bf, wu_bf, wd_bf) = _attn_call(
        q_bf, k_bf, v_bf, jnp.pad(mk.astype(BF16), pad), jnp.pad(mv.astype(BF16), pad),
        *lam_args[:4], subln_w[li][:, None],
        weights=(w_out[li], w_gate[li], w_up[li], w_down[li]))
    fin_args = (wo_bf, ln_mix_post[li][None], ln_ffn_pre[li][None], ln_ffn_post[li][None],
                wg_bf, wu_bf, wd_bf)

    n_pool = cache_k.shape[1]
    ck = jnp.transpose(cache_k[li], (0, 2, 3, 1)).reshape(n_pool, ATTN_WIDTH, PAGE_SIZE)
    cv = cache_v[li].reshape(n_pool, PAGE_SIZE * N_HEADS, V_HEAD_DIM)
    qcol = jnp.broadcast_to(qs[:, :, None], (Bd, ATTN_WIDTH, PAGE_SIZE))
    conv_s, new_state = _sample_conv_call(jnp.transpose(state_conv[li], (1, 0, 2)), us, *conv_args)
    y_prompt, y_sample = _finish_decode_call(
        page_table, attn_o.reshape(B * S, 512), conv_o.reshape(B * S, 512),
        x_prompt.reshape(B * S, D_MODEL), *fin_args,
        qs[:, None], qcol, ks[:, None], vs[:, None], *lam_args,
        conv_s, x_sample.reshape(Bd, D_MODEL), ck, cv, tm=B * S // Bd)

    return (y_prompt.reshape(B, S, D_MODEL),
            y_sample.reshape(Bd, 1, D_MODEL),
            new_k.reshape(1, B, T, N_MAPS, HEAD_DIM),
            new_v.reshape(1, B, T, N_HEADS, V_HEAD_DIM),
            new_conv[None],
            ks.reshape(1, Bd, 1, N_MAPS, HEAD_DIM),
            vs.reshape(1, Bd, 1, N_HEADS, V_HEAD_DIM),
            jnp.transpose(new_state, (1, 0, 2))[None])
```

```python
import math

import jax
import jax.numpy as jnp
from jax import lax
from jax.experimental import pallas as pl
from jax.experimental.pallas import tpu as pltpu

D_MODEL = 1024
N_META = 16
ATTN_WIDTH = 512
CONV_DIM = 512
N_HEADS = 4
HEAD_DIM = 64
V_HEAD_DIM = 128
N_MAPS = 2 * N_HEADS
CONV_WIDTH = 31
FFN_HIDDEN = 2816
PAGE_SIZE = 128
EPS = 1e-6
NEG_INF = -1e30
LAMBDA_INIT = 0.8 - 0.6 * math.exp(-0.3 * 0)
SCALE = HEAD_DIM ** -0.5
SLOPES = tuple(2.0 ** (-8.0 * (h + 1) / N_HEADS) for h in range(N_HEADS))

VMEM_LIMIT_BYTES = 56 * 1024 * 1024

F32 = jnp.float32
BF16 = jnp.bfloat16


def _rms(x, g):
    return x * lax.rsqrt(jnp.mean(x * x, axis=-1, keepdims=True) + EPS) * g


def _const_spec(shape):
    nd = len(shape)
    return pl.BlockSpec(shape, lambda *_: (0,) * nd, pipeline_mode=pl.Buffered(1))


def _project(x, g, w_ref):
    xn = _rms(x, g).astype(BF16)
    seg = lambda c: jnp.dot(xn, w_ref[:, c * 512:(c + 1) * 512], preferred_element_type=F32)
    q = seg(0) * SCALE
    k = seg(1)
    v = seg(2)
    u = seg(3) * jax.nn.sigmoid(seg(4))
    return q, k, v, u


def _ln_swish(y, w, b):
    mu = jnp.mean(y, axis=-1, keepdims=True)
    yc = y - mu
    z = yc * lax.rsqrt(jnp.mean(yc * yc, axis=-1, keepdims=True) + EPS) * w + b
    return z * jax.nn.sigmoid(z)


META_BLK = 128


def _prep_kernel(meta_ref, xs_ref, g_ref, w_ref, st_ref, cw_ref, cb_ref, lw_ref, lb_ref,
                 wbf_ref, mk_ref, mv_ref, mu_ref, mkb_ref, mvb_ref,
                 qs_ref, ks_ref, vs_ref, conv_ref, ns_ref):
    wbf_ref[...] = w_ref[...].astype(BF16)
    n_meta = meta_ref.shape[0]
    x = jnp.concatenate([meta_ref[...], xs_ref[...]], axis=0)
    q, k, v, u = _project(x, g_ref[...], wbf_ref)
    mk_ref[...] = k[:n_meta]
    mv_ref[...] = v[:n_meta]
    mu_ref[...] = u[:n_meta]
    pad = jnp.zeros((META_BLK - n_meta, ATTN_WIDTH), BF16)
    mkb_ref[...] = jnp.concatenate([k[:n_meta].astype(BF16), pad], axis=0)
    mvb_ref[...] = jnp.concatenate([v[:n_meta].astype(BF16), pad], axis=0)
    qs_ref[...] = q[n_meta:]
    ks_ref[...] = k[n_meta:]
    vs_ref[...] = v[n_meta:]

    us = u[n_meta:]
    acc = cb_ref[...] + us * cw_ref[CONV_WIDTH - 1:CONV_WIDTH, :]
    for w in range(CONV_WIDTH - 1):
        acc = acc + st_ref[w] * cw_ref[w:w + 1, :]
    conv_ref[...] = _ln_swish(acc, lw_ref[...], lb_ref[...]).astype(BF16)
    ns_ref[0:CONV_WIDTH - 2] = st_ref[1:CONV_WIDTH - 1]
    ns_ref[CONV_WIDTH - 2] = us


def _prep_call(meta, xs, g, w_in, state_wbc, cw, cb, lw, lb):
    n_meta, Bd = meta.shape[0], xs.shape[0]
    f32 = lambda rows: jax.ShapeDtypeStruct((rows, ATTN_WIDTH), F32)
    bf16 = lambda rows: jax.ShapeDtypeStruct((rows, ATTN_WIDTH), BF16)
    return pl.pallas_call(
        _prep_kernel,
        out_shape=(jax.ShapeDtypeStruct(w_in.shape, BF16),
                   f32(n_meta), f32(n_meta), f32(n_meta), bf16(META_BLK), bf16(META_BLK),
                   f32(Bd), f32(Bd), f32(Bd), bf16(Bd),
                   jax.ShapeDtypeStruct(state_wbc.shape, F32)),
        name="prep",
        compiler_params=pltpu.CompilerParams(vmem_limit_bytes=VMEM_LIMIT_BYTES),
    )(meta, xs, g, w_in, state_wbc, cw, cb, lw, lb)


PROJ_TM = 512
U_PAD = 48
U_ROWS_OFF = U_PAD - (CONV_WIDTH - 1)


CONV_SUB = 128
PROJ_RB = 256


def _zero_after(x):
    bits = pltpu.bitcast(x[0:8, 0:128], jnp.uint32)
    z = ((bits >> 16) >> 16).astype(F32)[0:1, :]
    return jnp.concatenate([z] * (CONV_DIM // 128), axis=1)


def _conv_rows(u_ref, win_start, cw_ref, bias):
    lead = U_ROWS_OFF % 8
    win = u_ref[pl.ds(win_start, CONV_SUB + 32), :]
    acc = jnp.broadcast_to(bias, (CONV_SUB, CONV_DIM))
    for res in range(8):
        rows = CONV_SUB + (8 if res else 0)
        part = None
        for w in range(CONV_WIDTH):
            if (lead + w) % 8 == res:
                off = lead + w - res
                term = win[off:off + rows, :].reshape(rows // 8, 8, CONV_DIM) * cw_ref[w]
                part = term if part is None else part + term
        acc = acc + part.reshape(rows, CONV_DIM)[res:res + CONV_SUB, :]
    return acc


def _store_value_rows(nv_ref, first_token, v):
    for h in range(N_HEADS):
        nv_ref[0, pl.ds(first_token * N_HEADS + h, v.shape[0], stride=N_HEADS), :] = (
            v[:, h * V_HEAD_DIM:(h + 1) * V_HEAD_DIM])


def _prompt_proj_kernel(x_ref, g_ref, w_ref, mk_ref, mv_ref, mu_ref,
                        cw_ref, cb_ref, lw_ref, lb_ref,
                        q_ref, kb_ref, vb_ref, nk_ref, nv_ref, co_ref, nc_ref,
                        u_ref, y_ref):
    i = pl.program_id(1)
    nt = pl.num_programs(1)

    @pl.when(i == 0)
    def _():
        nk_ref[0, 0:N_META, :] = mk_ref[...]
        _store_value_rows(nv_ref, 0, mv_ref[...])
        u_ref[0:U_PAD - N_META, :] = jnp.zeros((U_PAD - N_META, CONV_DIM), F32)
        u_ref[U_PAD - N_META:U_PAD, :] = mu_ref[...]

    xn = _rms(x_ref[0], g_ref[...]).astype(BF16)
    seg = lambda c: jnp.dot(xn, w_ref[:, c * 512:(c + 1) * 512], preferred_element_type=F32)
    r = i * PROJ_TM
    u_ref[pl.ds(pl.multiple_of(r + U_PAD, 8), PROJ_TM), :] = seg(3) * jax.nn.sigmoid(seg(4))

    win0 = r + U_ROWS_OFF - U_ROWS_OFF % 8
    n_sub = PROJ_TM // CONV_SUB

    def conv_passes(lo, hi, bias):
        for s in range(lo, hi):
            y_ref[s * CONV_SUB:(s + 1) * CONV_SUB, :] = _conv_rows(
                u_ref, pl.multiple_of(win0 + s * CONV_SUB, 8), cw_ref, bias)

    n_blk = PROJ_TM // PROJ_RB
    for rb in range(n_blk):
        rows = slice(rb * PROJ_RB, (rb + 1) * PROJ_RB)
        out_rows = pl.ds(pl.multiple_of(r + N_META + rb * PROJ_RB, 8), PROJ_RB)
        blk = lambda c: jnp.dot(xn[rows], w_ref[:, c * 512:(c + 1) * 512],
                                preferred_element_type=F32)
        q_ref[0, rows, :] = (blk(0) * SCALE).astype(BF16)
        k = blk(1)
        kb_ref[0, rows, :] = k.astype(BF16)
        nk_ref[0, out_rows, :] = k
        v = blk(2)
        vb_ref[0, rows, :] = v.astype(BF16)
        _store_value_rows(nv_ref, r + N_META + rb * PROJ_RB, v)
        conv_passes(rb * n_sub // n_blk, (rb + 1) * n_sub // n_blk, cb_ref[...] + _zero_after(v))
    co_ref[0] = _ln_swish(y_ref[...], lw_ref[...], lb_ref[...]).astype(BF16)

    @pl.when(i == nt - 1)
    def _():
        last = nt * PROJ_TM + U_PAD
        nc_ref[0] = u_ref[last - (CONV_WIDTH - 1):last, :]


def _prompt_proj_call(x, g, w_bf, mk, mv, mu, cw_tiles, cb, lw, lb):
    B, S, _ = x.shape
    T = S + N_META
    nt = S // PROJ_TM
    tile = pl.BlockSpec((1, PROJ_TM, 512), lambda b, i: (b, i, 0))
    full = lambda rows: pl.BlockSpec((1, rows, 512), lambda b, i: (b, 0, 0))
    bf = jax.ShapeDtypeStruct((B, S, 512), BF16)
    vec = _const_spec((1, CONV_DIM))
    return pl.pallas_call(
        _prompt_proj_kernel,
        grid=(B, nt),
        in_specs=[
            pl.BlockSpec((1, PROJ_TM, D_MODEL), lambda b, i: (b, i, 0)),
            _const_spec((1, D_MODEL)),
            _const_spec(w_bf.shape),
            _const_spec((N_META, 512)),
            _const_spec((N_META, 512)),
            _const_spec((N_META, 512)),
            _const_spec((CONV_WIDTH, 8, CONV_DIM)), vec, vec, vec,
        ],
        out_specs=(tile, tile, tile, full(T),
                   pl.BlockSpec((1, T * N_HEADS, V_HEAD_DIM), lambda b, i: (b, 0, 0)),
                   tile, full(CONV_WIDTH - 1)),
        out_shape=(bf, bf, bf,
                   jax.ShapeDtypeStruct((B, T, 512), F32),
                   jax.ShapeDtypeStruct((B, T * N_HEADS, V_HEAD_DIM), F32),
                   bf,
                   jax.ShapeDtypeStruct((B, CONV_WIDTH - 1, CONV_DIM), F32)),
        scratch_shapes=[pltpu.VMEM((S + U_PAD, CONV_DIM), F32),
                        pltpu.VMEM((PROJ_TM, CONV_DIM), F32)],
        name="prompt_proj",
        compiler_params=pltpu.CompilerParams(
            dimension_semantics=("arbitrary", "arbitrary"),
            vmem_limit_bytes=VMEM_LIMIT_BYTES),
    )(x, g, w_bf, mk, mv, mu, cw_tiles, cb, lw, lb)


def _lambda(lq1, lk1, lq2, lk2):
    s1 = jnp.sum(lq1 * lk1, axis=-1, keepdims=True)
    s2 = jnp.sum(lq2 * lk2, axis=-1, keepdims=True)
    return jnp.exp(s1) - jnp.exp(s2) + LAMBDA_INIT


def _head_out(o0, o1, lam, subln):
    o = o0 - lam * o1
    return _rms(o, subln) * (1.0 - LAMBDA_INIT)


ATT_TQ = 256
ATT_TK = 256
ATT_TK_WIDE = 512
QK_AHEAD = 6
VT_ROWS = V_HEAD_DIM + 16


def _augment(x, pos, is_query, slope):
    lane = lax.broadcasted_iota(jnp.int32, x.shape, 1)
    hi = (pos >> 6).astype(F32)
    lo = (pos & 63).astype(F32)
    out = []
    for j in range(2):
        a = lane - (64 if j == 0 else 0)
        if is_query:
            aug = jnp.where(a == 0, hi * (-64.0 * slope),
                  jnp.where(a == 1, lo * (-slope),
                  jnp.where(a == 2, 64.0 * slope,
                  jnp.where(a == 3, slope, 0.0))))
        else:
            aug = jnp.where(a == 0, 1.0,
                  jnp.where(a == 1, 1.0,
                  jnp.where(a == 2, hi,
                  jnp.where(a == 3, lo, 0.0))))
        own = (lane < 64) if j == 0 else (lane >= 64)
        out.append(jnp.where(own, x, aug).astype(BF16))
    return out


N_CAST = 4


def _attn_kernel(q_ref, kb_ref, vb_ref, mkb_ref, mvb_ref,
                 lq1_ref, lk1_ref, lq2_ref, lk2_ref, subcol_ref, *rest):
    w_f32 = rest[:N_CAST]
    o_ref = rest[N_CAST]
    w_bf16 = rest[N_CAST + 1:2 * N_CAST + 1]
    kexp_ref, vt_ref, qexp_ref, m_ref, acc_ref = rest[2 * N_CAST + 1:]
    qi = pl.program_id(1)
    S = kb_ref.shape[1]

    for src, dst in zip(w_f32, w_bf16):
        dst[...] = src[...].astype(BF16)

    def chunk_step(parts, first):
        def scores(mm):
            mcols = slice(mm * 128, (mm + 1) * 128)
            out = []
            for k_rows, _, mask in parts:
                s = lax.dot_general(kexp_ref[k_rows, mcols], qexp_ref[:, mcols],
                                    (((1,), (1,)), ((), ())), preferred_element_type=F32)
                out.append(s if mask is None else jnp.where(mask, s, NEG_INF))
            return out

        pending = [scores(mm) for mm in range(QK_AHEAD)]
        for mm in range(N_MAPS):
            h = mm // 2
            if mm + QK_AHEAD < N_MAPS:
                pending.append(scores(mm + QK_AHEAD))
            m_new = None if first else m_ref[mm]
            for s in pending[mm]:
                m_cur = jnp.max(s, axis=0, keepdims=True)
                m_new = m_cur if m_new is None else jnp.maximum(m_new, m_cur)
            pv = None
            for s, (_, v_cols, _) in zip(pending[mm], parts):
                vt = vt_ref[h * VT_ROWS:(h + 1) * VT_ROWS, v_cols]
                pb = jnp.exp(s - m_new).astype(BF16)
                if pb.shape[0] < vt.shape[1]:
                    pb = jnp.concatenate(
                        [pb, jnp.zeros((vt.shape[1] - pb.shape[0], pb.shape[1]), BF16)], axis=0)
                term = jnp.dot(vt, pb, preferred_element_type=F32)
                pv = term if pv is None else pv + term
            if first:
                acc_ref[mm] = pv
            else:
                acc_ref[mm] = jnp.exp(m_ref[mm] - m_new) * acc_ref[mm] + pv
            m_ref[mm] = m_new

    @pl.when(qi == 0)
    def _():
        ones_rows = (lax.broadcasted_iota(jnp.int32, (VT_ROWS - V_HEAD_DIM, ATT_TK), 0) == 0)

        def put_vt(cols_off, vc):
            n = vc.shape[0]
            for h in range(N_HEADS):
                vt_ref[h * VT_ROWS:h * VT_ROWS + V_HEAD_DIM, cols_off:cols_off + n] = (
                    vc[:, h * 128:(h + 1) * 128].astype(F32).T.astype(BF16))
                vt_ref[h * VT_ROWS + V_HEAD_DIM:(h + 1) * VT_ROWS, cols_off:cols_off + n] = (
                    ones_rows[:, :n].astype(F32).astype(BF16))

        put_vt(0, mvb_ref[...])
        for c in range(S // ATT_TK):
            put_vt(META_BLK + c * ATT_TK, vb_ref[0, c * ATT_TK:(c + 1) * ATT_TK, :])

        def put(rows_off, kc, pos0):
            rows = kc.shape[0]
            pos = pos0 + lax.broadcasted_iota(jnp.int32, (rows, 128), 0)
            for h in range(N_HEADS):
                e0, e1 = _augment(kc[:, h * 128:(h + 1) * 128].astype(F32), pos, False, 0.0)
                kexp_ref[rows_off:rows_off + rows, (2 * h) * 128:(2 * h + 1) * 128] = e0
                kexp_ref[rows_off:rows_off + rows, (2 * h + 1) * 128:(2 * h + 2) * 128] = e1

        put(0, mkb_ref[...], 0)
        for c in range(S // ATT_TK):
            put(META_BLK + c * ATT_TK, kb_ref[0, c * ATT_TK:(c + 1) * ATT_TK, :],
                N_META + c * ATT_TK)

    qpos = N_META + qi * ATT_TQ + lax.broadcasted_iota(jnp.int32, (ATT_TQ, 128), 0)
    q = q_ref[0].astype(F32)
    for h in range(N_HEADS):
        e0, e1 = _augment(q[:, h * 128:(h + 1) * 128], qpos, True, SLOPES[h])
        qexp_ref[:, (2 * h) * 128:(2 * h + 1) * 128] = e0
        qexp_ref[:, (2 * h + 1) * 128:(2 * h + 2) * 128] = e1

    lam = _lambda(lq1_ref[...], lk1_ref[...], lq2_ref[...], lk2_ref[...])
    key = lax.broadcasted_iota(jnp.int32, (ATT_TK, ATT_TQ), 0)
    qry = lax.broadcasted_iota(jnp.int32, (ATT_TK, ATT_TQ), 1)
    causal = key <= qry

    def real_part(first_key, width, mask):
        r = pl.multiple_of(META_BLK + first_key, 128)
        return (pl.ds(r, width), pl.ds(r, width), mask)

    chunk_step([(slice(0, N_META), slice(0, META_BLK), None),
                real_part(qi * ATT_TK, ATT_TK, causal)], True)

    def body(c, carry):
        chunk_step([real_part(c * ATT_TK_WIDE, ATT_TK_WIDE, None)], False)
        return carry

    n_wide = lax.div(qi * ATT_TK, ATT_TK_WIDE)
    lax.fori_loop(0, n_wide, body, 0)

    @pl.when(n_wide * ATT_TK_WIDE < qi * ATT_TK)
    def _():
        chunk_step([real_part(n_wide * ATT_TK_WIDE, ATT_TK, None)], False)

    for h in range(N_HEADS):
        a0, a1 = acc_ref[2 * h], acc_ref[2 * h + 1]
        o0 = a0[:V_HEAD_DIM] / a0[V_HEAD_DIM:V_HEAD_DIM + 1]
        o1 = a1[:V_HEAD_DIM] / a1[V_HEAD_DIM:V_HEAD_DIM + 1]
        o = o0 - lam * o1
        o = o * lax.rsqrt(jnp.mean(o * o, axis=0, keepdims=True) + EPS)
        o = o * subcol_ref[...] * (1.0 - LAMBDA_INIT)
        o_ref[0, :, h * 128:(h + 1) * 128] = o.T.astype(BF16)


def _attn_call(q, kb, vb, mkb, mvb, lq1, lk1, lq2, lk2, subln, weights):
    B, S, _ = q.shape
    nq = S // ATT_TQ
    steps = B * nq
    assert len(weights) == N_CAST
    per_b = pl.BlockSpec((1, S, 512), lambda b, i: (b, 0, 0))
    tile = pl.BlockSpec((1, ATT_TQ, 512), lambda b, i: (b, i, 0))
    small = _const_spec((1, HEAD_DIM))

    def slab_spec(w):
        rows = next(r for r in range(16, w.shape[0] + 1, 16)
                    if w.shape[0] % r == 0 and w.shape[0] // r <= steps)
        last = w.shape[0] // rows - 1
        return pl.BlockSpec((rows, w.shape[1]), lambda b, i: (jnp.minimum(b * nq + i, last), 0))

    slab_specs = [slab_spec(w) for w in weights]
    out = pl.pallas_call(
        _attn_kernel,
        grid=(B, nq),
        in_specs=[tile, per_b, per_b,
                  _const_spec((META_BLK, 512)), _const_spec((META_BLK, 512)),
                  small, small, small, small, _const_spec((V_HEAD_DIM, 1))] + slab_specs,
        out_specs=[tile] + slab_specs,
        out_shape=[jax.ShapeDtypeStruct((B, S, 512), BF16)]
                  + [jax.ShapeDtypeStruct(w.shape, BF16) for w in weights],
        scratch_shapes=[pltpu.VMEM((META_BLK + S, N_MAPS * 128), BF16),
                        pltpu.VMEM((N_HEADS * VT_ROWS, META_BLK + S), BF16),
                        pltpu.VMEM((ATT_TQ, N_MAPS * 128), BF16),
                        pltpu.VMEM((N_MAPS, 1, ATT_TQ), F32),
                        pltpu.VMEM((N_MAPS, VT_ROWS, ATT_TQ), F32)],
        name="attn",
        compiler_params=pltpu.CompilerParams(
            dimension_semantics=("arbitrary", "arbitrary"),
            vmem_limit_bytes=VMEM_LIMIT_BYTES),
    )(q, kb, vb, mkb, mvb, lq1, lk1, lq2, lk2, subln, *weights)
    return out[0], out[1:]


KEYS_PER_VBLK = 128 // N_HEADS


def _decode_consts():
    sub = lax.broadcasted_iota(jnp.int32, (N_MAPS, PAGE_SIZE), 0)
    lane = lax.broadcasted_iota(jnp.int32, (N_MAPS, PAGE_SIZE), 1)
    head = sub >> 1
    slope = jnp.where(head == 0, SLOPES[0],
            jnp.where(head == 1, SLOPES[1],
            jnp.where(head == 2, SLOPES[2], SLOPES[3])))
    return dict(lane=lane, head=head, slope=slope,
                own_head=(lane & (N_HEADS - 1)) == head, key_of_row=lane >> 2)


def _decode_init(q_row, k_new, v_new, head):
    sub_w = lax.broadcasted_iota(jnp.int32, (N_MAPS, ATTN_WIDTH), 0)
    lane_w = lax.broadcasted_iota(jnp.int32, (N_MAPS, ATTN_WIDTH), 1)
    qbd = jnp.where((lane_w >> 6) == sub_w, jnp.broadcast_to(q_row, (N_MAPS, ATTN_WIDTH)), 0.0)
    m = jnp.sum(qbd * k_new, axis=-1, keepdims=True)
    acc = jnp.zeros((N_MAPS, V_HEAD_DIM), F32)
    for h in range(N_HEADS):
        acc = jnp.where(head == h, v_new[:, h * 128:(h + 1) * 128], acc)
    return m, jnp.ones((N_MAPS, 1), F32), acc


def _decode_probs(m, l, qcol, k_pages, first_page, past, cst):
    partial = [[None] * N_MAPS for _ in k_pages]
    for mm in range(N_MAPS):
        for blk in range(HEAD_DIM // 8):
            rows = slice(mm * HEAD_DIM + blk * 8, mm * HEAD_DIM + blk * 8 + 8)
            qv = qcol[rows, :]
            for p, kp in enumerate(k_pages):
                t = kp[rows, :] * qv
                partial[p][mm] = t if partial[p][mm] is None else partial[p][mm] + t
    scores = []
    for p in range(len(k_pages)):
        s = jnp.sum(jnp.stack(partial[p]), axis=1)
        dist = past - ((first_page + p) * PAGE_SIZE + cst["lane"])
        scores.append(s - cst["slope"] * dist.astype(F32))
    s = jnp.concatenate(scores, axis=1)
    m_new = jnp.maximum(m, jnp.max(s, axis=-1, keepdims=True))
    alpha = jnp.exp(m - m_new)
    pr = jnp.exp(s - m_new)
    l = alpha * l + jnp.sum(pr, axis=-1, keepdims=True)
    prb = pr.astype(BF16).astype(F32)
    spread_probs = []
    for p in range(len(k_pages)):
        prp = prb[:, p * PAGE_SIZE:(p + 1) * PAGE_SIZE]
        spread = [jnp.where(cst["own_head"],
                            jnp.take_along_axis(prp, cst["key_of_row"] + c * KEYS_PER_VBLK, axis=1),
                            0.0)
                  for c in range(N_HEADS)]
        spread_probs.append(jnp.concatenate(spread, axis=1).astype(BF16))
    return m_new, l, alpha, spread_probs


def _decode_values(acc, alpha, spread_probs, v_pages):
    pv = jnp.zeros((N_MAPS, V_HEAD_DIM), F32)
    for pe, vp in zip(spread_probs, v_pages):
        pv = pv + jnp.dot(pe, vp.astype(BF16), preferred_element_type=F32)
    return alpha * acc + pv


def _decode_out(l, acc, lam, subln, rows_ref, row):
    o = acc / l
    for h in range(N_HEADS):
        ho = _head_out(o[2 * h:2 * h + 1], o[2 * h + 1:2 * h + 2], lam, subln)
        rows_ref[row, :, h * 128:(h + 1) * 128] = ho


FFN_CHUNK = 256


N_FFN_CHUNKS = FFN_HIDDEN // FFN_CHUNK


def _finish_rows(a, c, x_res, wo_ref, gpost_ref, gpre_ref, gfpost_ref,
                 wg_ref, wu_ref, wd_ref, before_chunk=None, after_chunk=None):
    mix = (jnp.dot(a, wo_ref[0:ATTN_WIDTH, :], preferred_element_type=F32)
           + jnp.dot(c, wo_ref[ATTN_WIDTH:, :], preferred_element_type=F32))
    x = x_res + _rms(mix, gpost_ref[...])
    hb = _rms(x, gpre_ref[...]).astype(BF16)
    f = jnp.zeros(x.shape, F32)
    for c in range(N_FFN_CHUNKS):
        cols = slice(c * FFN_CHUNK, (c + 1) * FFN_CHUNK)
        if before_chunk is not None:
            before_chunk(c)
        gate = jnp.dot(hb, wg_ref[:, cols], preferred_element_type=F32)
        up = jnp.dot(hb, wu_ref[:, cols], preferred_element_type=F32)
        act = (gate * jax.nn.sigmoid(gate) * up).astype(BF16)
        f = f + jnp.dot(act, wd_ref[cols, :], preferred_element_type=F32)
        if after_chunk is not None:
            after_chunk(c)
    return x + _rms(f, gfpost_ref[...])


DEC_GROUP = 8
DEC_AHEAD = 2
DEC_SLOTS = DEC_AHEAD + 1
DEC_FIRST_CHUNK = 2


def _finish_decode_kernel(pt_ref, a_ref, c_ref, x_ref, wo_ref, gpost_ref, gpre_ref, gfpost_ref,
                          wg_ref, wu_ref, wd_ref,
                          q_ref, kn_ref, vn_ref,
                          lq1_ref, lk1_ref, lq2_ref, lk2_ref, sub_ref, cs_ref, xs_ref,
                          ck_hbm, cv_hbm,
                          o_ref, ys_ref, kbuf, vbuf, sem, as_ref, qcol_ref):
    i = pl.program_id(0)
    qcol_ref[...] = jnp.broadcast_to(q_ref[0], (PAGE_SIZE, ATTN_WIDTH)).T
    n_steps = pl.num_programs(0)
    n_pages = pt_ref.shape[1]
    n_groups = n_pages // DEC_GROUP
    past = n_pages * PAGE_SIZE

    def group_copies(b, g, slot):
        copies = []
        for p in range(DEC_GROUP):
            page = pt_ref[b, g * DEC_GROUP + p]
            copies.append(pltpu.make_async_copy(ck_hbm.at[page], kbuf.at[slot, p], sem.at[0, slot]))
            copies.append(pltpu.make_async_copy(cv_hbm.at[page], vbuf.at[slot, p], sem.at[1, slot]))
        return copies

    def wait_group(slot):
        pltpu.make_async_copy(ck_hbm.at[pl.ds(0, DEC_GROUP)], kbuf.at[slot], sem.at[0, slot]).wait()
        pltpu.make_async_copy(cv_hbm.at[pl.ds(0, DEC_GROUP)], vbuf.at[slot], sem.at[1, slot]).wait()

    def slot_of(g):
        return lax.rem(i * n_groups + g, DEC_SLOTS)

    def start_group(g):
        if g < n_groups:
            for cp in group_copies(i, g, slot_of(g)):
                cp.start()
        else:
            @pl.when(i + 1 < n_steps)
            def _():
                for cp in group_copies(i + 1, g - n_groups, slot_of(g)):
                    cp.start()

    @pl.when(i == 0)
    def _():
        for g in range(DEC_AHEAD):
            start_group(g)

    cst = _decode_consts()
    m, l, acc = _decode_init(q_ref[0], kn_ref[0], vn_ref[0], cst["head"])
    st = dict(m=m, l=l, acc=acc)

    def before_chunk(c):
        g = c - DEC_FIRST_CHUNK
        if not 0 <= g < n_groups:
            return
        start_group(g + DEC_AHEAD)
        slot = slot_of(g)
        wait_group(slot)
        st["m"], st["l"], st["alpha"], st["probs"] = _decode_probs(
            st["m"], st["l"], qcol_ref, [kbuf.at[slot, p] for p in range(DEC_GROUP)],
            g * DEC_GROUP, past, cst)

    def after_chunk(c):
        g = c - DEC_FIRST_CHUNK
        if not 0 <= g < n_groups:
            return
        slot = slot_of(g)
        st["acc"] = _decode_values(st["acc"], st["alpha"], st["probs"],
                                   [vbuf[slot, p] for p in range(DEC_GROUP)])

    weights = (wo_ref, gpost_ref, gpre_ref, gfpost_ref, wg_ref, wu_ref, wd_ref)
    o_ref[...] = _finish_rows(a_ref[...], c_ref[...], x_ref[...], *weights,
                              before_chunk, after_chunk)
    lam = _lambda(lq1_ref[...], lk1_ref[...], lq2_ref[...], lk2_ref[...])
    _decode_out(st["l"], st["acc"], lam, sub_ref[...], as_ref, i)

    @pl.when(i == n_steps - 1)
    def _():
        attn_rows = as_ref[...].reshape(as_ref.shape[0], ATTN_WIDTH).astype(BF16)
        ys_ref[...] = _finish_rows(attn_rows, cs_ref[...], xs_ref[...], *weights)


def _finish_decode_call(page_table, a, c, x, wo, gpost, gpre, gfpost, wg, wu, wd,
                        q, kn, vn, lq1, lk1, lq2, lk2, subln, conv_s, x_s, ck, cv, tm):
    rows = x.shape[0]
    Bd, n_pages = page_table.shape
    assert rows // tm == Bd and n_pages % DEC_GROUP == 0
    assert DEC_AHEAD <= n_pages // DEC_GROUP <= N_FFN_CHUNKS - DEC_FIRST_CHUNK
    half = pl.BlockSpec((tm, 512), lambda i, pt: (i, 0))
    full = pl.BlockSpec((tm, D_MODEL), lambda i, pt: (i, 0))
    vec = _const_spec((1, D_MODEL))
    row = pl.BlockSpec((1, 1, ATTN_WIDTH), lambda i, pt: (i, 0, 0))
    small = _const_spec((1, HEAD_DIM))
    hbm = pl.BlockSpec(memory_space=pl.ANY)
    grid_spec = pltpu.PrefetchScalarGridSpec(
        num_scalar_prefetch=1,
        grid=(Bd,),
        in_specs=[half, half, full, _const_spec(wo.shape), vec, vec, vec,
                  _const_spec(wg.shape), _const_spec(wu.shape), _const_spec(wd.shape),
                  row, row, row, small, small, small, small, _const_spec((1, V_HEAD_DIM)),
                  _const_spec(conv_s.shape), _const_spec(x_s.shape),
                  hbm, hbm],
        out_specs=(full, pl.BlockSpec((Bd, D_MODEL), lambda i, pt: (0, 0))),
        scratch_shapes=[pltpu.VMEM((DEC_SLOTS, DEC_GROUP, ATTN_WIDTH, PAGE_SIZE), F32),
                        pltpu.VMEM((DEC_SLOTS, DEC_GROUP, ATTN_WIDTH, PAGE_SIZE), F32),
                        pltpu.SemaphoreType.DMA((2, DEC_SLOTS)),
                        pltpu.VMEM((Bd, 1, ATTN_WIDTH), F32),
                        pltpu.VMEM((ATTN_WIDTH, PAGE_SIZE), F32)],
    )
    return pl.pallas_call(
        _finish_decode_kernel,
        grid_spec=grid_spec,
        out_shape=(jax.ShapeDtypeStruct((rows, D_MODEL), F32),
                   jax.ShapeDtypeStruct((Bd, D_MODEL), F32)),
        name="finish_decode",
        compiler_params=pltpu.CompilerParams(
            dimension_semantics=("arbitrary",),
            vmem_limit_bytes=VMEM_LIMIT_BYTES),
    )(page_table, a, c, x, wo, gpost, gpre, gfpost, wg, wu, wd,
      q, kn, vn, lq1, lk1, lq2, lk2, subln, conv_s, x_s, ck, cv)


def kernel(x_prompt, x_sample, cache_k, cache_v, state_conv, page_table, meta_tokens,
           ln_mix_pre, ln_mix_post, w_in, lambda_q1, lambda_k1, lambda_q2, lambda_k2,
           subln_w, conv_w, conv_b, conv_ln_w, conv_ln_b, w_out, ln_ffn_pre, ln_ffn_post,
           w_gate, w_up, w_down):
    B, S, _ = x_prompt.shape
    Bd = x_sample.shape[0]
    T = S + N_META
    li = 0
    g_pre = ln_mix_pre[li][None]
    lam_args = (lambda_q1[li][None], lambda_k1[li][None], lambda_q2[li][None],
                lambda_k2[li][None], subln_w[li][None])
    conv_args = (conv_w[li], conv_b[li][None], conv_ln_w[li][None], conv_ln_b[li][None])
    x_s = x_sample.reshape(Bd, D_MODEL)

    (w_in_bf, mk, mv, mu, mkb, mvb, qs, ks, vs, conv_s, new_state) = _prep_call(
        meta_tokens, x_s, g_pre, w_in[li], jnp.transpose(state_conv[li], (1, 0, 2)), *conv_args)

    cw_tiles = jnp.broadcast_to(conv_w[li][:, None, :], (CONV_WIDTH, 8, CONV_DIM))
    q_bf, k_bf, v_bf, new_k, new_v, conv_o, new_conv = _prompt_proj_call(
        x_prompt, g_pre, w_in_bf, mk, mv, mu, cw_tiles, *conv_args[1:])
    attn_o, (wo_bf, wg_bf, wu_bf, wd_bf) = _attn_call(
        q_bf, k_bf, v_bf, mkb, mvb, *lam_args[:4], subln_w[li][:, None],
        weights=(w_out[li], w_gate[li], w_up[li], w_down[li]))
    fin_args = (wo_bf, ln_mix_post[li][None], ln_ffn_pre[li][None], ln_ffn_post[li][None],
                wg_bf, wu_bf, wd_bf)

    n_pool = cache_k.shape[1]
    ck = jnp.transpose(cache_k[li], (0, 2, 3, 1)).reshape(n_pool, ATTN_WIDTH, PAGE_SIZE)
    cv = cache_v[li].reshape(n_pool, PAGE_SIZE * N_HEADS, V_HEAD_DIM)
    y_prompt, y_sample = _finish_decode_call(
        page_table, attn_o.reshape(B * S, 512), conv_o.reshape(B * S, 512),
        x_prompt.reshape(B * S, D_MODEL), *fin_args,
        qs[:, None], ks[:, None], vs[:, None], *lam_args,
        conv_s, x_s, ck, cv, tm=B * S // Bd)

    return (y_prompt.reshape(B, S, D_MODEL),
            y_sample.reshape(Bd, 1, D_MODEL),
            new_k.reshape(1, B, T, N_MAPS, HEAD_DIM),
            new_v.reshape(1, B, T, N_HEADS, V_HEAD_DIM),
            new_conv[None],
            ks.reshape(1, Bd, 1, N_MAPS, HEAD_DIM),
            vs.reshape(1, Bd, 1, N_HEADS, V_HEAD_DIM),
            jnp.transpose(new_state, (1, 0, 2))[None])
```

```python
import math

import jax
import jax.numpy as jnp
from jax import lax
from jax.experimental import pallas as pl
from jax.experimental.pallas import tpu as pltpu

D_MODEL = 1024
N_META = 16
ATTN_WIDTH = 512
CONV_DIM = 512
N_HEADS = 4
HEAD_DIM = 64
V_HEAD_DIM = 128
N_MAPS = 2 * N_HEADS
CONV_WIDTH = 31
FFN_HIDDEN = 2816
PAGE_SIZE = 128
EPS = 1e-6
NEG_INF = -1e30
LAMBDA_INIT = 0.8 - 0.6 * math.exp(-0.3 * 0)
SCALE = HEAD_DIM ** -0.5
SLOPES = tuple(2.0 ** (-8.0 * (h + 1) / N_HEADS) for h in range(N_HEADS))

VMEM_LIMIT_BYTES = 60 * 1024 * 1024

F32 = jnp.float32
BF16 = jnp.bfloat16


def _rms(x, g):
    return x * lax.rsqrt(jnp.mean(x * x, axis=-1, keepdims=True) + EPS) * g


def _const_spec(shape):
    nd = len(shape)
    return pl.BlockSpec(shape, lambda *_: (0,) * nd, pipeline_mode=pl.Buffered(1))


def _project(x, g, w_ref):
    xn = _rms(x, g).astype(BF16)
    seg = lambda c: jnp.dot(xn, w_ref[:, c * 512:(c + 1) * 512], preferred_element_type=F32)
    q = seg(0) * SCALE
    k = seg(1)
    v = seg(2)
    u = seg(3) * jax.nn.sigmoid(seg(4))
    return q, k, v, u


def _ln_swish(y, w, b):
    mu = jnp.mean(y, axis=-1, keepdims=True)
    yc = y - mu
    z = yc * lax.rsqrt(jnp.mean(yc * yc, axis=-1, keepdims=True) + EPS) * w + b
    return z * jax.nn.sigmoid(z)


META_BLK = 128


def _prep_kernel(meta_ref, xs_ref, g_ref, w_ref, st_ref, cw_ref, cb_ref, lw_ref, lb_ref,
                 wbf_ref, mk_ref, mv_ref, mu_ref, mkb_ref, mvb_ref,
                 qs_ref, ks_ref, vs_ref, conv_ref, ns_ref):
    wbf_ref[...] = w_ref[...].astype(BF16)
    n_meta = meta_ref.shape[0]
    x = jnp.concatenate([meta_ref[...], xs_ref[...]], axis=0)
    q, k, v, u = _project(x, g_ref[...], wbf_ref)
    mk_ref[...] = k[:n_meta]
    mv_ref[...] = v[:n_meta]
    mu_ref[...] = u[:n_meta]
    pad = jnp.zeros((META_BLK - n_meta, ATTN_WIDTH), BF16)
    mkb_ref[...] = jnp.concatenate([k[:n_meta].astype(BF16), pad], axis=0)
    mvb_ref[...] = jnp.concatenate([v[:n_meta].astype(BF16), pad], axis=0)
    qs_ref[...] = q[n_meta:]
    ks_ref[...] = k[n_meta:]
    vs_ref[...] = v[n_meta:]

    us = u[n_meta:]
    acc = cb_ref[...] + us * cw_ref[CONV_WIDTH - 1:CONV_WIDTH, :]
    for w in range(CONV_WIDTH - 1):
        acc = acc + st_ref[w] * cw_ref[w:w + 1, :]
    conv_ref[...] = _ln_swish(acc, lw_ref[...], lb_ref[...]).astype(BF16)
    ns_ref[0:CONV_WIDTH - 2] = st_ref[1:CONV_WIDTH - 1]
    ns_ref[CONV_WIDTH - 2] = us


def _prep_call(meta, xs, g, w_in, state_wbc, cw, cb, lw, lb):
    n_meta, Bd = meta.shape[0], xs.shape[0]
    f32 = lambda rows: jax.ShapeDtypeStruct((rows, ATTN_WIDTH), F32)
    bf16 = lambda rows: jax.ShapeDtypeStruct((rows, ATTN_WIDTH), BF16)
    return pl.pallas_call(
        _prep_kernel,
        out_shape=(jax.ShapeDtypeStruct(w_in.shape, BF16),
                   f32(n_meta), f32(n_meta), f32(n_meta), bf16(META_BLK), bf16(META_BLK),
                   f32(Bd), f32(Bd), f32(Bd), bf16(Bd),
                   jax.ShapeDtypeStruct(state_wbc.shape, F32)),
        name="prep",
        compiler_params=pltpu.CompilerParams(vmem_limit_bytes=VMEM_LIMIT_BYTES),
    )(meta, xs, g, w_in, state_wbc, cw, cb, lw, lb)


PROJ_TM = 512
U_PAD = 48
U_ROWS_OFF = U_PAD - (CONV_WIDTH - 1)


CONV_SUB = 128
PROJ_RB = 256


def _zero_after(x):
    bits = pltpu.bitcast(x[0:8, 0:128], jnp.uint32)
    z = ((bits >> 16) >> 16).astype(F32)[0:1, :]
    return jnp.concatenate([z] * (CONV_DIM // 128), axis=1)


def _conv_rows(u_ref, win_start, cw_ref, bias):
    lead = U_ROWS_OFF % 8
    win = u_ref[pl.ds(win_start, CONV_SUB + 32), :]
    acc = jnp.broadcast_to(bias, (CONV_SUB, CONV_DIM))
    for res in range(8):
        rows = CONV_SUB + (8 if res else 0)
        part = None
        for w in range(CONV_WIDTH):
            if (lead + w) % 8 == res:
                off = lead + w - res
                term = win[off:off + rows, :].reshape(rows // 8, 8, CONV_DIM) * cw_ref[w]
                part = term if part is None else part + term
        acc = acc + part.reshape(rows, CONV_DIM)[res:res + CONV_SUB, :]
    return acc


def _store_value_rows(nv_ref, first_token, v):
    for h in range(N_HEADS):
        nv_ref[0, pl.ds(first_token * N_HEADS + h, v.shape[0], stride=N_HEADS), :] = (
            v[:, h * V_HEAD_DIM:(h + 1) * V_HEAD_DIM])


def _prompt_proj_kernel(x_ref, g_ref, w_ref, mk_ref, mv_ref, mu_ref,
                        cw_ref, cb_ref, lw_ref, lb_ref,
                        q_ref, kb_ref, vb_ref, nk_ref, nv_ref, co_ref, nc_ref,
                        u_ref, y_ref):
    i = pl.program_id(1)
    nt = pl.num_programs(1)

    @pl.when(i == 0)
    def _():
        nk_ref[0, 0:N_META, :] = mk_ref[...]
        _store_value_rows(nv_ref, 0, mv_ref[...])
        u_ref[0:U_PAD - N_META, :] = jnp.zeros((U_PAD - N_META, CONV_DIM), F32)
        u_ref[U_PAD - N_META:U_PAD, :] = mu_ref[...]

    xn = _rms(x_ref[0], g_ref[...]).astype(BF16)
    seg = lambda c: jnp.dot(xn, w_ref[:, c * 512:(c + 1) * 512], preferred_element_type=F32)
    r = i * PROJ_TM
    u_ref[pl.ds(pl.multiple_of(r + U_PAD, 8), PROJ_TM), :] = seg(3) * jax.nn.sigmoid(seg(4))

    win0 = r + U_ROWS_OFF - U_ROWS_OFF % 8
    n_sub = PROJ_TM // CONV_SUB

    def conv_passes(lo, hi, bias):
        for s in range(lo, hi):
            y_ref[s * CONV_SUB:(s + 1) * CONV_SUB, :] = _conv_rows(
                u_ref, pl.multiple_of(win0 + s * CONV_SUB, 8), cw_ref, bias)

    n_blk = PROJ_TM // PROJ_RB
    for rb in range(n_blk):
        rows = slice(rb * PROJ_RB, (rb + 1) * PROJ_RB)
        out_rows = pl.ds(pl.multiple_of(r + N_META + rb * PROJ_RB, 8), PROJ_RB)
        blk = lambda c: jnp.dot(xn[rows], w_ref[:, c * 512:(c + 1) * 512],
                                preferred_element_type=F32)
        q_ref[0, rows, :] = (blk(0) * SCALE).astype(BF16)
        k = blk(1)
        kb_ref[0, rows, :] = k.astype(BF16)
        nk_ref[0, out_rows, :] = k
        v = blk(2)
        vb_ref[0, rows, :] = v.astype(BF16)
        _store_value_rows(nv_ref, r + N_META + rb * PROJ_RB, v)
        conv_passes(rb * n_sub // n_blk, (rb + 1) * n_sub // n_blk, cb_ref[...] + _zero_after(v))
    co_ref[0] = _ln_swish(y_ref[...], lw_ref[...], lb_ref[...]).astype(BF16)

    @pl.when(i == nt - 1)
    def _():
        last = nt * PROJ_TM + U_PAD
        nc_ref[0] = u_ref[last - (CONV_WIDTH - 1):last, :]


def _prompt_proj_call(x, g, w_bf, mk, mv, mu, cw_tiles, cb, lw, lb):
    B, S, _ = x.shape
    T = S + N_META
    nt = S // PROJ_TM
    tile = pl.BlockSpec((1, PROJ_TM, 512), lambda b, i: (b, i, 0))
    full = lambda rows: pl.BlockSpec((1, rows, 512), lambda b, i: (b, 0, 0))
    bf = jax.ShapeDtypeStruct((B, S, 512), BF16)
    vec = _const_spec((1, CONV_DIM))
    return pl.pallas_call(
        _prompt_proj_kernel,
        grid=(B, nt),
        in_specs=[
            pl.BlockSpec((1, PROJ_TM, D_MODEL), lambda b, i: (b, i, 0)),
            _const_spec((1, D_MODEL)),
            _const_spec(w_bf.shape),
            _const_spec((N_META, 512)),
            _const_spec((N_META, 512)),
            _const_spec((N_META, 512)),
            _const_spec((CONV_WIDTH, 8, CONV_DIM)), vec, vec, vec,
        ],
        out_specs=(tile, tile, tile, full(T),
                   pl.BlockSpec((1, T * N_HEADS, V_HEAD_DIM), lambda b, i: (b, 0, 0)),
                   tile, full(CONV_WIDTH - 1)),
        out_shape=(bf, bf, bf,
                   jax.ShapeDtypeStruct((B, T, 512), F32),
                   jax.ShapeDtypeStruct((B, T * N_HEADS, V_HEAD_DIM), F32),
                   bf,
                   jax.ShapeDtypeStruct((B, CONV_WIDTH - 1, CONV_DIM), F32)),
        scratch_shapes=[pltpu.VMEM((S + U_PAD, CONV_DIM), F32),
                        pltpu.VMEM((PROJ_TM, CONV_DIM), F32)],
        name="prompt_proj",
        compiler_params=pltpu.CompilerParams(
            dimension_semantics=("arbitrary", "arbitrary"),
            vmem_limit_bytes=VMEM_LIMIT_BYTES),
    )(x, g, w_bf, mk, mv, mu, cw_tiles, cb, lw, lb)


def _lambda(lq1, lk1, lq2, lk2):
    s1 = jnp.sum(lq1 * lk1, axis=-1, keepdims=True)
    s2 = jnp.sum(lq2 * lk2, axis=-1, keepdims=True)
    return jnp.exp(s1) - jnp.exp(s2) + LAMBDA_INIT


def _head_out(o0, o1, lam, subln):
    o = o0 - lam * o1
    return _rms(o, subln) * (1.0 - LAMBDA_INIT)


ATT_TQ = 256
ATT_TK = 256
ATT_TK_WIDE = 512
QK_AHEAD = 6
VT_ROWS = V_HEAD_DIM + 16


def _augment(x, pos, is_query, slope):
    lane = lax.broadcasted_iota(jnp.int32, x.shape, 1)
    hi = (pos >> 6).astype(F32)
    lo = (pos & 63).astype(F32)
    out = []
    for j in range(2):
        a = lane - (64 if j == 0 else 0)
        if is_query:
            aug = jnp.where(a == 0, hi * (-64.0 * slope),
                  jnp.where(a == 1, lo * (-slope),
                  jnp.where(a == 2, 64.0 * slope,
                  jnp.where(a == 3, slope, 0.0))))
        else:
            aug = jnp.where(a == 0, 1.0,
                  jnp.where(a == 1, 1.0,
                  jnp.where(a == 2, hi,
                  jnp.where(a == 3, lo, 0.0))))
        own = (lane < 64) if j == 0 else (lane >= 64)
        out.append(jnp.where(own, x, aug).astype(BF16))
    return out


N_CAST = 4


def _attn_kernel(q_ref, kb_ref, vb_ref, mkb_ref, mvb_ref,
                 lq1_ref, lk1_ref, lq2_ref, lk2_ref, subcol_ref, *rest):
    w_f32 = rest[:N_CAST]
    o_ref = rest[N_CAST]
    w_bf16 = rest[N_CAST + 1:2 * N_CAST + 1]
    kexp_ref, vt_ref, qexp_ref, m_ref, acc_ref = rest[2 * N_CAST + 1:]
    qi = pl.program_id(1)
    S = kb_ref.shape[1]

    for src, dst in zip(w_f32, w_bf16):
        dst[...] = src[...].astype(BF16)

    def chunk_step(parts, first):
        def scores(mm):
            mcols = slice(mm * 128, (mm + 1) * 128)
            out = []
            for k_rows, _, mask in parts:
                s = lax.dot_general(kexp_ref[k_rows, mcols], qexp_ref[:, mcols],
                                    (((1,), (1,)), ((), ())), preferred_element_type=F32)
                out.append(s if mask is None else jnp.where(mask, s, NEG_INF))
            return out

        pending = [scores(mm) for mm in range(QK_AHEAD)]
        for mm in range(N_MAPS):
            h = mm // 2
            if mm + QK_AHEAD < N_MAPS:
                pending.append(scores(mm + QK_AHEAD))
            m_new = None if first else m_ref[mm]
            for s in pending[mm]:
                m_cur = jnp.max(s, axis=0, keepdims=True)
                m_new = m_cur if m_new is None else jnp.maximum(m_new, m_cur)
            pv = None
            for s, (_, v_cols, _) in zip(pending[mm], parts):
                vt = vt_ref[h * VT_ROWS:(h + 1) * VT_ROWS, v_cols]
                pb = jnp.exp(s - m_new).astype(BF16)
                if pb.shape[0] < vt.shape[1]:
                    pb = jnp.concatenate(
                        [pb, jnp.zeros((vt.shape[1] - pb.shape[0], pb.shape[1]), BF16)], axis=0)
                term = jnp.dot(vt, pb, preferred_element_type=F32)
                pv = term if pv is None else pv + term
            if first:
                acc_ref[mm] = pv
            else:
                acc_ref[mm] = jnp.exp(m_ref[mm] - m_new) * acc_ref[mm] + pv
            m_ref[mm] = m_new

    @pl.when(qi == 0)
    def _():
        ones_rows = (lax.broadcasted_iota(jnp.int32, (VT_ROWS - V_HEAD_DIM, ATT_TK), 0) == 0)

        def put_vt(cols_off, vc):
            n = vc.shape[0]
            for h in range(N_HEADS):
                vt_ref[h * VT_ROWS:h * VT_ROWS + V_HEAD_DIM, cols_off:cols_off + n] = (
                    vc[:, h * 128:(h + 1) * 128].astype(F32).T.astype(BF16))
                vt_ref[h * VT_ROWS + V_HEAD_DIM:(h + 1) * VT_ROWS, cols_off:cols_off + n] = (
                    ones_rows[:, :n].astype(F32).astype(BF16))

        put_vt(0, mvb_ref[...])
        for c in range(S // ATT_TK):
            put_vt(META_BLK + c * ATT_TK, vb_ref[0, c * ATT_TK:(c + 1) * ATT_TK, :])

        def put(rows_off, kc, pos0):
            rows = kc.shape[0]
            pos = pos0 + lax.broadcasted_iota(jnp.int32, (rows, 128), 0)
            for h in range(N_HEADS):
                e0, e1 = _augment(kc[:, h * 128:(h + 1) * 128].astype(F32), pos, False, 0.0)
                kexp_ref[rows_off:rows_off + rows, (2 * h) * 128:(2 * h + 1) * 128] = e0
                kexp_ref[rows_off:rows_off + rows, (2 * h + 1) * 128:(2 * h + 2) * 128] = e1

        put(0, mkb_ref[...], 0)
        for c in range(S // ATT_TK):
            put(META_BLK + c * ATT_TK, kb_ref[0, c * ATT_TK:(c + 1) * ATT_TK, :],
                N_META + c * ATT_TK)

    qpos = N_META + qi * ATT_TQ + lax.broadcasted_iota(jnp.int32, (ATT_TQ, 128), 0)
    q = q_ref[0].astype(F32)
    for h in range(N_HEADS):
        e0, e1 = _augment(q[:, h * 128:(h + 1) * 128], qpos, True, SLOPES[h])
        qexp_ref[:, (2 * h) * 128:(2 * h + 1) * 128] = e0
        qexp_ref[:, (2 * h + 1) * 128:(2 * h + 2) * 128] = e1

    lam = _lambda(lq1_ref[...], lk1_ref[...], lq2_ref[...], lk2_ref[...])
    key = lax.broadcasted_iota(jnp.int32, (ATT_TK, ATT_TQ), 0)
    qry = lax.broadcasted_iota(jnp.int32, (ATT_TK, ATT_TQ), 1)
    causal = key <= qry

    def real_part(first_key, width, mask):
        r = pl.multiple_of(META_BLK + first_key, 128)
        return (pl.ds(r, width), pl.ds(r, width), mask)

    chunk_step([(slice(0, N_META), slice(0, META_BLK), None),
                real_part(qi * ATT_TK, ATT_TK, causal)], True)

    def body(c, carry):
        chunk_step([real_part(c * ATT_TK_WIDE, ATT_TK_WIDE, None)], False)
        return carry

    n_wide = lax.div(qi * ATT_TK, ATT_TK_WIDE)
    lax.fori_loop(0, n_wide, body, 0)

    @pl.when(n_wide * ATT_TK_WIDE < qi * ATT_TK)
    def _():
        chunk_step([real_part(n_wide * ATT_TK_WIDE, ATT_TK, None)], False)

    for h in range(N_HEADS):
        a0, a1 = acc_ref[2 * h], acc_ref[2 * h + 1]
        o0 = a0[:V_HEAD_DIM] / a0[V_HEAD_DIM:V_HEAD_DIM + 1]
        o1 = a1[:V_HEAD_DIM] / a1[V_HEAD_DIM:V_HEAD_DIM + 1]
        o = o0 - lam * o1
        o = o * lax.rsqrt(jnp.mean(o * o, axis=0, keepdims=True) + EPS)
        o = o * subcol_ref[...] * (1.0 - LAMBDA_INIT)
        o_ref[0, :, h * 128:(h + 1) * 128] = o.T.astype(BF16)


def _attn_call(q, kb, vb, mkb, mvb, lq1, lk1, lq2, lk2, subln, weights):
    B, S, _ = q.shape
    nq = S // ATT_TQ
    steps = B * nq
    assert len(weights) == N_CAST
    per_b = pl.BlockSpec((1, S, 512), lambda b, i: (b, 0, 0))
    tile = pl.BlockSpec((1, ATT_TQ, 512), lambda b, i: (b, i, 0))
    small = _const_spec((1, HEAD_DIM))

    def slab_spec(w):
        rows = next(r for r in range(16, w.shape[0] + 1, 16)
                    if w.shape[0] % r == 0 and w.shape[0] // r <= steps)
        last = w.shape[0] // rows - 1
        return pl.BlockSpec((rows, w.shape[1]), lambda b, i: (jnp.minimum(b * nq + i, last), 0))

    slab_specs = [slab_spec(w) for w in weights]
    out = pl.pallas_call(
        _attn_kernel,
        grid=(B, nq),
        in_specs=[tile, per_b, per_b,
                  _const_spec((META_BLK, 512)), _const_spec((META_BLK, 512)),
                  small, small, small, small, _const_spec((V_HEAD_DIM, 1))] + slab_specs,
        out_specs=[tile] + slab_specs,
        out_shape=[jax.ShapeDtypeStruct((B, S, 512), BF16)]
                  + [jax.ShapeDtypeStruct(w.shape, BF16) for w in weights],
        scratch_shapes=[pltpu.VMEM((META_BLK + S, N_MAPS * 128), BF16),
                        pltpu.VMEM((N_HEADS * VT_ROWS, META_BLK + S), BF16),
                        pltpu.VMEM((ATT_TQ, N_MAPS * 128), BF16),
                        pltpu.VMEM((N_MAPS, 1, ATT_TQ), F32),
                        pltpu.VMEM((N_MAPS, VT_ROWS, ATT_TQ), F32)],
        name="attn",
        compiler_params=pltpu.CompilerParams(
            dimension_semantics=("arbitrary", "arbitrary"),
            vmem_limit_bytes=VMEM_LIMIT_BYTES),
    )(q, kb, vb, mkb, mvb, lq1, lk1, lq2, lk2, subln, *weights)
    return out[0], out[1:]


KEYS_PER_VBLK = 128 // N_HEADS


def _decode_consts():
    sub = lax.broadcasted_iota(jnp.int32, (N_MAPS, PAGE_SIZE), 0)
    lane = lax.broadcasted_iota(jnp.int32, (N_MAPS, PAGE_SIZE), 1)
    head = sub >> 1
    slope = jnp.where(head == 0, SLOPES[0],
            jnp.where(head == 1, SLOPES[1],
            jnp.where(head == 2, SLOPES[2], SLOPES[3])))
    return dict(lane=lane, head=head, slope=slope,
                own_head=(lane & (N_HEADS - 1)) == head, key_of_row=lane >> 2)


def _decode_init(q_row, k_new, v_new, head):
    sub_w = lax.broadcasted_iota(jnp.int32, (N_MAPS, ATTN_WIDTH), 0)
    lane_w = lax.broadcasted_iota(jnp.int32, (N_MAPS, ATTN_WIDTH), 1)
    qbd = jnp.where((lane_w >> 6) == sub_w, jnp.broadcast_to(q_row, (N_MAPS, ATTN_WIDTH)), 0.0)
    m = jnp.sum(qbd * k_new, axis=-1, keepdims=True)
    acc = jnp.zeros((N_MAPS, V_HEAD_DIM), F32)
    for h in range(N_HEADS):
        acc = jnp.where(head == h, v_new[:, h * 128:(h + 1) * 128], acc)
    return m, jnp.ones((N_MAPS, 1), F32), acc


def _decode_probs(m, l, qcol, k_pages, first_page, past, cst):
    partial = [[None] * N_MAPS for _ in k_pages]
    for mm in range(N_MAPS):
        for blk in range(HEAD_DIM // 8):
            rows = slice(mm * HEAD_DIM + blk * 8, mm * HEAD_DIM + blk * 8 + 8)
            qv = qcol[rows, :]
            for p, kp in enumerate(k_pages):
                t = kp[rows, :] * qv
                partial[p][mm] = t if partial[p][mm] is None else partial[p][mm] + t
    scores = []
    for p in range(len(k_pages)):
        s = jnp.sum(jnp.stack(partial[p]), axis=1)
        dist = past - ((first_page + p) * PAGE_SIZE + cst["lane"])
        scores.append(s - cst["slope"] * dist.astype(F32))
    s = jnp.concatenate(scores, axis=1)
    m_new = jnp.maximum(m, jnp.max(s, axis=-1, keepdims=True))
    alpha = jnp.exp(m - m_new)
    pr = jnp.exp(s - m_new)
    l = alpha * l + jnp.sum(pr, axis=-1, keepdims=True)
    prb = pr.astype(BF16).astype(F32)
    spread_probs = []
    for p in range(len(k_pages)):
        prp = prb[:, p * PAGE_SIZE:(p + 1) * PAGE_SIZE]
        spread = [jnp.where(cst["own_head"],
                            jnp.take_along_axis(prp, cst["key_of_row"] + c * KEYS_PER_VBLK, axis=1),
                            0.0)
                  for c in range(N_HEADS)]
        spread_probs.append(jnp.concatenate(spread, axis=1).astype(BF16))
    return m_new, l, alpha, spread_probs


def _decode_values(acc, alpha, spread_probs, v_pages):
    pv = jnp.zeros((N_MAPS, V_HEAD_DIM), F32)
    for pe, vp in zip(spread_probs, v_pages):
        pv = pv + jnp.dot(pe, vp.astype(BF16), preferred_element_type=F32)
    return alpha * acc + pv


def _decode_out(l, acc, lam, subln, rows_ref, row):
    o = acc / l
    for h in range(N_HEADS):
        ho = _head_out(o[2 * h:2 * h + 1], o[2 * h + 1:2 * h + 2], lam, subln)
        rows_ref[row, :, h * 128:(h + 1) * 128] = ho


FFN_CHUNK = 256


N_FFN_CHUNKS = FFN_HIDDEN // FFN_CHUNK


def _finish_rows(a, c, x_res, wo_ref, gpost_ref, gpre_ref, gfpost_ref,
                 wg_ref, wu_ref, wd_ref, before_chunk=None, after_chunk=None):
    mix = (jnp.dot(a, wo_ref[0:ATTN_WIDTH, :], preferred_element_type=F32)
           + jnp.dot(c, wo_ref[ATTN_WIDTH:, :], preferred_element_type=F32))
    x = x_res + _rms(mix, gpost_ref[...])
    hb = _rms(x, gpre_ref[...]).astype(BF16)
    f = jnp.zeros(x.shape, F32)
    for c in range(N_FFN_CHUNKS):
        cols = slice(c * FFN_CHUNK, (c + 1) * FFN_CHUNK)
        if before_chunk is not None:
            before_chunk(c)
        gate = jnp.dot(hb, wg_ref[:, cols], preferred_element_type=F32)
        up = jnp.dot(hb, wu_ref[:, cols], preferred_element_type=F32)
        act = (gate * jax.nn.sigmoid(gate) * up).astype(BF16)
        f = f + jnp.dot(act, wd_ref[cols, :], preferred_element_type=F32)
        if after_chunk is not None:
            after_chunk(c)
    return x + _rms(f, gfpost_ref[...])


DEC_GROUP = 8
DEC_AHEAD = 3
DEC_SLOTS = DEC_AHEAD + 1
DEC_FIRST_CHUNK = 2


def _finish_decode_kernel(pt_ref, a_ref, c_ref, x_ref, wo_ref, gpost_ref, gpre_ref, gfpost_ref,
                          wg_ref, wu_ref, wd_ref,
                          q_ref, kn_ref, vn_ref,
                          lq1_ref, lk1_ref, lq2_ref, lk2_ref, sub_ref, cs_ref, xs_ref,
                          ck_hbm, cv_hbm,
                          o_ref, ys_ref, kbuf, vbuf, sem, as_ref, qcol_ref):
    i = pl.program_id(0)
    qcol_ref[...] = jnp.broadcast_to(q_ref[0], (PAGE_SIZE, ATTN_WIDTH)).T
    n_steps = pl.num_programs(0)
    n_pages = pt_ref.shape[1]
    n_groups = n_pages // DEC_GROUP
    past = n_pages * PAGE_SIZE

    def group_copies(b, g, slot):
        copies = []
        for p in range(DEC_GROUP):
            page = pt_ref[b, g * DEC_GROUP + p]
            copies.append(pltpu.make_async_copy(ck_hbm.at[page], kbuf.at[slot, p], sem.at[0, slot]))
            copies.append(pltpu.make_async_copy(cv_hbm.at[page], vbuf.at[slot, p], sem.at[1, slot]))
        return copies

    def wait_group(slot):
        pltpu.make_async_copy(ck_hbm.at[pl.ds(0, DEC_GROUP)], kbuf.at[slot], sem.at[0, slot]).wait()
        pltpu.make_async_copy(cv_hbm.at[pl.ds(0, DEC_GROUP)], vbuf.at[slot], sem.at[1, slot]).wait()

    def slot_of(g):
        return lax.rem(i * n_groups + g, DEC_SLOTS)

    def start_group(g):
        if g < n_groups:
            for cp in group_copies(i, g, slot_of(g)):
                cp.start()
        else:
            @pl.when(i + 1 < n_steps)
            def _():
                for cp in group_copies(i + 1, g - n_groups, slot_of(g)):
                    cp.start()

    @pl.when(i == 0)
    def _():
        for g in range(DEC_AHEAD):
            start_group(g)

    cst = _decode_consts()
    m, l, acc = _decode_init(q_ref[0], kn_ref[0], vn_ref[0], cst["head"])
    st = dict(m=m, l=l, acc=acc)

    def before_chunk(c):
        g = c - DEC_FIRST_CHUNK
        if not 0 <= g < n_groups:
            return
        start_group(g + DEC_AHEAD)
        slot = slot_of(g)
        wait_group(slot)
        st["m"], st["l"], st["alpha"], st["probs"] = _decode_probs(
            st["m"], st["l"], qcol_ref, [kbuf.at[slot, p] for p in range(DEC_GROUP)],
            g * DEC_GROUP, past, cst)

    def after_chunk(c):
        g = c - DEC_FIRST_CHUNK
        if not 0 <= g < n_groups:
            return
        slot = slot_of(g)
        st["acc"] = _decode_values(st["acc"], st["alpha"], st["probs"],
                                   [vbuf[slot, p] for p in range(DEC_GROUP)])

    weights = (wo_ref, gpost_ref, gpre_ref, gfpost_ref, wg_ref, wu_ref, wd_ref)
    o_ref[...] = _finish_rows(a_ref[...], c_ref[...], x_ref[...], *weights,
                              before_chunk, after_chunk)
    lam = _lambda(lq1_ref[...], lk1_ref[...], lq2_ref[...], lk2_ref[...])
    _decode_out(st["l"], st["acc"], lam, sub_ref[...], as_ref, i)

    @pl.when(i == n_steps - 1)
    def _():
        attn_rows = as_ref[...].reshape(as_ref.shape[0], ATTN_WIDTH).astype(BF16)
        ys_ref[...] = _finish_rows(attn_rows, cs_ref[...], xs_ref[...], *weights)


def _finish_decode_call(page_table, a, c, x, wo, gpost, gpre, gfpost, wg, wu, wd,
                        q, kn, vn, lq1, lk1, lq2, lk2, subln, conv_s, x_s, ck, cv, tm):
    rows = x.shape[0]
    Bd, n_pages = page_table.shape
    assert rows // tm == Bd and n_pages % DEC_GROUP == 0
    assert DEC_AHEAD <= n_pages // DEC_GROUP <= N_FFN_CHUNKS - DEC_FIRST_CHUNK
    half = pl.BlockSpec((tm, 512), lambda i, pt: (i, 0))
    full = pl.BlockSpec((tm, D_MODEL), lambda i, pt: (i, 0))
    vec = _const_spec((1, D_MODEL))
    row = pl.BlockSpec((1, 1, ATTN_WIDTH), lambda i, pt: (i, 0, 0))
    small = _const_spec((1, HEAD_DIM))
    hbm = pl.BlockSpec(memory_space=pl.ANY)
    grid_spec = pltpu.PrefetchScalarGridSpec(
        num_scalar_prefetch=1,
        grid=(Bd,),
        in_specs=[half, half, full, _const_spec(wo.shape), vec, vec, vec,
                  _const_spec(wg.shape), _const_spec(wu.shape), _const_spec(wd.shape),
                  row, row, row, small, small, small, small, _const_spec((1, V_HEAD_DIM)),
                  _const_spec(conv_s.shape), _const_spec(x_s.shape),
                  hbm, hbm],
        out_specs=(full, pl.BlockSpec((Bd, D_MODEL), lambda i, pt: (0, 0))),
        scratch_shapes=[pltpu.VMEM((DEC_SLOTS, DEC_GROUP, ATTN_WIDTH, PAGE_SIZE), F32),
                        pltpu.VMEM((DEC_SLOTS, DEC_GROUP, ATTN_WIDTH, PAGE_SIZE), F32),
                        pltpu.SemaphoreType.DMA((2, DEC_SLOTS)),
                        pltpu.VMEM((Bd, 1, ATTN_WIDTH), F32),
                        pltpu.VMEM((ATTN_WIDTH, PAGE_SIZE), F32)],
    )
    return pl.pallas_call(
        _finish_decode_kernel,
        grid_spec=grid_spec,
        out_shape=(jax.ShapeDtypeStruct((rows, D_MODEL), F32),
                   jax.ShapeDtypeStruct((Bd, D_MODEL), F32)),
        name="finish_decode",
        compiler_params=pltpu.CompilerParams(
            dimension_semantics=("arbitrary",),
            vmem_limit_bytes=VMEM_LIMIT_BYTES),
    )(page_table, a, c, x, wo, gpost, gpre, gfpost, wg, wu, wd,
      q, kn, vn, lq1, lk1, lq2, lk2, subln, conv_s, x_s, ck, cv)


def kernel(x_prompt, x_sample, cache_k, cache_v, state_conv, page_table, meta_tokens,
           ln_mix_pre, ln_mix_post, w_in, lambda_q1, lambda_k1, lambda_q2, lambda_k2,
           subln_w, conv_w, conv_b, conv_ln_w, conv_ln_b, w_out, ln_ffn_pre, ln_ffn_post,
           w_gate, w_up, w_down):
    B, S, _ = x_prompt.shape
    Bd = x_sample.shape[0]
    T = S + N_META
    li = 0
    g_pre = ln_mix_pre[li][None]
    lam_args = (lambda_q1[li][None], lambda_k1[li][None], lambda_q2[li][None],
                lambda_k2[li][None], subln_w[li][None])
    conv_args = (conv_w[li], conv_b[li][None], conv_ln_w[li][None], conv_ln_b[li][None])
    x_s = x_sample.reshape(Bd, D_MODEL)

    (w_in_bf, mk, mv, mu, mkb, mvb, qs, ks, vs, conv_s, new_state) = _prep_call(
        meta_tokens, x_s, g_pre, w_in[li], jnp.transpose(state_conv[li], (1, 0, 2)), *conv_args)

    cw_tiles = jnp.broadcast_to(conv_w[li][:, None, :], (CONV_WIDTH, 8, CONV_DIM))
    q_bf, k_bf, v_bf, new_k, new_v, conv_o, new_conv = _prompt_proj_call(
        x_prompt, g_pre, w_in_bf, mk, mv, mu, cw_tiles, *conv_args[1:])
    attn_o, (wo_bf, wg_bf, wu_bf, wd_bf) = _attn_call(
        q_bf, k_bf, v_bf, mkb, mvb, *lam_args[:4], subln_w[li][:, None],
        weights=(w_out[li], w_gate[li], w_up[li], w_down[li]))
    fin_args = (wo_bf, ln_mix_post[li][None], ln_ffn_pre[li][None], ln_ffn_post[li][None],
                wg_bf, wu_bf, wd_bf)

    n_pool = cache_k.shape[1]
    ck = jnp.transpose(cache_k[li], (0, 2, 3, 1)).reshape(n_pool, ATTN_WIDTH, PAGE_SIZE)
    cv = cache_v[li].reshape(n_pool, PAGE_SIZE * N_HEADS, V_HEAD_DIM)
    y_prompt, y_sample = _finish_decode_call(
        page_table, attn_o.reshape(B * S, 512), conv_o.reshape(B * S, 512),
        x_prompt.reshape(B * S, D_MODEL), *fin_args,
        qs[:, None], ks[:, None], vs[:, None], *lam_args,
        conv_s, x_s, ck, cv, tm=B * S // Bd)

    return (y_prompt.reshape(B, S, D_MODEL),
            y_sample.reshape(Bd, 1, D_MODEL),
            new_k.reshape(1, B, T, N_MAPS, HEAD_DIM),
            new_v.reshape(1, B, T, N_HEADS, V_HEAD_DIM),
            new_conv[None],
            ks.reshape(1, Bd, 1, N_MAPS, HEAD_DIM),
            vs.reshape(1, Bd, 1, N_HEADS, V_HEAD_DIM),
            jnp.transpose(new_state, (1, 0, 2))[None])
```
